```python
import math
import jax, jax.numpy as jnp
from jax import lax
import numpy as np

D_MODEL = 2048
BATCH = 16
SEQ = 256
DEPTH = 2
DEC_BATCH = 2
DEC_SEQ = 2048
PAST_LEN = 256

GRID_W = 64
N_MIXERS = 2
N_A_LAYERS = (DEPTH + N_MIXERS - 1) // N_MIXERS
N_B_LAYERS = DEPTH // N_MIXERS
N_SUB = 3
A_HEAD_DIM = 128
A_HEADS = D_MODEL // A_HEAD_DIM
A_KV_HEADS = 4
B_QK_DIM = 64
B_V_DIM = 2 * B_QK_DIM
B_HEADS = D_MODEL // B_V_DIM
D_FF = ((8 * D_MODEL // 3 + 127) // 128) * 128
ROPE_THETA = 10000.0
Q_BLOCK = 128
EPS = 1e-6
MACARON_WEIGHT = 0.5
DEEPNORM_ALPHA = (2 * DEPTH) ** 0.25
DEEPNORM_BETA = (8 * DEPTH) ** -0.25

kernel_name = "hybrid_diffusion_gqa_diffattn_macaron_step"


def layer_norm(x, g, b):
    xf = x.astype(jnp.float32)
    mu = jnp.mean(xf, -1, keepdims=True)
    xc = xf - mu
    var = jnp.mean(xc * xc, -1, keepdims=True)
    return (xc * lax.rsqrt(var + EPS) * g + b).astype(x.dtype)


def rms_norm(x, g):
    xf = x.astype(jnp.float32)
    return (xf * lax.rsqrt(jnp.mean(xf * xf, -1, keepdims=True) + EPS) * g).astype(x.dtype)


def modulation(cond, w, b):
    m = jax.nn.silu(cond) @ w + b
    return m.reshape(cond.shape[0], N_SUB, 3, D_MODEL)


def modulate(x, m, s):
    shift = m[:, s, 0][:, None, :]
    scale = m[:, s, 1][:, None, :]
    gate = m[:, s, 2][:, None, :]
    return x * (1 + scale) + shift, gate


def post_norm(x, out, g, b):
    return layer_norm(DEEPNORM_ALPHA * x + out, g, b)


def swiglu(h, w_in, w_out):
    gu = h @ w_in
    g, u = jnp.split(gu, 2, axis=-1)
    return (jax.nn.silu(g) * u) @ w_out


def ffn_sublayer(x, m, s, w_in, w_out, g, b):
    h, gate = modulate(x, m, s)
    return post_norm(x, MACARON_WEIGHT * gate * swiglu(h, w_in, w_out), g, b)


def axial_rope(rows, dim):
    row = jnp.repeat(jnp.arange(rows, dtype=jnp.float32), GRID_W)
    col = jnp.tile(jnp.arange(GRID_W, dtype=jnp.float32), rows)
    quarter = dim // 4
    inv = ROPE_THETA ** (-jnp.arange(quarter, dtype=jnp.float32) / quarter)
    ang = jnp.concatenate([row[:, None] * inv, col[:, None] * inv], axis=-1)
    return jnp.cos(ang), jnp.sin(ang)


def apply_rope(x, cos, sin):
    shp = (cos.shape[0],) + (1,) * (x.ndim - 3) + (cos.shape[-1],)
    cos = cos.reshape(shp)
    sin = sin.reshape(shp)
    xf = x.astype(jnp.float32)
    half = x.shape[-1] // 2
    x1, x2 = xf[..., :half], xf[..., half:]
    return jnp.concatenate([x1 * cos - x2 * sin, x2 * cos + x1 * sin], axis=-1).astype(x.dtype)


def split_query_blocks(q):
    bsz, s = q.shape[0], q.shape[1]
    return q.reshape(bsz, s // Q_BLOCK, Q_BLOCK, *q.shape[2:]).swapaxes(0, 1)


def merge_query_blocks(o):
    nb, bsz, qb = o.shape[0], o.shape[1], o.shape[2]
    return o.swapaxes(0, 1).reshape(bsz, nb * qb, *o.shape[3:])


def gqa_attention(q, k, v):
    bsz, s, h, d = q.shape
    kvh = k.shape[2]
    qg = q.reshape(bsz, s, kvh, h // kvh, d)
    scale = d ** -0.5

    def block(qb):
        sc = jnp.einsum('bqhgd,bkhd->bhgqk', qb, k).astype(jnp.float32) * scale
        p = jax.nn.softmax(sc, axis=-1).astype(v.dtype)
        return jnp.einsum('bhgqk,bkhd->bqhgd', p, v)

    o = lax.map(block, split_query_blocks(qg))
    return merge_query_blocks(o).reshape(bsz, s, h * d)


def diff_attention(q, k, v, lam):
    scale = q.shape[-1] ** -0.5

    def block(qb):
        sc = jnp.einsum('bqhcd,bkhcd->cbhqk', qb, k).astype(jnp.float32) * scale
        p = jax.nn.softmax(sc, axis=-1)
        w = (p[0] - lam * p[1]).astype(v.dtype)
        return jnp.einsum('bhqk,bkhd->bqhd', w, v)

    o = lax.map(block, split_query_blocks(q))
    return merge_query_blocks(o)


def mixer_a_qkv(h, w_qkv, qn, kn):
    bsz, s, _ = h.shape
    qkv = h @ w_qkv
    q, k, v = jnp.split(qkv, [A_HEADS * A_HEAD_DIM, (A_HEADS + A_KV_HEADS) * A_HEAD_DIM], axis=-1)
    q = rms_norm(q.reshape(bsz, s, A_HEADS, A_HEAD_DIM), qn)
    k = rms_norm(k.reshape(bsz, s, A_KV_HEADS, A_HEAD_DIM), kn)
    v = v.reshape(bsz, s, A_KV_HEADS, A_HEAD_DIM)
    return q, k, v


def mixer_b_qkv(h, w_qkv):
    bsz, s, _ = h.shape
    qk_w = B_HEADS * 2 * B_QK_DIM
    q, k, v = jnp.split(h @ w_qkv, [qk_w, 2 * qk_w], axis=-1)
    q = q.reshape(bsz, s, B_HEADS, 2, B_QK_DIM)
    k = k.reshape(bsz, s, B_HEADS, 2, B_QK_DIM)
    v = v.reshape(bsz, s, B_HEADS, B_V_DIM)
    return q, k, v


def diff_lambda_init(layer_idx):
    return 0.8 - 0.6 * math.exp(-0.3 * layer_idx)


def diff_lambda(lam_p, lambda_init):
    lp = lam_p.astype(jnp.float32)
    return jnp.exp(jnp.sum(lp[0] * lp[1])) - jnp.exp(jnp.sum(lp[2] * lp[3])) + lambda_init


def diff_output(o, subln_g, lambda_init, w_o):
    bsz, s = o.shape[0], o.shape[1]
    o = rms_norm(o, subln_g) * (1.0 - lambda_init)
    return o.reshape(bsz, s, B_HEADS * B_V_DIM) @ w_o


def setup_inputs(seed: int = 0) -> dict:
    key = jax.random.key(seed)
    ks = jax.random.split(key, 24)
    f32 = jnp.float32
    nrm = lambda k, shp, sc: jax.random.normal(k, shp, f32) * sc
    qkv_a_w = (A_HEADS + 2 * A_KV_HEADS) * A_HEAD_DIM
    qkv_b_w = 2 * B_HEADS * 2 * B_QK_DIM + B_HEADS * B_V_DIM
    return {
        "x_prompt": nrm(ks[0], (BATCH, SEQ, D_MODEL), 1.0),
        "x_sample": nrm(ks[1], (DEC_BATCH, DEC_SEQ, D_MODEL), 1.0),
        "cache_a_k": nrm(ks[2], (DEC_BATCH, N_A_LAYERS, PAST_LEN, A_KV_HEADS, A_HEAD_DIM), 1.0),
        "cache_a_v": nrm(ks[3], (DEC_BATCH, N_A_LAYERS, PAST_LEN, A_KV_HEADS, A_HEAD_DIM), 1.0),
        "cache_b_k": nrm(ks[4], (DEC_BATCH, N_B_LAYERS, PAST_LEN, B_HEADS, 2, B_QK_DIM), 1.0),
        "cache_b_v": nrm(ks[5], (DEC_BATCH, N_B_LAYERS, PAST_LEN, B_HEADS, B_V_DIM), 1.0),
        "c": nrm(ks[6], (DEC_BATCH, D_MODEL), 1.0),
        "c_ctx": nrm(ks[7], (D_MODEL,), 1.0),
        "ada_w": nrm(ks[8], (DEPTH, D_MODEL, N_SUB * 3 * D_MODEL), 0.5 * D_MODEL ** -0.5),
        "ada_b": nrm(ks[9], (DEPTH, N_SUB * 3 * D_MODEL), 0.01),
        "ln_g": 1.0 + nrm(ks[10], (DEPTH, N_SUB, D_MODEL), 0.01),
        "ln_b": nrm(ks[11], (DEPTH, N_SUB, D_MODEL), 0.01),
        "ffn_w_in": nrm(ks[12], (DEPTH, 2, D_MODEL, 2 * D_FF), D_MODEL ** -0.5),
        "ffn_w_out": nrm(ks[13], (DEPTH, 2, D_FF, D_MODEL), DEEPNORM_BETA * D_FF ** -0.5),
        "a_w_qkv": nrm(ks[14], (N_A_LAYERS, D_MODEL, qkv_a_w), D_MODEL ** -0.5),
        "a_q_norm": 1.0 + nrm(ks[15], (N_A_LAYERS, A_HEAD_DIM), 0.01),
        "a_k_norm": 1.0 + nrm(ks[16], (N_A_LAYERS, A_HEAD_DIM), 0.01),
        "a_w_o": nrm(ks[17], (N_A_LAYERS, A_HEADS * A_HEAD_DIM, D_MODEL), DEEPNORM_BETA * (A_HEADS * A_HEAD_DIM) ** -0.5),
        "b_w_qkv": nrm(ks[18], (N_B_LAYERS, D_MODEL, qkv_b_w), D_MODEL ** -0.5),
        "b_lambda": nrm(ks[19], (N_B_LAYERS, 4, B_QK_DIM), 0.1),
        "b_subln": 1.0 + nrm(ks[20], (N_B_LAYERS, B_V_DIM), 0.01),
        "b_w_o": nrm(ks[21], (N_B_LAYERS, B_HEADS * B_V_DIM, D_MODEL), DEEPNORM_BETA * (B_HEADS * B_V_DIM) ** -0.5),
    }


def reference(x_prompt, x_sample, cache_a_k, cache_a_v, cache_b_k, cache_b_v, c, c_ctx,
              ada_w, ada_b, ln_g, ln_b, ffn_w_in, ffn_w_out,
              a_w_qkv, a_q_norm, a_k_norm, a_w_o,
              b_w_qkv, b_lambda, b_subln, b_w_o):
    x = x_prompt
    new_a_k, new_a_v, new_b_k, new_b_v = [], [], [], []
    for i in range(DEPTH):
        j = i // N_MIXERS
        m = modulation(c_ctx[None, :], ada_w[i], ada_b[i])
        x = ffn_sublayer(x, m, 0, ffn_w_in[i, 0], ffn_w_out[i, 0], ln_g[i, 0], ln_b[i, 0])
        h, gate = modulate(x, m, 1)
        if i % N_MIXERS == 0:
            q, k, v = mixer_a_qkv(h, a_w_qkv[j], a_q_norm[j], a_k_norm[j])
            y = gqa_attention(q, k, v) @ a_w_o[j]
            new_a_k.append(k)
            new_a_v.append(v)
        else:
            lam_init = diff_lambda_init(i)
            q, k, v = mixer_b_qkv(h, b_w_qkv[j])
            o = diff_attention(q, k, v, diff_lambda(b_lambda[j], lam_init))
            y = diff_output(o, b_subln[j], lam_init, b_w_o[j])
            new_b_k.append(k)
            new_b_v.append(v)
        x = post_norm(x, gate * y, ln_g[i, 1], ln_b[i, 1])
        x = ffn_sublayer(x, m, 2, ffn_w_in[i, 1], ffn_w_out[i, 1], ln_g[i, 2], ln_b[i, 2])
    y_prompt = x

    rows = x_sample.shape[1] // GRID_W
    cos_a, sin_a = axial_rope(rows, A_HEAD_DIM)
    cos_b, sin_b = axial_rope(rows, B_QK_DIM)
    x = x_sample
    for i in range(DEPTH):
        j = i // N_MIXERS
        m = modulation(c, ada_w[i], ada_b[i])
        x = ffn_sublayer(x, m, 0, ffn_w_in[i, 0], ffn_w_out[i, 0], ln_g[i, 0], ln_b[i, 0])
        h, gate = modulate(x, m, 1)
        if i % N_MIXERS == 0:
            q, k, v = mixer_a_qkv(h, a_w_qkv[j], a_q_norm[j], a_k_norm[j])
            q = apply_rope(q, cos_a, sin_a)
            k = apply_rope(k, cos_a, sin_a)
            k_all = jnp.concatenate([cache_a_k[:, j], k], axis=1)
            v_all = jnp.concatenate([cache_a_v[:, j], v], axis=1)
            y = gqa_attention(q, k_all, v_all) @ a_w_o[j]
        else:
            lam_init = diff_lambda_init(i)
            q, k, v = mixer_b_qkv(h, b_w_qkv[j])
            q = apply_rope(q, cos_b, sin_b)
            k = apply_rope(k, cos_b, sin_b)
            k_all = jnp.concatenate([cache_b_k[:, j], k], axis=1)
            v_all = jnp.concatenate([cache_b_v[:, j], v], axis=1)
            o = diff_attention(q, k_all, v_all, diff_lambda(b_lambda[j], lam_init))
            y = diff_output(o, b_subln[j], lam_init, b_w_o[j])
        x = post_norm(x, gate * y, ln_g[i, 1], ln_b[i, 1])
        x = ffn_sublayer(x, m, 2, ffn_w_in[i, 1], ffn_w_out[i, 1], ln_g[i, 2], ln_b[i, 2])
    y_sample = x

    new_a_k = jnp.stack(new_a_k, axis=1)
    new_a_v = jnp.stack(new_a_v, axis=1)
    new_b_k = jnp.stack(new_b_k, axis=1)
    new_b_v = jnp.stack(new_b_v, axis=1)
    return (y_prompt, y_sample, new_a_k, new_a_v, new_b_k, new_b_v)
```

```python
import functools
import math

import jax
import jax.numpy as jnp
import numpy as np
from jax import lax
from jax.experimental import pallas as pl
from jax.experimental.pallas import tpu as pltpu

D_MODEL = 2048
BATCH = 16
SEQ = 256
DEPTH = 2
DEC_BATCH = 2
DEC_SEQ = 2048
PAST_LEN = 256
GRID_W = 64
N_MIXERS = 2
N_SUB = 3
A_HEAD_DIM = 128
A_HEADS = D_MODEL // A_HEAD_DIM
A_KV_HEADS = 4
A_REP = A_HEADS // A_KV_HEADS
B_QK_DIM = 64
B_V_DIM = 2 * B_QK_DIM
B_HEADS = D_MODEL // B_V_DIM
D_FF = ((8 * D_MODEL // 3 + 127) // 128) * 128
ROPE_THETA = 10000.0
EPS = 1e-6
MACARON_WEIGHT = 0.5
DEEPNORM_ALPHA = (2 * DEPTH) ** 0.25

N_CTX = BATCH * SEQ
N_LAT = DEC_BATCH * DEC_SEQ
N_ROWS = N_CTX + N_LAT
N_GROUPS = 1 + DEC_BATCH
LANES = 128
D_FF_PAD = ((D_FF + 511) // 512) * 512
V7X_VMEM_BYTES = 64 * 1024 * 1024

F32 = jnp.float32
BF16 = jnp.bfloat16


def _vmem_limit(nbytes):
    return int(min(nbytes * 3 // 2 + (2 << 20), V7X_VMEM_BYTES - (4 << 20)))


def _group_of_row(r):
    return jnp.where(r < N_CTX, 0, 1 + (r - N_CTX) // DEC_SEQ)


def _mod_index(layer, sub, kind):
    return ((layer * N_SUB + sub) * 3 + kind) * N_GROUPS


def _mod_kernel(c_ref, w_ref, b_ref, o_ref):
    s = jax.nn.silu(c_ref[...]).astype(BF16)
    o_ref[...] = jnp.dot(s, w_ref[...].astype(BF16), preferred_element_type=F32) + b_ref[...]


def _modulation(cond, ada_w, ada_b):
    tn = 1024
    n_out = N_SUB * 3 * D_MODEL
    rows = cond.shape[0]
    return pl.pallas_call(
        _mod_kernel,
        grid=(DEPTH, n_out // tn),
        in_specs=[
            pl.BlockSpec((rows, D_MODEL), lambda l, j: (0, 0)),
            pl.BlockSpec((None, D_MODEL, tn), lambda l, j: (l, 0, j)),
            pl.BlockSpec((None, 1, tn), lambda l, j: (l, 0, j)),
        ],
        out_specs=pl.BlockSpec((None, rows, tn), lambda l, j: (l, 0, j)),
        out_shape=jax.ShapeDtypeStruct((DEPTH, rows, n_out), F32),
        compiler_params=pltpu.CompilerParams(
            dimension_semantics=("arbitrary", "arbitrary"),
            vmem_limit_bytes=_vmem_limit(2 * D_MODEL * tn * 4 + D_MODEL * tn * 2)),
        name="modulation",
    )(cond, ada_w, ada_b.reshape(DEPTH, 1, n_out))


def _premod_kernel(x_ref, sc_ref, sh_ref, h_ref):
    h_ref[...] = (x_ref[...] * (1.0 + sc_ref[...]) + sh_ref[...]).astype(BF16)


def _vec_spec(base, tm, grid_rank, row_axis):
    def index_map(*idx):
        return (base + _group_of_row(idx[row_axis] * tm), 0, 0)
    del grid_rank
    return pl.BlockSpec((None, 1, D_MODEL), index_map)


def _premod(x, modv, layer, sub):
    tm = 512
    return pl.pallas_call(
        _premod_kernel,
        grid=(N_ROWS // tm,),
        in_specs=[
            pl.BlockSpec((tm, D_MODEL), lambda i: (i, 0)),
            _vec_spec(_mod_index(layer, sub, 1), tm, 1, 0),
            _vec_spec(_mod_index(layer, sub, 0), tm, 1, 0),
        ],
        out_specs=pl.BlockSpec((tm, D_MODEL), lambda i: (i, 0)),
        out_shape=jax.ShapeDtypeStruct((N_ROWS, D_MODEL), BF16),
        compiler_params=pltpu.CompilerParams(
            dimension_semantics=("arbitrary",),
            vmem_limit_bytes=_vmem_limit(2 * tm * D_MODEL * 6)),
        name="premod",
    )(x, modv, modv)


def _up_kernel(h_ref, wg_ref, wu_ref, a_ref):
    h = h_ref[...]
    g = jnp.dot(h, wg_ref[...], preferred_element_type=F32)
    u = jnp.dot(h, wu_ref[...], preferred_element_type=F32)
    a_ref[...] = (jax.nn.silu(g) * u).astype(BF16)


def _ffn_up(h, w_in, layer, half):
    tm, tf = 1024, 512
    nf = D_FF_PAD // tf
    return pl.pallas_call(
        _up_kernel,
        grid=(nf, N_ROWS // tm),
        in_specs=[
            pl.BlockSpec((tm, D_MODEL), lambda j, i: (i, 0)),
            pl.BlockSpec((None, None, D_MODEL, tf), lambda j, i: (layer, half, 0, j)),
            pl.BlockSpec((None, None, D_MODEL, tf), lambda j, i: (layer, half, 0, nf + j)),
        ],
        out_specs=pl.BlockSpec((tm, tf), lambda j, i: (i, j)),
        out_shape=jax.ShapeDtypeStruct((N_ROWS, D_FF_PAD), BF16),
        compiler_params=pltpu.CompilerParams(
            dimension_semantics=("arbitrary", "arbitrary"),
            vmem_limit_bytes=_vmem_limit(2 * tm * D_MODEL * 2 + 4 * D_MODEL * tf * 2
                                         + 2 * tm * tf * 2 + 3 * tm * tf * 4)),
        name="ffn_up",
    )(h, w_in, w_in)


def _deepnorm_kernel(*refs, coef, emit_h):
    if emit_h:
        lhs_ref, w_ref, x_ref, gate_ref, lng_ref, lnb_ref, sc_ref, sh_ref, xo_ref, ho_ref = refs
    else:
        lhs_ref, w_ref, x_ref, gate_ref, lng_ref, lnb_ref, xo_ref = refs
    y = jnp.dot(lhs_ref[...], w_ref[...], preferred_element_type=F32)
    z = DEEPNORM_ALPHA * x_ref[...] + (coef * gate_ref[...]) * y
    mu = jnp.mean(z, axis=-1, keepdims=True)
    zc = z - mu
    var = jnp.mean(zc * zc, axis=-1, keepdims=True)
    o = zc * lax.rsqrt(var + EPS) * lng_ref[...] + lnb_ref[...]
    xo_ref[...] = o
    if emit_h:
        ho_ref[...] = (o * (1.0 + sc_ref[...]) + sh_ref[...]).astype(BF16)


def _proj_deepnorm(lhs, w, w_index, x, modv, ln_g, ln_b, layer, sub, coef, next_mod):
    tm = 256
    k = lhs.shape[1]
    emit_h = next_mod is not None
    lead = (None,) * len(w_index)
    in_specs = [
        pl.BlockSpec((tm, k), lambda i: (i, 0)),
        pl.BlockSpec(lead + (k, D_MODEL), lambda i: w_index + (0, 0), pipeline_mode=pl.Buffered(1)),
        pl.BlockSpec((tm, D_MODEL), lambda i: (i, 0)),
        _vec_spec(_mod_index(layer, sub, 2), tm, 1, 0),
        pl.BlockSpec((None, 1, D_MODEL), lambda i: (layer * N_SUB + sub, 0, 0)),
        pl.BlockSpec((None, 1, D_MODEL), lambda i: (layer * N_SUB + sub, 0, 0)),
    ]
    args = [lhs, w, x, modv, ln_g, ln_b]
    out_specs = [pl.BlockSpec((tm, D_MODEL), lambda i: (i, 0))]
    out_shape = [jax.ShapeDtypeStruct((N_ROWS, D_MODEL), F32)]
    if emit_h:
        in_specs += [_vec_spec(_mod_index(next_mod[0], next_mod[1], 1), tm, 1, 0),
                     _vec_spec(_mod_index(next_mod[0], next_mod[1], 0), tm, 1, 0)]
        args += [modv, modv]
        out_specs.append(pl.BlockSpec((tm, D_MODEL), lambda i: (i, 0)))
        out_shape.append(jax.ShapeDtypeStruct((N_ROWS, D_MODEL), BF16))
    est = (2 * tm * k * 2 + k * D_MODEL * 2 + 4 * tm * D_MODEL * 4 + 2 * tm * D_MODEL * 2
           + 4 * tm * D_MODEL * 4)
    outs = pl.pallas_call(
        functools.partial(_deepnorm_kernel, coef=coef, emit_h=emit_h),
        grid=(N_ROWS // tm,),
        in_specs=in_specs,
        out_specs=out_specs,
        out_shape=out_shape,
        compiler_params=pltpu.CompilerParams(
            dimension_semantics=("arbitrary",), vmem_limit_bytes=_vmem_limit(est)),
        name="proj_deepnorm",
    )(*args)
    return (outs[0], outs[1]) if emit_h else (outs[0], None)


def _rope_tables(dim):
    rows = DEC_SEQ // GRID_W
    row = np.repeat(np.arange(rows, dtype=np.float64), GRID_W)
    col = np.tile(np.arange(GRID_W, dtype=np.float64), rows)
    quarter = dim // 4
    inv = ROPE_THETA ** (-np.arange(quarter, dtype=np.float64) / quarter)
    ang = np.concatenate([row[:, None] * inv, col[:, None] * inv], axis=-1)
    cos, sin, zero = np.cos(ang), np.sin(ang), np.zeros_like(ang)
    reps = LANES // dim
    full = lambda a, b: np.tile(np.concatenate([a, b], axis=-1), (1, reps)).astype(np.float32)
    return jnp.asarray(full(cos, cos)), jnp.asarray(full(-sin, zero)), jnp.asarray(full(zero, sin))


def _qkv_kernel(*refs, tm, tn, norm, rope_dim):
    if rope_dim:
        h_ref, w_ref, gain_ref, cos_ref, slo_ref, shi_ref, o_ref = refs
    else:
        h_ref, w_ref, gain_ref, o_ref = refs
    y = jnp.dot(h_ref[...], w_ref[...], preferred_element_type=F32)

    def finish(rope):
        for c in range(tn // LANES):
            yc = y[:, c * LANES:(c + 1) * LANES]
            if norm:
                yc = yc * lax.rsqrt(jnp.mean(yc * yc, axis=-1, keepdims=True) + EPS) * gain_ref[...]
            if rope:
                half = rope_dim // 2
                yc = (yc * cos_ref[...]
                      + pltpu.roll(yc, LANES - half, 1) * slo_ref[...]
                      + pltpu.roll(yc, half, 1) * shi_ref[...])
            o_ref[:, c * LANES:(c + 1) * LANES] = yc.astype(o_ref.dtype)

    if rope_dim:
        is_latent = pl.program_id(1) * tm >= N_CTX
        pl.when(is_latent)(lambda: finish(True))
        pl.when(jnp.logical_not(is_latent))(lambda: finish(False))
    else:
        finish(False)


def _qkv_proj(h, w, layer_j, col0, width, gain, tables, out_dtype, norm, rope_dim):
    tm, tn = 1024, 512
    n0 = col0 // tn
    in_specs = [
        pl.BlockSpec((tm, D_MODEL), lambda j, i: (i, 0)),
        pl.BlockSpec((None, D_MODEL, tn), lambda j, i: (layer_j, 0, n0 + j)),
        pl.BlockSpec((1, LANES), lambda j, i: (0, 0)),
    ]
    args = [h, w, gain]
    if rope_dim:
        def tab_map(j, i):
            r = i * tm
            return (jnp.where(r >= N_CTX, ((r - N_CTX) % DEC_SEQ) // tm, 0), 0)
        in_specs += [pl.BlockSpec((tm, LANES), tab_map)] * 3
        args += list(tables)
    return pl.pallas_call(
        functools.partial(_qkv_kernel, tm=tm, tn=tn, norm=norm, rope_dim=rope_dim),
        grid=(width // tn, N_ROWS // tm),
        in_specs=in_specs,
        out_specs=pl.BlockSpec((tm, tn), lambda j, i: (i, j)),
        out_shape=jax.ShapeDtypeStruct((N_ROWS, width), out_dtype),
        compiler_params=pltpu.CompilerParams(
            dimension_semantics=("arbitrary", "arbitrary"),
            vmem_limit_bytes=_vmem_limit(2 * tm * D_MODEL * 2 + 2 * D_MODEL * tn * 2 + 2 * tm * tn * 4
                                         + 6 * tm * LANES * 4 + 3 * tm * tn * 4)),
        name="qkv_proj",
    )(*args)


_NT = (((1,), (1,)), ((), ()))


def _attn_a_kernel(*refs, tq, has_cache):
    if has_cache:
        q_ref, kc_ref, vc_ref, kn_ref, vn_ref, o_ref = refs
    else:
        q_ref, kn_ref, vn_ref, o_ref = refs
    scale = A_HEAD_DIM ** -0.5
    q = jnp.concatenate([q_ref[:, g * LANES:(g + 1) * LANES] for g in range(A_REP)], axis=0)
    kn = kn_ref[...].astype(BF16)
    vn = vn_ref[...].astype(BF16)
    s_new = lax.dot_general(q, kn, _NT, preferred_element_type=F32) * scale
    m = jnp.max(s_new, axis=-1, keepdims=True)
    if has_cache:
        kc = kc_ref[...].astype(BF16)
        vc = vc_ref[...].astype(BF16)
        s_old = lax.dot_general(q, kc, _NT, preferred_element_type=F32) * scale
        m = jnp.maximum(m, jnp.max(s_old, axis=-1, keepdims=True))
    e_new = jnp.exp(s_new - m)
    denom = jnp.sum(e_new, axis=-1, keepdims=True)
    acc = jnp.dot(e_new.astype(BF16), vn, preferred_element_type=F32)
    if has_cache:
        e_old = jnp.exp(s_old - m)
        denom = denom + jnp.sum(e_old, axis=-1, keepdims=True)
        acc = acc + jnp.dot(e_old.astype(BF16), vc, preferred_element_type=F32)
    o = acc / denom
    for g in range(A_REP):
        o_ref[:, g * LANES:(g + 1) * LANES] = o[g * tq:(g + 1) * tq].astype(o_ref.dtype)


def _attn_a(q, k, v, cache_k, cache_v):
    gw = A_REP * LANES
    tq = SEQ
    o_ctx = pl.pallas_call(
        functools.partial(_attn_a_kernel, tq=tq, has_cache=False),
        grid=(BATCH, A_KV_HEADS),
        in_specs=[
            pl.BlockSpec((tq, gw), lambda b, h: (b, h)),
            pl.BlockSpec((SEQ, LANES), lambda b, h: (b, h)),
            pl.BlockSpec((SEQ, LANES), lambda b, h: (b, h)),
        ],
        out_specs=pl.BlockSpec((tq, gw), lambda b, h: (b, h)),
        out_shape=jax.ShapeDtypeStruct((N_CTX, D_MODEL), BF16),
        compiler_params=pltpu.CompilerParams(
            dimension_semantics=("arbitrary", "arbitrary"),
            vmem_limit_bytes=_vmem_limit(4 * tq * gw * 2 + 4 * SEQ * LANES * 4 + 4 * A_REP * tq * SEQ * 4)),
        name="attn_a_ctx",
    )(q, k, v)
    tq = 256
    nq = DEC_SEQ // tq
    row0 = N_CTX // tq
    t_all = PAST_LEN + DEC_SEQ
    o_lat = pl.pallas_call(
        functools.partial(_attn_a_kernel, tq=tq, has_cache=True),
        grid=(DEC_BATCH, A_KV_HEADS, nq),
        in_specs=[
            pl.BlockSpec((tq, gw), lambda b, h, i: (row0 + b * nq + i, h)),
            pl.BlockSpec((None, PAST_LEN, LANES), lambda b, h, i: (b, 0, h)),
            pl.BlockSpec((None, PAST_LEN, LANES), lambda b, h, i: (b, 0, h)),
            pl.BlockSpec((DEC_SEQ, LANES), lambda b, h, i: (N_CTX // DEC_SEQ + b, h)),
            pl.BlockSpec((DEC_SEQ, LANES), lambda b, h, i: (N_CTX // DEC_SEQ + b, h)),
        ],
        out_specs=pl.BlockSpec((tq, gw), lambda b, h, i: (b * nq + i, h)),
        out_shape=jax.ShapeDtypeStruct((N_LAT, D_MODEL), BF16),
        compiler_params=pltpu.CompilerParams(
            dimension_semantics=("arbitrary", "arbitrary", "arbitrary"),
            vmem_limit_bytes=_vmem_limit(4 * tq * gw * 2 + 4 * t_all * LANES * 4 + 2 * t_all * LANES * 2
                                         + 3 * A_REP * tq * t_all * 4)),
        name="attn_a_lat",
    )(q, cache_k, cache_v, k, v)
    return jnp.concatenate([o_ctx, o_lat], axis=0)


def _attn_b_kernel(*refs, heads, has_cache, lam_init):
    if has_cache:
        q_ref, kc_ref, vc_ref, kn_ref, vn_ref, lam_ref, sub_ref, o_ref = refs
    else:
        q_ref, kn_ref, vn_ref, lam_ref, sub_ref, o_ref = refs
    scale = B_QK_DIM ** -0.5
    lp = lam_ref[...]
    lam = (jnp.exp(jnp.sum(lp[0:1] * lp[1:2], axis=-1, keepdims=True))
           - jnp.exp(jnp.sum(lp[2:3] * lp[3:4], axis=-1, keepdims=True)) + lam_init)
    first = lax.broadcasted_iota(jnp.int32, (1, LANES), 1) < B_QK_DIM
    for h in range(heads):
        sl = slice(h * LANES, (h + 1) * LANES)
        q = q_ref[:, sl]
        zero = jnp.zeros_like(q)
        qs = (jnp.where(first, q, zero), jnp.where(first, zero, q))
        kn = kn_ref[:, sl].astype(BF16)
        vn = vn_ref[:, sl].astype(BF16)
        if has_cache:
            kc = kc_ref[:, sl].astype(BF16)
            vc = vc_ref[:, sl].astype(BF16)
        w_new, w_old = None, None
        for c in range(2):
            s_new = lax.dot_general(qs[c], kn, _NT, preferred_element_type=F32) * scale
            m = jnp.max(s_new, axis=-1, keepdims=True)
            if has_cache:
                s_old = lax.dot_general(qs[c], kc, _NT, preferred_element_type=F32) * scale
                m = jnp.maximum(m, jnp.max(s_old, axis=-1, keepdims=True))
            e_new = jnp.exp(s_new - m)
            denom = jnp.sum(e_new, axis=-1, keepdims=True)
            if has_cache:
                e_old = jnp.exp(s_old - m)
                denom = denom + jnp.sum(e_old, axis=-1, keepdims=True)
            coef = 1.0 / denom if c == 0 else -lam / denom
            w_new = e_new * coef if c == 0 else w_new + e_new * coef
            if has_cache:
                w_old = e_old * coef if c == 0 else w_old + e_old * coef
        o = jnp.dot(w_new.astype(BF16), vn, preferred_element_type=F32)
        if has_cache:
            o = o + jnp.dot(w_old.astype(BF16), vc, preferred_element_type=F32)
        o = o * lax.rsqrt(jnp.mean(o * o, axis=-1, keepdims=True) + EPS) * sub_ref[...] * (1.0 - lam_init)
        o_ref[:, sl] = o.astype(o_ref.dtype)


def _attn_b(q, k, v, cache_k, cache_v, lam_p, subln, lam_init):
    heads = 4
    gw = heads * LANES
    tq = SEQ
    o_ctx = pl.pallas_call(
        functools.partial(_attn_b_kernel, heads=heads, has_cache=False, lam_init=lam_init),
        grid=(BATCH, B_HEADS // heads),
        in_specs=[
            pl.BlockSpec((tq, gw), lambda b, h: (b, h)),
            pl.BlockSpec((SEQ, gw), lambda b, h: (b, h)),
            pl.BlockSpec((SEQ, gw), lambda b, h: (b, h)),
            pl.BlockSpec((4, B_QK_DIM), lambda b, h: (0, 0)),
            pl.BlockSpec((1, LANES), lambda b, h: (0, 0)),
        ],
        out_specs=pl.BlockSpec((tq, gw), lambda b, h: (b, h)),
        out_shape=jax.ShapeDtypeStruct((N_CTX, D_MODEL), BF16),
        compiler_params=pltpu.CompilerParams(
            dimension_semantics=("arbitrary", "arbitrary"),
            vmem_limit_bytes=_vmem_limit(4 * tq * gw * 2 + 4 * SEQ * gw * 4 + 8 * heads * tq * SEQ * 4)),
        name="attn_b_ctx",
    )(q, k, v, lam_p, subln)
    tq = 512
    nq = DEC_SEQ // tq
    row0 = N_CTX // tq
    t_all = PAST_LEN + DEC_SEQ
    o_lat = pl.pallas_call(
        functools.partial(_attn_b_kernel, heads=1, has_cache=True, lam_init=lam_init),
        grid=(DEC_BATCH, B_HEADS, nq),
        in_specs=[
            pl.BlockSpec((tq, LANES), lambda b, h, i: (row0 + b * nq + i, h)),
            pl.BlockSpec((None, PAST_LEN, LANES), lambda b, h, i: (b, 0, h)),
            pl.BlockSpec((None, PAST_LEN, LANES), lambda b, h, i: (b, 0, h)),
            pl.BlockSpec((DEC_SEQ, LANES), lambda b, h, i: (N_CTX // DEC_SEQ + b, h)),
            pl.BlockSpec((DEC_SEQ, LANES), lambda b, h, i: (N_CTX // DEC_SEQ + b, h)),
            pl.BlockSpec((4, B_QK_DIM), lambda b, h, i: (0, 0)),
            pl.BlockSpec((1, LANES), lambda b, h, i: (0, 0)),
        ],
        out_specs=pl.BlockSpec((tq, LANES), lambda b, h, i: (b * nq + i, h)),
        out_shape=jax.ShapeDtypeStruct((N_LAT, D_MODEL), BF16),
        compiler_params=pltpu.CompilerParams(
            dimension_semantics=("arbitrary", "arbitrary", "arbitrary"),
            vmem_limit_bytes=_vmem_limit(4 * tq * LANES * 2 + 4 * t_all * LANES * 4 + 2 * t_all * LANES * 2
                                         + 6 * tq * t_all * 4)),
        name="attn_b_lat",
    )(q, cache_k, cache_v, k, v, lam_p, subln)
    return jnp.concatenate([o_ctx, o_lat], axis=0)


def _diff_lambda_init(layer_idx):
    return 0.8 - 0.6 * math.exp(-0.3 * layer_idx)


def kernel(x_prompt, x_sample, cache_a_k, cache_a_v, cache_b_k, cache_b_v, c, c_ctx, ada_w, ada_b, ln_g, ln_b, ffn_w_in, ffn_w_out, a_w_qkv, a_q_norm, a_k_norm, a_w_o, b_w_qkv, b_lambda, b_subln, b_w_o):
    pad = D_FF_PAD - D_FF
    w_in = jnp.concatenate(
        [jnp.pad(ffn_w_in[..., :D_FF], ((0, 0), (0, 0), (0, 0), (0, pad))),
         jnp.pad(ffn_w_in[..., D_FF:], ((0, 0), (0, 0), (0, 0), (0, pad)))], axis=-1).astype(BF16)
    w_out = jnp.pad(ffn_w_out, ((0, 0), (0, 0), (0, pad), (0, 0))).astype(BF16)
    a_qkv, a_o = a_w_qkv.astype(BF16), a_w_o.astype(BF16)
    b_qkv, b_o = b_w_qkv.astype(BF16), b_w_o.astype(BF16)
    ln_g3 = ln_g.reshape(DEPTH * N_SUB, 1, D_MODEL)
    ln_b3 = ln_b.reshape(DEPTH * N_SUB, 1, D_MODEL)
    tables_a = _rope_tables(A_HEAD_DIM)
    tables_b = _rope_tables(B_QK_DIM)
    ones = jnp.ones((1, LANES), F32)

    cond = jnp.concatenate([c_ctx[None, :], c, jnp.zeros((8 - N_GROUPS, D_MODEL), F32)], axis=0)
    mod = _modulation(cond, ada_w, ada_b)[:, :N_GROUPS]
    modv = mod.reshape(DEPTH, N_GROUPS, N_SUB * 3, D_MODEL).transpose(0, 2, 1, 3)
    modv = modv.reshape(DEPTH * N_SUB * 3 * N_GROUPS, 1, D_MODEL)

    x = jnp.concatenate([x_prompt.reshape(N_CTX, D_MODEL), x_sample.reshape(N_LAT, D_MODEL)], axis=0)
    h = _premod(x, modv, 0, 0)

    new_kv = {}
    for i in range(DEPTH):
        j = i // N_MIXERS
        a = _ffn_up(h, w_in, i, 0)
        x, h = _proj_deepnorm(a, w_out, (i, 0), x, modv, ln_g3, ln_b3, i, 0, MACARON_WEIGHT, (i, 1))
        if i % N_MIXERS == 0:
            qw = A_HEADS * A_HEAD_DIM
            kw = A_KV_HEADS * A_HEAD_DIM
            q = _qkv_proj(h, a_qkv, j, 0, qw, a_q_norm[j][None, :], tables_a, BF16, True, A_HEAD_DIM)
            k = _qkv_proj(h, a_qkv, j, qw, kw, a_k_norm[j][None, :], tables_a, F32, True, A_HEAD_DIM)
            v = _qkv_proj(h, a_qkv, j, qw + kw, kw, ones, None, F32, False, 0)
            o = _attn_a(q, k, v, cache_a_k[:, j].reshape(DEC_BATCH, PAST_LEN, kw),
                        cache_a_v[:, j].reshape(DEC_BATCH, PAST_LEN, kw))
            new_kv.setdefault("ak", []).append(k[:N_CTX].reshape(BATCH, SEQ, A_KV_HEADS, A_HEAD_DIM))
            new_kv.setdefault("av", []).append(v[:N_CTX].reshape(BATCH, SEQ, A_KV_HEADS, A_HEAD_DIM))
            w_o, w_o_idx = a_o, (j,)
        else:
            lam_init = _diff_lambda_init(i)
            qw = B_HEADS * 2 * B_QK_DIM
            q = _qkv_proj(h, b_qkv, j, 0, qw, ones, tables_b, BF16, False, B_QK_DIM)
            k = _qkv_proj(h, b_qkv, j, qw, qw, ones, tables_b, F32, False, B_QK_DIM)
            v = _qkv_proj(h, b_qkv, j, 2 * qw, B_HEADS * B_V_DIM, ones, None, F32, False, 0)
            o = _attn_b(q, k, v, cache_b_k[:, j].reshape(DEC_BATCH, PAST_LEN, qw),
                        cache_b_v[:, j].reshape(DEC_BATCH, PAST_LEN, B_HEADS * B_V_DIM),
                        b_lambda[j], b_subln[j][None, :], lam_init)
            new_kv.setdefault("bk", []).append(k[:N_CTX].reshape(BATCH, SEQ, B_HEADS, 2, B_QK_DIM))
            new_kv.setdefault("bv", []).append(v[:N_CTX].reshape(BATCH, SEQ, B_HEADS, B_V_DIM))
            w_o, w_o_idx = b_o, (j,)
        x, h = _proj_deepnorm(o, w_o, w_o_idx, x, modv, ln_g3, ln_b3, i, 1, 1.0, (i, 2))
        a = _ffn_up(h, w_in, i, 1)
        nxt = (i + 1, 0) if i + 1 < DEPTH else None
        x, h = _proj_deepnorm(a, w_out, (i, 1), x, modv, ln_g3, ln_b3, i, 2, MACARON_WEIGHT, nxt)

    y_prompt = x[:N_CTX].reshape(BATCH, SEQ, D_MODEL)
    y_sample = x[N_CTX:].reshape(DEC_BATCH, DEC_SEQ, D_MODEL)
    return (y_prompt, y_sample,
            jnp.stack(new_kv["ak"], axis=1), jnp.stack(new_kv["av"], axis=1),
            jnp.stack(new_kv["bk"], axis=1), jnp.stack(new_kv["bv"], axis=1))
```

```python
import functools
import math

import jax
import jax.numpy as jnp
import numpy as np
from jax import lax
from jax.experimental import pallas as pl
from jax.experimental.pallas import tpu as pltpu

D_MODEL = 2048
BATCH = 16
SEQ = 256
DEPTH = 2
DEC_BATCH = 2
DEC_SEQ = 2048
PAST_LEN = 256
GRID_W = 64
N_MIXERS = 2
N_SUB = 3
A_HEAD_DIM = 128
A_HEADS = D_MODEL // A_HEAD_DIM
A_KV_HEADS = 4
A_REP = A_HEADS // A_KV_HEADS
B_QK_DIM = 64
B_V_DIM = 2 * B_QK_DIM
B_HEADS = D_MODEL // B_V_DIM
D_FF = ((8 * D_MODEL // 3 + 127) // 128) * 128
ROPE_THETA = 10000.0
EPS = 1e-6
MACARON_WEIGHT = 0.5
DEEPNORM_ALPHA = (2 * DEPTH) ** 0.25

N_CTX = BATCH * SEQ
N_LAT = DEC_BATCH * DEC_SEQ
N_ROWS = N_CTX + N_LAT
N_GROUPS = 1 + DEC_BATCH
COND_ROWS = 8
LANES = 128
V7X_VMEM_BYTES = 64 * 1024 * 1024

F32 = jnp.float32
BF16 = jnp.bfloat16


def _vmem_limit(nbytes):
    return int(min(nbytes * 3 // 2 + (2 << 20), V7X_VMEM_BYTES - (4 << 20)))


def _group_of_row(r):
    return jnp.where(r < N_CTX, 0, 1 + (r - N_CTX) // DEC_SEQ)


def _mod_spec(layer, sub, kind, grid_rank):
    col = sub * 3 + kind
    return pl.BlockSpec((None, COND_ROWS, D_MODEL), lambda *idx: (layer, 0, col))


def _mod_row(ref, row0):
    return ref[pl.ds(_group_of_row(row0), 1), :]


def _mod_kernel(c_ref, w_ref, b_ref, o_ref):
    s = jax.nn.silu(c_ref[...]).astype(BF16)
    o_ref[...] = jnp.dot(s, w_ref[...].astype(BF16), preferred_element_type=F32) + b_ref[...]


def _modulation(cond, ada_w, ada_b):
    tn = 1024
    n_out = N_SUB * 3 * D_MODEL
    return pl.pallas_call(
        _mod_kernel,
        grid=(DEPTH, n_out // tn),
        in_specs=[
            pl.BlockSpec((COND_ROWS, D_MODEL), lambda l, j: (0, 0)),
            pl.BlockSpec((None, D_MODEL, tn), lambda l, j: (l, 0, j)),
            pl.BlockSpec((None, 1, tn), lambda l, j: (l, 0, j)),
        ],
        out_specs=pl.BlockSpec((None, COND_ROWS, tn), lambda l, j: (l, 0, j)),
        out_shape=jax.ShapeDtypeStruct((DEPTH, COND_ROWS, n_out), F32),
        compiler_params=pltpu.CompilerParams(
            dimension_semantics=("arbitrary", "arbitrary"),
            vmem_limit_bytes=_vmem_limit(2 * D_MODEL * tn * 4 + D_MODEL * tn * 2)),
        name="modulation",
    )(cond, ada_w, ada_b.reshape(DEPTH, 1, n_out))


def _split_specs(tm, width, row_axis):
    n_ctx = N_CTX // tm

    def ctx_map(*idx):
        return (jnp.minimum(idx[row_axis], n_ctx - 1), 0)

    def lat_map(*idx):
        return (jnp.maximum(idx[row_axis] - n_ctx, 0), 0)

    return pl.BlockSpec((tm, width), ctx_map), pl.BlockSpec((tm, width), lat_map)


def _premod_kernel(xc_ref, xl_ref, sc_ref, sh_ref, h_ref, *, tm):
    row0 = pl.program_id(0) * tm
    sc = 1.0 + _mod_row(sc_ref, row0)
    sh = _mod_row(sh_ref, row0)

    def emit(x_ref):
        h_ref[...] = (x_ref[...] * sc + sh).astype(BF16)

    pl.when(row0 < N_CTX)(lambda: emit(xc_ref))
    pl.when(row0 >= N_CTX)(lambda: emit(xl_ref))


def _premod(x_ctx, x_lat, mod, layer, sub):
    tm = 512
    return pl.pallas_call(
        functools.partial(_premod_kernel, tm=tm),
        grid=(N_ROWS // tm,),
        in_specs=[*_split_specs(tm, D_MODEL, 0),
                  _mod_spec(layer, sub, 1, 1), _mod_spec(layer, sub, 0, 1)],
        out_specs=pl.BlockSpec((tm, D_MODEL), lambda i: (i, 0)),
        out_shape=jax.ShapeDtypeStruct((N_ROWS, D_MODEL), BF16),
        compiler_params=pltpu.CompilerParams(
            dimension_semantics=("arbitrary",),
            vmem_limit_bytes=_vmem_limit(4 * tm * D_MODEL * 4 + 2 * tm * D_MODEL * 2)),
        name="premod",
    )(x_ctx, x_lat, mod, mod)


def _up_kernel(h_ref, wg_ref, wu_ref, wo_ref, a_ref, wob_ref, wbf_ref, *, tf, nf, tail):
    is_tail = pl.program_id(0) == nf - 1

    @pl.when(pl.program_id(1) == 0)
    def _():
        wbf_ref[:, :tf] = wg_ref[...].astype(BF16)
        wbf_ref[:, tf:] = wu_ref[...].astype(BF16)

        @pl.when(jnp.logical_not(is_tail))
        def _():
            wob_ref[...] = wo_ref[...].astype(BF16)

        @pl.when(is_tail)
        def _():
            wob_ref[:tail, :] = wo_ref[tf - tail:, :].astype(BF16)

    gu = jnp.dot(h_ref[...], wbf_ref[...], preferred_element_type=F32)
    a = (jax.nn.silu(gu[:, :tf]) * gu[:, tf:]).astype(BF16)

    @pl.when(jnp.logical_not(is_tail))
    def _():
        a_ref[...] = a

    @pl.when(is_tail)
    def _():
        a_ref[:, :tail] = a[:, tf - tail:]


def _ffn_up(h, w_in, w_out, layer, half):
    tm, tf = 1024, 512
    nf = pl.cdiv(D_FF, tf)
    tail = D_FF - (nf - 1) * tf
    start = lambda j: pl.multiple_of(jnp.minimum(j * tf, D_FF - tf), LANES)
    ustart = lambda j: pl.multiple_of(D_FF + jnp.minimum(j * tf, D_FF - tf), LANES)
    elem = pl.Element
    return pl.pallas_call(
        functools.partial(_up_kernel, tf=tf, nf=nf, tail=tail),
        grid=(nf, N_ROWS // tm),
        in_specs=[
            pl.BlockSpec((tm, D_MODEL), lambda j, i: (i, 0)),
            pl.BlockSpec((None, None, elem(D_MODEL), elem(tf)), lambda j, i: (layer, half, 0, start(j))),
            pl.BlockSpec((None, None, elem(D_MODEL), elem(tf)), lambda j, i: (layer, half, 0, ustart(j))),
            pl.BlockSpec((None, None, elem(tf), elem(D_MODEL)), lambda j, i: (layer, half, start(j), 0)),
        ],
        out_specs=[pl.BlockSpec((tm, tf), lambda j, i: (i, j)),
                   pl.BlockSpec((tf, D_MODEL), lambda j, i: (j, 0))],
        out_shape=[jax.ShapeDtypeStruct((N_ROWS, D_FF), BF16),
                   jax.ShapeDtypeStruct((D_FF, D_MODEL), BF16)],
        scratch_shapes=[pltpu.VMEM((D_MODEL, 2 * tf), BF16)],
        compiler_params=pltpu.CompilerParams(
            dimension_semantics=("arbitrary", "arbitrary"),
            vmem_limit_bytes=_vmem_limit(2 * tm * D_MODEL * 2 + 6 * D_MODEL * tf * 4 + 2 * D_MODEL * tf * 2
                                         + 2 * tm * tf * 2 + 2 * tf * D_MODEL * 2 + 3 * tm * tf * 4)),
        name="ffn_up",
    )(h, w_in, w_in, w_out)


def _deepnorm_kernel(*refs, tm, coef, cast_w, split_lhs, split_x, split_out, emit_h):
    refs = list(refs)
    take = lambda n: [refs.pop(0) for _ in range(n)]
    lhs_refs = take(2 if split_lhs else 1)
    (w_ref,) = take(1)
    x_refs = take(2 if split_x else 1)
    gate_ref, lng_ref, lnb_ref = take(3)
    sc_ref, sh_ref = take(2) if emit_h else (None, None)
    xo_refs = take(2 if split_out else 1)
    (ho_ref,) = take(1) if emit_h else (None,)
    (wbf_ref,) = take(1) if cast_w else (w_ref,)

    if cast_w:
        @pl.when(pl.program_id(0) == 0)
        def _():
            wbf_ref[...] = w_ref[...].astype(BF16)

    row0 = pl.program_id(0) * tm
    gate = coef * _mod_row(gate_ref, row0)
    if emit_h:
        sc = 1.0 + _mod_row(sc_ref, row0)
        sh = _mod_row(sh_ref, row0)

    def body(lhs_ref, x_ref, xo_ref):
        y = jnp.dot(lhs_ref[...], wbf_ref[...], preferred_element_type=F32)
        z = DEEPNORM_ALPHA * x_ref[...] + gate * y
        mu = jnp.mean(z, axis=-1, keepdims=True)
        zc = z - mu
        var = jnp.mean(zc * zc, axis=-1, keepdims=True)
        o = zc * lax.rsqrt(var + EPS) * lng_ref[...] + lnb_ref[...]
        xo_ref[...] = o
        if emit_h:
            ho_ref[...] = (o * sc + sh).astype(BF16)

    if split_lhs or split_x or split_out:
        pl.when(row0 < N_CTX)(lambda: body(lhs_refs[0], x_refs[0], xo_refs[0]))
        pl.when(row0 >= N_CTX)(lambda: body(lhs_refs[-1], x_refs[-1], xo_refs[-1]))
    else:
        body(lhs_refs[0], x_refs[0], xo_refs[0])


def _proj_deepnorm(lhs, w, w_index, x, mod, ln_g, ln_b, layer, sub, coef, next_mod, split_out=False):
    tm = 256
    split_lhs, split_x = isinstance(lhs, tuple), isinstance(x, tuple)
    k = (lhs[0] if split_lhs else lhs).shape[1]
    cast_w = w.dtype != BF16
    emit_h = next_mod is not None
    lead = (None,) * len(w_index)
    row_spec = lambda width: pl.BlockSpec((tm, width), lambda i: (i, 0))
    in_specs, args = [], []
    if split_lhs:
        in_specs += list(_split_specs(tm, k, 0)); args += list(lhs)
    else:
        in_specs.append(row_spec(k)); args.append(lhs)
    in_specs.append(pl.BlockSpec(lead + (k, D_MODEL), lambda i: w_index + (0, 0), pipeline_mode=pl.Buffered(1)))
    args.append(w)
    if split_x:
        in_specs += list(_split_specs(tm, D_MODEL, 0)); args += list(x)
    else:
        in_specs.append(row_spec(D_MODEL)); args.append(x)
    ln_spec = pl.BlockSpec((None, 1, D_MODEL), lambda i: (layer * N_SUB + sub, 0, 0))
    in_specs += [_mod_spec(layer, sub, 2, 1), ln_spec, ln_spec]
    args += [mod, ln_g, ln_b]
    if emit_h:
        in_specs += [_mod_spec(next_mod[0], next_mod[1], 1, 1), _mod_spec(next_mod[0], next_mod[1], 0, 1)]
        args += [mod, mod]
    if split_out:
        out_specs = list(_split_specs(tm, D_MODEL, 0))
        out_shape = [jax.ShapeDtypeStruct((N_CTX, D_MODEL), F32), jax.ShapeDtypeStruct((N_LAT, D_MODEL), F32)]
    else:
        out_specs = [row_spec(D_MODEL)]
        out_shape = [jax.ShapeDtypeStruct((N_ROWS, D_MODEL), F32)]
    if emit_h:
        out_specs.append(row_spec(D_MODEL))
        out_shape.append(jax.ShapeDtypeStruct((N_ROWS, D_MODEL), BF16))
    w_bytes = k * D_MODEL * (6 if cast_w else 2)
    est = (2 * (1 + split_lhs) * tm * k * 2 + w_bytes + 2 * (2 + split_x + split_out) * tm * D_MODEL * 4
           + 2 * tm * D_MODEL * 2 + 4 * tm * D_MODEL * 4)
    outs = pl.pallas_call(
        functools.partial(_deepnorm_kernel, tm=tm, coef=coef, cast_w=cast_w, split_lhs=split_lhs,
                          split_x=split_x, split_out=split_out, emit_h=emit_h),
        grid=(N_ROWS // tm,),
        in_specs=in_specs,
        out_specs=out_specs,
        out_shape=out_shape,
        scratch_shapes=[pltpu.VMEM((k, D_MODEL), BF16)] if cast_w else [],
        compiler_params=pltpu.CompilerParams(
            dimension_semantics=("arbitrary",), vmem_limit_bytes=_vmem_limit(est)),
        name="proj_deepnorm",
    )(*args)
    x_new = (outs[0], outs[1]) if split_out else outs[0]
    return x_new, (outs[-1] if emit_h else None)


def _rope_tables(dim):
    rows = DEC_SEQ // GRID_W
    row = np.repeat(np.arange(rows, dtype=np.float64), GRID_W)
    col = np.tile(np.arange(GRID_W, dtype=np.float64), rows)
    quarter = dim // 4
    inv = ROPE_THETA ** (-np.arange(quarter, dtype=np.float64) / quarter)
    ang = np.concatenate([row[:, None] * inv, col[:, None] * inv], axis=-1)
    cos, sin, zero = np.cos(ang), np.sin(ang), np.zeros_like(ang)
    reps = LANES // dim
    full = lambda a, b: np.tile(np.concatenate([a, b], axis=-1), (1, reps)).astype(np.float32)
    return jnp.asarray(full(cos, cos)), jnp.asarray(full(-sin, zero)), jnp.asarray(full(zero, sin))


def _qkv_kernel(*refs, tm, tn, norm, rope_dim, split_out):
    refs = list(refs)
    take = lambda n: [refs.pop(0) for _ in range(n)]
    h_ref, w_ref, gain_ref = take(3)
    cos_ref, slo_ref, shi_ref = take(3) if rope_dim else (None, None, None)
    o_refs = take(2 if split_out else 1)
    (wbf_ref,) = take(1)

    @pl.when(pl.program_id(1) == 0)
    def _():
        wbf_ref[...] = w_ref[...].astype(BF16)

    y = jnp.dot(h_ref[...], wbf_ref[...], preferred_element_type=F32)

    def finish(o_ref, rope):
        for c in range(tn // LANES):
            yc = y[:, c * LANES:(c + 1) * LANES]
            if norm:
                yc = yc * lax.rsqrt(jnp.mean(yc * yc, axis=-1, keepdims=True) + EPS) * gain_ref[...]
            if rope:
                half = rope_dim // 2
                yc = (yc * cos_ref[...]
                      + pltpu.roll(yc, LANES - half, 1) * slo_ref[...]
                      + pltpu.roll(yc, half, 1) * shi_ref[...])
            o_ref[:, c * LANES:(c + 1) * LANES] = yc.astype(o_ref.dtype)

    if rope_dim or split_out:
        is_latent = pl.program_id(1) * tm >= N_CTX
        pl.when(jnp.logical_not(is_latent))(lambda: finish(o_refs[0], False))
        pl.when(is_latent)(lambda: finish(o_refs[-1], bool(rope_dim)))
    else:
        finish(o_refs[0], False)


def _qkv_proj(h, w, layer_j, col0, width, gain, tables, out_dtypes, norm, rope_dim):
    tm, tn = 1024, 512
    n0 = col0 // tn
    split_out = isinstance(out_dtypes, tuple)
    in_specs = [
        pl.BlockSpec((tm, D_MODEL), lambda j, i: (i, 0)),
        pl.BlockSpec((None, D_MODEL, tn), lambda j, i: (layer_j, 0, n0 + j)),
        pl.BlockSpec((1, LANES), lambda j, i: (0, 0)),
    ]
    args = [h, w, gain]
    if rope_dim:
        def tab_map(j, i):
            r = i * tm
            return (jnp.where(r >= N_CTX, ((r - N_CTX) % DEC_SEQ) // tm, 0), 0)
        in_specs += [pl.BlockSpec((tm, LANES), tab_map)] * 3
        args += list(tables)
    if split_out:
        n_ctx = N_CTX // tm
        out_specs = [pl.BlockSpec((tm, tn), lambda j, i: (jnp.minimum(i, n_ctx - 1), j)),
                     pl.BlockSpec((tm, tn), lambda j, i: (jnp.maximum(i - n_ctx, 0), j))]
        out_shape = [jax.ShapeDtypeStruct((N_CTX, width), out_dtypes[0]),
                     jax.ShapeDtypeStruct((N_LAT, width), out_dtypes[1])]
    else:
        out_specs = [pl.BlockSpec((tm, tn), lambda j, i: (i, j))]
        out_shape = [jax.ShapeDtypeStruct((N_ROWS, width), out_dtypes)]
    outs = pl.pallas_call(
        functools.partial(_qkv_kernel, tm=tm, tn=tn, norm=norm, rope_dim=rope_dim, split_out=split_out),
        grid=(width // tn, N_ROWS // tm),
        in_specs=in_specs,
        out_specs=out_specs,
        out_shape=out_shape,
        scratch_shapes=[pltpu.VMEM((D_MODEL, tn), BF16)],
        compiler_params=pltpu.CompilerParams(
            dimension_semantics=("arbitrary", "arbitrary"),
            vmem_limit_bytes=_vmem_limit(2 * tm * D_MODEL * 2 + 2 * D_MODEL * tn * 4 + D_MODEL * tn * 2
                                         + 4 * tm * tn * 4 + 6 * tm * LANES * 4 + 3 * tm * tn * 4)),
        name="qkv_proj",
    )(*args)
    return (outs[0], outs[1]) if split_out else outs[0]


_NT = (((1,), (1,)), ((), ()))


def _attn_a_kernel(*refs, tq, has_cache):
    if has_cache:
        q_ref, kc_ref, vc_ref, kn_ref, vn_ref, o_ref = refs
    else:
        q_ref, kn_ref, vn_ref, o_ref = refs
    scale = A_HEAD_DIM ** -0.5
    q = jnp.concatenate([q_ref[:, g * LANES:(g + 1) * LANES] for g in range(A_REP)], axis=0)
    kn = kn_ref[...].astype(BF16)
    vn = vn_ref[...].astype(BF16)
    s_new = lax.dot_general(q, kn, _NT, preferred_element_type=F32) * scale
    m = jnp.max(s_new, axis=-1, keepdims=True)
    if has_cache:
        kc = kc_ref[...].astype(BF16)
        vc = vc_ref[...].astype(BF16)
        s_old = lax.dot_general(q, kc, _NT, preferred_element_type=F32) * scale
        m = jnp.maximum(m, jnp.max(s_old, axis=-1, keepdims=True))
    e_new = jnp.exp(s_new - m)
    denom = jnp.sum(e_new, axis=-1, keepdims=True)
    acc = jnp.dot(e_new.astype(BF16), vn, preferred_element_type=F32)
    if has_cache:
        e_old = jnp.exp(s_old - m)
        denom = denom + jnp.sum(e_old, axis=-1, keepdims=True)
        acc = acc + jnp.dot(e_old.astype(BF16), vc, preferred_element_type=F32)
    o = acc / denom
    for g in range(A_REP):
        o_ref[:, g * LANES:(g + 1) * LANES] = o[g * tq:(g + 1) * tq].astype(o_ref.dtype)


def _attn_a(q, k, v, cache_k, cache_v):
    gw = A_REP * LANES
    tq = SEQ
    o_ctx = pl.pallas_call(
        functools.partial(_attn_a_kernel, tq=tq, has_cache=False),
        grid=(BATCH, A_KV_HEADS),
        in_specs=[
            pl.BlockSpec((tq, gw), lambda b, h: (b, h)),
            pl.BlockSpec((SEQ, LANES), lambda b, h: (b, h)),
            pl.BlockSpec((SEQ, LANES), lambda b, h: (b, h)),
        ],
        out_specs=pl.BlockSpec((tq, gw), lambda b, h: (b, h)),
        out_shape=jax.ShapeDtypeStruct((N_CTX, D_MODEL), BF16),
        compiler_params=pltpu.CompilerParams(
            dimension_semantics=("arbitrary", "arbitrary"),
            vmem_limit_bytes=_vmem_limit(4 * tq * gw * 2 + 4 * SEQ * LANES * 4 + 4 * A_REP * tq * SEQ * 4)),
        name="attn_a_ctx",
    )(q, k[0], v[0])
    tq = 256
    nq = DEC_SEQ // tq
    row0 = N_CTX // tq
    t_all = PAST_LEN + DEC_SEQ
    o_lat = pl.pallas_call(
        functools.partial(_attn_a_kernel, tq=tq, has_cache=True),
        grid=(DEC_BATCH, A_KV_HEADS, nq),
        in_specs=[
            pl.BlockSpec((tq, gw), lambda b, h, i: (row0 + b * nq + i, h)),
            pl.BlockSpec((None, PAST_LEN, LANES), lambda b, h, i: (b, 0, h)),
            pl.BlockSpec((None, PAST_LEN, LANES), lambda b, h, i: (b, 0, h)),
            pl.BlockSpec((DEC_SEQ, LANES), lambda b, h, i: (b, h)),
            pl.BlockSpec((DEC_SEQ, LANES), lambda b, h, i: (b, h)),
        ],
        out_specs=pl.BlockSpec((tq, gw), lambda b, h, i: (b * nq + i, h)),
        out_shape=jax.ShapeDtypeStruct((N_LAT, D_MODEL), BF16),
        compiler_params=pltpu.CompilerParams(
            dimension_semantics=("arbitrary", "arbitrary", "arbitrary"),
            vmem_limit_bytes=_vmem_limit(4 * tq * gw * 2 + 4 * t_all * LANES * 4
                                         + 3 * A_REP * tq * t_all * 4)),
        name="attn_a_lat",
    )(q, cache_k, cache_v, k[1], v[1])
    return o_ctx, o_lat


def _attn_b_kernel(*refs, heads, has_cache, lam_init):
    if has_cache:
        q_ref, kc_ref, vc_ref, kn_ref, vn_ref, lam_ref, sub_ref, o_ref = refs
    else:
        q_ref, kn_ref, vn_ref, lam_ref, sub_ref, o_ref = refs
    scale = B_QK_DIM ** -0.5
    lp = lam_ref[...]
    lam = (jnp.exp(jnp.sum(lp[0:1] * lp[1:2], axis=-1, keepdims=True))
           - jnp.exp(jnp.sum(lp[2:3] * lp[3:4], axis=-1, keepdims=True)) + lam_init)
    first = lax.broadcasted_iota(jnp.int32, (1, LANES), 1) < B_QK_DIM
    for h in range(heads):
        sl = slice(h * LANES, (h + 1) * LANES)
        q = q_ref[:, sl]
        zero = jnp.zeros_like(q)
        qs = (jnp.where(first, q, zero), jnp.where(first, zero, q))
        kn = kn_ref[:, sl].astype(BF16)
        vn = vn_ref[:, sl].astype(BF16)
        if has_cache:
            kc = kc_ref[:, sl].astype(BF16)
            vc = vc_ref[:, sl].astype(BF16)
        w_new, w_old = None, None
        for c in range(2):
            s_new = lax.dot_general(qs[c], kn, _NT, preferred_element_type=F32) * scale
            m = jnp.max(s_new, axis=-1, keepdims=True)
            if has_cache:
                s_old = lax.dot_general(qs[c], kc, _NT, preferred_element_type=F32) * scale
                m = jnp.maximum(m, jnp.max(s_old, axis=-1, keepdims=True))
            e_new = jnp.exp(s_new - m)
            denom = jnp.sum(e_new, axis=-1, keepdims=True)
            if has_cache:
                e_old = jnp.exp(s_old - m)
                denom = denom + jnp.sum(e_old, axis=-1, keepdims=True)
            coef = 1.0 / denom if c == 0 else -lam / denom
            w_new = e_new * coef if c == 0 else w_new + e_new * coef
            if has_cache:
                w_old = e_old * coef if c == 0 else w_old + e_old * coef
        o = jnp.dot(w_new.astype(BF16), vn, preferred_element_type=F32)
        if has_cache:
            o = o + jnp.dot(w_old.astype(BF16), vc, preferred_element_type=F32)
        o = o * lax.rsqrt(jnp.mean(o * o, axis=-1, keepdims=True) + EPS) * sub_ref[...] * (1.0 - lam_init)
        o_ref[:, sl] = o.astype(o_ref.dtype)


def _attn_b(q, k, v, cache_k, cache_v, lam_p, subln, lam_init):
    heads = 4
    gw = heads * LANES
    tq = SEQ
    o_ctx = pl.pallas_call(
        functools.partial(_attn_b_kernel, heads=heads, has_cache=False, lam_init=lam_init),
        grid=(BATCH, B_HEADS // heads),
        in_specs=[
            pl.BlockSpec((tq, gw), lambda b, h: (b, h)),
            pl.BlockSpec((SEQ, gw), lambda b, h: (b, h)),
            pl.BlockSpec((SEQ, gw), lambda b, h: (b, h)),
            pl.BlockSpec((4, B_QK_DIM), lambda b, h: (0, 0)),
            pl.BlockSpec((1, LANES), lambda b, h: (0, 0)),
        ],
        out_specs=pl.BlockSpec((tq, gw), lambda b, h: (b, h)),
        out_shape=jax.ShapeDtypeStruct((N_CTX, D_MODEL), BF16),
        compiler_params=pltpu.CompilerParams(
            dimension_semantics=("arbitrary", "arbitrary"),
            vmem_limit_bytes=_vmem_limit(4 * tq * gw * 2 + 4 * SEQ * gw * 4 + 8 * heads * tq * SEQ * 4)),
        name="attn_b_ctx",
    )(q, k[0], v[0], lam_p, subln)
    tq = 512
    nq = DEC_SEQ // tq
    row0 = N_CTX // tq
    t_all = PAST_LEN + DEC_SEQ
    o_lat = pl.pallas_call(
        functools.partial(_attn_b_kernel, heads=1, has_cache=True, lam_init=lam_init),
        grid=(DEC_BATCH, B_HEADS, nq),
        in_specs=[
            pl.BlockSpec((tq, LANES), lambda b, h, i: (row0 + b * nq + i, h)),
            pl.BlockSpec((None, PAST_LEN, LANES), lambda b, h, i: (b, 0, h)),
            pl.BlockSpec((None, PAST_LEN, LANES), lambda b, h, i: (b, 0, h)),
            pl.BlockSpec((DEC_SEQ, LANES), lambda b, h, i: (b, h)),
            pl.BlockSpec((DEC_SEQ, LANES), lambda b, h, i: (b, h)),
            pl.BlockSpec((4, B_QK_DIM), lambda b, h, i: (0, 0)),
            pl.BlockSpec((1, LANES), lambda b, h, i: (0, 0)),
        ],
        out_specs=pl.BlockSpec((tq, LANES), lambda b, h, i: (b * nq + i, h)),
        out_shape=jax.ShapeDtypeStruct((N_LAT, D_MODEL), BF16),
        compiler_params=pltpu.CompilerParams(
            dimension_semantics=("arbitrary", "arbitrary", "arbitrary"),
            vmem_limit_bytes=_vmem_limit(4 * tq * LANES * 2 + 4 * t_all * LANES * 4
                                         + 6 * tq * t_all * 4)),
        name="attn_b_lat",
    )(q, cache_k, cache_v, k[1], v[1], lam_p, subln)
    return o_ctx, o_lat


def _diff_lambda_init(layer_idx):
    return 0.8 - 0.6 * math.exp(-0.3 * layer_idx)


def kernel(x_prompt, x_sample, cache_a_k, cache_a_v, cache_b_k, cache_b_v, c, c_ctx, ada_w, ada_b, ln_g, ln_b, ffn_w_in, ffn_w_out, a_w_qkv, a_q_norm, a_k_norm, a_w_o, b_w_qkv, b_lambda, b_subln, b_w_o):
    ln_g3 = ln_g.reshape(DEPTH * N_SUB, 1, D_MODEL)
    ln_b3 = ln_b.reshape(DEPTH * N_SUB, 1, D_MODEL)
    tables_a = _rope_tables(A_HEAD_DIM)
    tables_b = _rope_tables(B_QK_DIM)
    ones = jnp.ones((1, LANES), F32)

    cond = jnp.concatenate([c_ctx[None, :], c, jnp.zeros((COND_ROWS - N_GROUPS, D_MODEL), F32)], axis=0)
    mod = _modulation(cond, ada_w, ada_b)

    x = (x_prompt.reshape(N_CTX, D_MODEL), x_sample.reshape(N_LAT, D_MODEL))
    h = _premod(x[0], x[1], mod, 0, 0)

    new_kv = {"ak": [], "av": [], "bk": [], "bv": []}
    for i in range(DEPTH):
        j = i // N_MIXERS
        last = i + 1 == DEPTH
        a, w_down = _ffn_up(h, ffn_w_in, ffn_w_out, i, 0)
        x, h = _proj_deepnorm(a, w_down, (), x, mod, ln_g3, ln_b3, i, 0, MACARON_WEIGHT, (i, 1))
        if i % N_MIXERS == 0:
            qw = A_HEADS * A_HEAD_DIM
            kw = A_KV_HEADS * A_HEAD_DIM
            q = _qkv_proj(h, a_w_qkv, j, 0, qw, a_q_norm[j][None, :], tables_a, BF16, True, A_HEAD_DIM)
            k = _qkv_proj(h, a_w_qkv, j, qw, kw, a_k_norm[j][None, :], tables_a, (F32, BF16), True, A_HEAD_DIM)
            v = _qkv_proj(h, a_w_qkv, j, qw + kw, kw, ones, None, (F32, BF16), False, 0)
            o = _attn_a(q, k, v, cache_a_k[:, j].reshape(DEC_BATCH, PAST_LEN, kw),
                        cache_a_v[:, j].reshape(DEC_BATCH, PAST_LEN, kw))
            new_kv["ak"].append(k[0].reshape(BATCH, SEQ, A_KV_HEADS, A_HEAD_DIM))
            new_kv["av"].append(v[0].reshape(BATCH, SEQ, A_KV_HEADS, A_HEAD_DIM))
            w_o = a_w_o
        else:
            lam_init = _diff_lambda_init(i)
            qw = B_HEADS * 2 * B_QK_DIM
            q = _qkv_proj(h, b_w_qkv, j, 0, qw, ones, tables_b, BF16, False, B_QK_DIM)
            k = _qkv_proj(h, b_w_qkv, j, qw, qw, ones, tables_b, (F32, BF16), False, B_QK_DIM)
            v = _qkv_proj(h, b_w_qkv, j, 2 * qw, B_HEADS * B_V_DIM, ones, None, (F32, BF16), False, 0)
            o = _attn_b(q, k, v, cache_b_k[:, j].reshape(DEC_BATCH, PAST_LEN, qw),
                        cache_b_v[:, j].reshape(DEC_BATCH, PAST_LEN, B_HEADS * B_V_DIM),
                        b_lambda[j], b_subln[j][None, :], lam_init)
            new_kv["bk"].append(k[0].reshape(BATCH, SEQ, B_HEADS, 2, B_QK_DIM))
            new_kv["bv"].append(v[0].reshape(BATCH, SEQ, B_HEADS, B_V_DIM))
            w_o = b_w_o
        x, h = _proj_deepnorm(o, w_o, (j,), x, mod, ln_g3, ln_b3, i, 1, 1.0, (i, 2))
        a, w_down = _ffn_up(h, ffn_w_in, ffn_w_out, i, 1)
        x, h = _proj_deepnorm(a, w_down, (), x, mod, ln_g3, ln_b3, i, 2, MACARON_WEIGHT,
                              None if last else (i + 1, 0), split_out=last)

    y_prompt = x[0].reshape(BATCH, SEQ, D_MODEL)
    y_sample = x[1].reshape(DEC_BATCH, DEC_SEQ, D_MODEL)
    return (y_prompt, y_sample,
            jnp.stack(new_kv["ak"], axis=1), jnp.stack(new_kv["av"], axis=1),
            jnp.stack(new_kv["bk"], axis=1), jnp.stack(new_kv["bv"], axis=1))
```

```python
import functools
import math

import jax
import jax.numpy as jnp
import numpy as np
from jax import lax
from jax.experimental import pallas as pl
from jax.experimental.pallas import tpu as pltpu

D_MODEL = 2048
BATCH = 16
SEQ = 256
DEPTH = 2
DEC_BATCH = 2
DEC_SEQ = 2048
PAST_LEN = 256
GRID_W = 64
N_MIXERS = 2
N_SUB = 3
A_HEAD_DIM = 128
A_HEADS = D_MODEL // A_HEAD_DIM
A_KV_HEADS = 4
A_REP = A_HEADS // A_KV_HEADS
B_QK_DIM = 64
B_V_DIM = 2 * B_QK_DIM
B_HEADS = D_MODEL // B_V_DIM
D_FF = ((8 * D_MODEL // 3 + 127) // 128) * 128
ROPE_THETA = 10000.0
EPS = 1e-6
MACARON_WEIGHT = 0.5
DEEPNORM_ALPHA = (2 * DEPTH) ** 0.25

N_CTX = BATCH * SEQ
N_LAT = DEC_BATCH * DEC_SEQ
N_ROWS = N_CTX + N_LAT
N_GROUPS = 1 + DEC_BATCH
COND_ROWS = 8
LANES = 128
V7X_VMEM_BYTES = 64 * 1024 * 1024

F32 = jnp.float32
BF16 = jnp.bfloat16


def _vmem_limit(nbytes):
    return int(min(nbytes * 3 // 2 + (2 << 20), V7X_VMEM_BYTES - (4 << 20)))


def _group_of_row(r):
    return jnp.where(r < N_CTX, 0, 1 + (r - N_CTX) // DEC_SEQ)


def _mod_spec(layer, sub, kind, grid_rank):
    col = sub * 3 + kind
    return pl.BlockSpec((None, COND_ROWS, D_MODEL), lambda *idx: (layer, 0, col))


def _mod_row(ref, row0):
    return ref[pl.ds(_group_of_row(row0), 1), :]


def _mod_kernel(c_ref, w_ref, b_ref, o_ref):
    s = jax.nn.silu(c_ref[...]).astype(BF16)
    o_ref[...] = jnp.dot(s, w_ref[...].astype(BF16), preferred_element_type=F32) + b_ref[...]


def _modulation(cond, ada_w, ada_b):
    tn = 1024
    n_out = N_SUB * 3 * D_MODEL
    return pl.pallas_call(
        _mod_kernel,
        grid=(DEPTH, n_out // tn),
        in_specs=[
            pl.BlockSpec((COND_ROWS, D_MODEL), lambda l, j: (0, 0)),
            pl.BlockSpec((None, D_MODEL, tn), lambda l, j: (l, 0, j)),
            pl.BlockSpec((None, 1, tn), lambda l, j: (l, 0, j)),
        ],
        out_specs=pl.BlockSpec((None, COND_ROWS, tn), lambda l, j: (l, 0, j)),
        out_shape=jax.ShapeDtypeStruct((DEPTH, COND_ROWS, n_out), F32),
        compiler_params=pltpu.CompilerParams(
            dimension_semantics=("arbitrary", "arbitrary"),
            vmem_limit_bytes=_vmem_limit(2 * D_MODEL * tn * 4 + D_MODEL * tn * 2)),
        name="modulation",
    )(cond, ada_w, ada_b.reshape(DEPTH, 1, n_out))


def _split_specs(tm, width, row_axis):
    n_ctx = N_CTX // tm

    def ctx_map(*idx):
        return (jnp.minimum(idx[row_axis], n_ctx - 1), 0)

    def lat_map(*idx):
        return (jnp.maximum(idx[row_axis] - n_ctx, 0), 0)

    return pl.BlockSpec((tm, width), ctx_map), pl.BlockSpec((tm, width), lat_map)


def _premod_kernel(xc_ref, xl_ref, sc_ref, sh_ref, h_ref, *, tm):
    row0 = pl.program_id(0) * tm
    sc = 1.0 + _mod_row(sc_ref, row0)
    sh = _mod_row(sh_ref, row0)

    def emit(x_ref):
        h_ref[...] = (x_ref[...] * sc + sh).astype(BF16)

    pl.when(row0 < N_CTX)(lambda: emit(xc_ref))
    pl.when(row0 >= N_CTX)(lambda: emit(xl_ref))


def _premod(x_ctx, x_lat, mod, layer, sub):
    tm = 512
    return pl.pallas_call(
        functools.partial(_premod_kernel, tm=tm),
        grid=(N_ROWS // tm,),
        in_specs=[*_split_specs(tm, D_MODEL, 0),
                  _mod_spec(layer, sub, 1, 1), _mod_spec(layer, sub, 0, 1)],
        out_specs=pl.BlockSpec((tm, D_MODEL), lambda i: (i, 0)),
        out_shape=jax.ShapeDtypeStruct((N_ROWS, D_MODEL), BF16),
        compiler_params=pltpu.CompilerParams(
            dimension_semantics=("arbitrary",),
            vmem_limit_bytes=_vmem_limit(4 * tm * D_MODEL * 4 + 2 * tm * D_MODEL * 2)),
        name="premod",
    )(x_ctx, x_lat, mod, mod)


def _up_kernel(h_ref, wg_ref, wu_ref, wo_ref, a_ref, wob_ref, wbf_ref, *, tf, nf, tail):
    is_tail = pl.program_id(0) == nf - 1

    @pl.when(pl.program_id(1) == 0)
    def _():
        wbf_ref[:, :tf] = wg_ref[...].astype(BF16)
        wbf_ref[:, tf:] = wu_ref[...].astype(BF16)

        @pl.when(jnp.logical_not(is_tail))
        def _():
            wob_ref[...] = wo_ref[...].astype(BF16)

        @pl.when(is_tail)
        def _():
            wob_ref[:tail, :] = wo_ref[tf - tail:, :].astype(BF16)

    gu = jnp.dot(h_ref[...], wbf_ref[...], preferred_element_type=F32)
    a_ref[...] = (jax.nn.silu(gu[:, :tf]) * gu[:, tf:]).astype(BF16)

    @pl.when(is_tail)
    def _():
        a_ref[:, :tail] = a_ref[:, tf - tail:]


def _ffn_up(h, w_in, w_out, layer, half):
    tm, tf = 1024, 512
    nf = pl.cdiv(D_FF, tf)
    tail = D_FF - (nf - 1) * tf
    start = lambda j: pl.multiple_of(jnp.minimum(j * tf, D_FF - tf), LANES)
    ustart = lambda j: pl.multiple_of(D_FF + jnp.minimum(j * tf, D_FF - tf), LANES)
    elem = pl.Element
    return pl.pallas_call(
        functools.partial(_up_kernel, tf=tf, nf=nf, tail=tail),
        grid=(nf, N_ROWS // tm),
        in_specs=[
            pl.BlockSpec((tm, D_MODEL), lambda j, i: (i, 0)),
            pl.BlockSpec((None, None, elem(D_MODEL), elem(tf)), lambda j, i: (layer, half, 0, start(j))),
            pl.BlockSpec((None, None, elem(D_MODEL), elem(tf)), lambda j, i: (layer, half, 0, ustart(j))),
            pl.BlockSpec((None, None, elem(tf), elem(D_MODEL)), lambda j, i: (layer, half, start(j), 0)),
        ],
        out_specs=[pl.BlockSpec((tm, tf), lambda j, i: (i, j)),
                   pl.BlockSpec((tf, D_MODEL), lambda j, i: (j, 0))],
        out_shape=[jax.ShapeDtypeStruct((N_ROWS, D_FF), BF16),
                   jax.ShapeDtypeStruct((D_FF, D_MODEL), BF16)],
        scratch_shapes=[pltpu.VMEM((D_MODEL, 2 * tf), BF16)],
        compiler_params=pltpu.CompilerParams(
            dimension_semantics=("arbitrary", "arbitrary"),
            vmem_limit_bytes=_vmem_limit(2 * tm * D_MODEL * 2 + 6 * D_MODEL * tf * 4 + 2 * D_MODEL * tf * 2
                                         + 2 * tm * tf * 2 + 2 * tf * D_MODEL * 2 + 3 * tm * tf * 4)),
        name="ffn_up",
    )(h, w_in, w_in, w_out)


def _deepnorm_kernel(*refs, tm, rows, coef, cast_w, split_lhs, split_x, split_out, emit_h):
    refs = list(refs)
    take = lambda n: [refs.pop(0) for _ in range(n)]
    lhs_refs = take(2 if split_lhs else 1)
    (w_ref,) = take(1)
    x_refs = take(2 if split_x else 1)
    gate_ref, lng_ref, lnb_ref = take(3)
    sc_ref, sh_ref = take(2) if emit_h else (None, None)
    xo_refs = take(2 if split_out else 1)
    (ho_ref,) = take(1) if emit_h else (None,)
    (wbf_ref,) = take(1) if cast_w else (w_ref,)

    if cast_w:
        @pl.when(pl.program_id(0) == 0)
        def _():
            wbf_ref[...] = w_ref[...].astype(BF16)

    row0 = pl.program_id(0) * tm
    gate = coef * _mod_row(gate_ref, row0)
    if emit_h:
        sc = 1.0 + _mod_row(sc_ref, row0)
        sh = _mod_row(sh_ref, row0)

    def body(lhs_ref, x_ref, xo_ref):
        for r in range(tm // rows):
            rs = slice(r * rows, (r + 1) * rows)
            y = jnp.dot(lhs_ref[rs, :], wbf_ref[...], preferred_element_type=F32)
            z = DEEPNORM_ALPHA * x_ref[rs, :] + gate * y
            mu = jnp.mean(z, axis=-1, keepdims=True)
            zc = z - mu
            var = jnp.mean(zc * zc, axis=-1, keepdims=True)
            o = zc * lax.rsqrt(var + EPS) * lng_ref[...] + lnb_ref[...]
            xo_ref[rs, :] = o
            if emit_h:
                ho_ref[rs, :] = (o * sc + sh).astype(BF16)

    if split_lhs or split_x or split_out:
        pl.when(row0 < N_CTX)(lambda: body(lhs_refs[0], x_refs[0], xo_refs[0]))
        pl.when(row0 >= N_CTX)(lambda: body(lhs_refs[-1], x_refs[-1], xo_refs[-1]))
    else:
        body(lhs_refs[0], x_refs[0], xo_refs[0])


def _proj_deepnorm(lhs, w, w_index, x, mod, ln_g, ln_b, layer, sub, coef, next_mod, split_out=False):
    split_lhs, split_x = isinstance(lhs, tuple), isinstance(x, tuple)
    k = (lhs[0] if split_lhs else lhs).shape[1]
    tm = rows = 256
    cast_w = w.dtype != BF16
    emit_h = next_mod is not None
    lead = (None,) * len(w_index)
    row_spec = lambda width: pl.BlockSpec((tm, width), lambda i: (i, 0))
    in_specs, args = [], []
    if split_lhs:
        in_specs += list(_split_specs(tm, k, 0)); args += list(lhs)
    else:
        in_specs.append(row_spec(k)); args.append(lhs)
    in_specs.append(pl.BlockSpec(lead + (k, D_MODEL), lambda i: w_index + (0, 0), pipeline_mode=pl.Buffered(1)))
    args.append(w)
    if split_x:
        in_specs += list(_split_specs(tm, D_MODEL, 0)); args += list(x)
    else:
        in_specs.append(row_spec(D_MODEL)); args.append(x)
    ln_spec = pl.BlockSpec((None, 1, D_MODEL), lambda i: (layer * N_SUB + sub, 0, 0))
    in_specs += [_mod_spec(layer, sub, 2, 1), ln_spec, ln_spec]
    args += [mod, ln_g, ln_b]
    if emit_h:
        in_specs += [_mod_spec(next_mod[0], next_mod[1], 1, 1), _mod_spec(next_mod[0], next_mod[1], 0, 1)]
        args += [mod, mod]
    if split_out:
        out_specs = list(_split_specs(tm, D_MODEL, 0))
        out_shape = [jax.ShapeDtypeStruct((N_CTX, D_MODEL), F32), jax.ShapeDtypeStruct((N_LAT, D_MODEL), F32)]
    else:
        out_specs = [row_spec(D_MODEL)]
        out_shape = [jax.ShapeDtypeStruct((N_ROWS, D_MODEL), F32)]
    if emit_h:
        out_specs.append(row_spec(D_MODEL))
        out_shape.append(jax.ShapeDtypeStruct((N_ROWS, D_MODEL), BF16))
    w_bytes = k * D_MODEL * (6 if cast_w else 2)
    est = (2 * (1 + split_lhs) * tm * k * 2 + w_bytes + 2 * (2 + split_x + split_out) * tm * D_MODEL * 4
           + 2 * tm * D_MODEL * 2 + 4 * tm * D_MODEL * 4)
    outs = pl.pallas_call(
        functools.partial(_deepnorm_kernel, tm=tm, rows=rows, coef=coef, cast_w=cast_w, split_lhs=split_lhs,
                          split_x=split_x, split_out=split_out, emit_h=emit_h),
        grid=(N_ROWS // tm,),
        in_specs=in_specs,
        out_specs=out_specs,
        out_shape=out_shape,
        scratch_shapes=[pltpu.VMEM((k, D_MODEL), BF16)] if cast_w else [],
        compiler_params=pltpu.CompilerParams(
            dimension_semantics=("arbitrary",), vmem_limit_bytes=_vmem_limit(est)),
        name="proj_deepnorm",
    )(*args)
    x_new = (outs[0], outs[1]) if split_out else outs[0]
    return x_new, (outs[-1] if emit_h else None)


def _rope_tables(dim):
    rows = DEC_SEQ // GRID_W
    row = np.repeat(np.arange(rows, dtype=np.float64), GRID_W)
    col = np.tile(np.arange(GRID_W, dtype=np.float64), rows)
    quarter = dim // 4
    inv = ROPE_THETA ** (-np.arange(quarter, dtype=np.float64) / quarter)
    ang = np.concatenate([row[:, None] * inv, col[:, None] * inv], axis=-1)
    cos, sin, zero = np.cos(ang), np.sin(ang), np.zeros_like(ang)
    reps = LANES // dim
    full = lambda a, b: np.tile(np.concatenate([a, b], axis=-1), (1, reps)).astype(np.float32)
    return jnp.asarray(full(cos, cos)), jnp.asarray(full(-sin, zero)), jnp.asarray(full(zero, sin))


DENSE = -1


def _qkv_kernel(*refs, tm, tn, norm, rope_dim, cache_split, out_scale):
    refs = list(refs)
    take = lambda n: [refs.pop(0) for _ in range(n)]
    h_ref, w_ref, gain_ref = take(3)
    cos_ref, slo_ref, shi_ref = take(3) if rope_dim else (None, None, None)
    (o_ref,) = take(1)
    (cache_ref,) = take(1) if cache_split else (None,)
    (wbf_ref,) = take(1)

    @pl.when(pl.program_id(1) == 0)
    def _():
        wbf_ref[...] = w_ref[...].astype(BF16)

    def finish(rope, write_cache):
        for r in range(tm // SEQ):
            rs = slice(r * SEQ, (r + 1) * SEQ)
            y = jnp.dot(h_ref[rs, :], wbf_ref[...], preferred_element_type=F32)
            for c in range(tn // LANES):
                yc = y[:, c * LANES:(c + 1) * LANES]
                if norm:
                    yc = yc * lax.rsqrt(jnp.mean(yc * yc, axis=-1, keepdims=True) + EPS) * gain_ref[...]
                if rope:
                    half = rope_dim // 2
                    yc = (yc * cos_ref[rs, :]
                          + pltpu.roll(yc, LANES - half, 1) * slo_ref[rs, :]
                          + pltpu.roll(yc, half, 1) * shi_ref[rs, :])
                if out_scale != 1.0:
                    yc = yc * out_scale
                o_ref[rs, c * LANES:(c + 1) * LANES] = yc.astype(o_ref.dtype)
                if write_cache and cache_split == DENSE:
                    cache_ref[rs, c * LANES:(c + 1) * LANES] = yc
                elif write_cache:
                    width = LANES // cache_split
                    for p in range(cache_split):
                        piece = yc[:, p * width:(p + 1) * width]
                        if cache_split == 1:
                            cache_ref[r, :, c, :] = piece
                        else:
                            cache_ref[r, :, c, p, :] = piece

    if rope_dim or cache_split:
        is_latent = pl.program_id(1) * tm >= N_CTX
        pl.when(jnp.logical_not(is_latent))(lambda: finish(False, bool(cache_split)))
        pl.when(is_latent)(lambda: finish(bool(rope_dim), False))
    else:
        finish(False, False)


def _qkv_proj(h, w, layer_j, col0, width, gain, tables, norm, rope_dim, out_scale=1.0, cache_split=0):
    tm, tn = 1024, 512
    n0 = col0 // tn
    in_specs = [
        pl.BlockSpec((tm, D_MODEL), lambda j, i: (i, 0)),
        pl.BlockSpec((None, D_MODEL, tn), lambda j, i: (layer_j, 0, n0 + j)),
        pl.BlockSpec((1, LANES), lambda j, i: (0, 0)),
    ]
    args = [h, w, gain]
    if rope_dim:
        def tab_map(j, i):
            r = i * tm
            return (jnp.where(r >= N_CTX, ((r - N_CTX) % DEC_SEQ) // tm, 0), 0)
        in_specs += [pl.BlockSpec((tm, LANES), tab_map)] * 3
        args += list(tables)
    out_specs = [pl.BlockSpec((tm, tn), lambda j, i: (i, j))]
    out_shape = [jax.ShapeDtypeStruct((N_ROWS, width), BF16)]
    n_ctx = N_CTX // tm
    if cache_split == DENSE:
        out_specs.append(pl.BlockSpec((tm, tn), lambda j, i: (jnp.minimum(i, n_ctx - 1), j)))
        out_shape.append(jax.ShapeDtypeStruct((N_CTX, width), F32))
    elif cache_split:
        tail = (LANES,) if cache_split == 1 else (cache_split, LANES // cache_split)
        zeros = (0,) * len(tail)
        out_specs.append(pl.BlockSpec((tm // SEQ, None, SEQ, tn // LANES) + tail,
                                      lambda j, i: (jnp.minimum(i, n_ctx - 1), 0, 0, j) + zeros))
        out_shape.append(jax.ShapeDtypeStruct((BATCH, 1, SEQ, width // LANES) + tail, F32))
    outs = pl.pallas_call(
        functools.partial(_qkv_kernel, tm=tm, tn=tn, norm=norm, rope_dim=rope_dim,
                          cache_split=cache_split, out_scale=out_scale),
        grid=(width // tn, N_ROWS // tm),
        in_specs=in_specs,
        out_specs=out_specs,
        out_shape=out_shape,
        scratch_shapes=[pltpu.VMEM((D_MODEL, tn), BF16)],
        compiler_params=pltpu.CompilerParams(
            dimension_semantics=("arbitrary", "arbitrary"),
            vmem_limit_bytes=_vmem_limit(2 * tm * D_MODEL * 2 + 2 * D_MODEL * tn * 4 + D_MODEL * tn * 2
                                         + 2 * tm * tn * 2 + 2 * abs(cache_split) * tm * tn * 4
                                         + 6 * tm * LANES * 4 + 3 * tm * tn * 4)),
        name="qkv_proj",
    )(*args)
    return (outs[0], outs[1]) if cache_split else outs[0]


_NT = (((1,), (1,)), ((), ()))


LOG2_E = math.log2(math.e)


def _with_ones(v):
    v = v.astype(BF16)
    return jnp.concatenate([v, jnp.ones_like(v)], axis=1)


def _softmax_pv(q, keys, vals1):
    logits = [lax.dot_general(q, k, _NT, preferred_element_type=F32) for k in keys]
    m = functools.reduce(jnp.maximum, [jnp.max(s, axis=-1, keepdims=True) for s in logits])
    acc = None
    for s, v1 in zip(logits, vals1):
        pv = jnp.dot(jnp.exp2(s - m).astype(BF16), v1, preferred_element_type=F32)
        acc = pv if acc is None else acc + pv
    return acc[:, :LANES] / acc[:, LANES:]


def _attn_a_kernel(*refs, has_cache):
    if has_cache:
        q_ref, kc_ref, vc_ref, kn_ref, vn_ref, o_ref = refs
    else:
        q_ref, kn_ref, vn_ref, o_ref = refs
    keys = [kn_ref[...].astype(BF16)]
    vals1 = [_with_ones(vn_ref[...])]
    if has_cache:
        keys.append(kc_ref[...].astype(BF16))
        vals1.append(_with_ones(vc_ref[...]))
    for g in range(A_REP):
        sl = slice(g * LANES, (g + 1) * LANES)
        o_ref[:, sl] = _softmax_pv(q_ref[:, sl], keys, vals1).astype(o_ref.dtype)


def _attn_a(q, k, v, cache_k, cache_v):
    gw = A_REP * LANES
    tq = SEQ
    o_ctx = pl.pallas_call(
        functools.partial(_attn_a_kernel, has_cache=False),
        grid=(BATCH, A_KV_HEADS),
        in_specs=[
            pl.BlockSpec((tq, gw), lambda b, h: (b, h)),
            pl.BlockSpec((SEQ, LANES), lambda b, h: (b, h)),
            pl.BlockSpec((SEQ, LANES), lambda b, h: (b, h)),
        ],
        out_specs=pl.BlockSpec((tq, gw), lambda b, h: (b, h)),
        out_shape=jax.ShapeDtypeStruct((N_CTX, D_MODEL), BF16),
        compiler_params=pltpu.CompilerParams(
            dimension_semantics=("arbitrary", "arbitrary"),
            vmem_limit_bytes=_vmem_limit(4 * tq * gw * 2 + 4 * SEQ * LANES * 4 + 4 * A_REP * tq * SEQ * 4)),
        name="attn_a_ctx",
    )(q, k, v)
    tq = 512
    nq = DEC_SEQ // tq
    row0 = N_CTX // tq
    t_all = PAST_LEN + DEC_SEQ
    o_lat = pl.pallas_call(
        functools.partial(_attn_a_kernel, has_cache=True),
        grid=(DEC_BATCH, A_KV_HEADS, nq),
        in_specs=[
            pl.BlockSpec((tq, gw), lambda b, h, i: (row0 + b * nq + i, h)),
            pl.BlockSpec((None, PAST_LEN, LANES), lambda b, h, i: (b, 0, h)),
            pl.BlockSpec((None, PAST_LEN, LANES), lambda b, h, i: (b, 0, h)),
            pl.BlockSpec((DEC_SEQ, LANES), lambda b, h, i: (N_CTX // DEC_SEQ + b, h)),
            pl.BlockSpec((DEC_SEQ, LANES), lambda b, h, i: (N_CTX // DEC_SEQ + b, h)),
        ],
        out_specs=pl.BlockSpec((tq, gw), lambda b, h, i: (b * nq + i, h)),
        out_shape=jax.ShapeDtypeStruct((N_LAT, D_MODEL), BF16),
        compiler_params=pltpu.CompilerParams(
            dimension_semantics=("arbitrary", "arbitrary", "arbitrary"),
            vmem_limit_bytes=_vmem_limit(4 * tq * gw * 2 + 4 * t_all * LANES * 4
                                         + 3 * A_REP * tq * t_all * 4)),
        name="attn_a_lat",
    )(q, cache_k, cache_v, k, v)
    return o_ctx, o_lat


def _attn_b_kernel(*refs, heads, has_cache, lam_init):
    if has_cache:
        q_ref, kc_ref, vc_ref, kn_ref, vn_ref, lam_ref, sub_ref, o_ref = refs
    else:
        q_ref, kn_ref, vn_ref, lam_ref, sub_ref, o_ref = refs
    lp = lam_ref[...]
    lam = (jnp.exp(jnp.sum(lp[0:1] * lp[1:2], axis=-1, keepdims=True))
           - jnp.exp(jnp.sum(lp[2:3] * lp[3:4], axis=-1, keepdims=True)) + lam_init)
    first = lax.broadcasted_iota(jnp.int32, (1, LANES), 1) < B_QK_DIM
    for h in range(heads):
        sl = slice(h * LANES, (h + 1) * LANES)
        q = q_ref[:, sl]
        zero = jnp.zeros_like(q)
        keys = [kn_ref[:, sl].astype(BF16)]
        vals1 = [_with_ones(vn_ref[:, sl])]
        if has_cache:
            keys.append(kc_ref[:, sl].astype(BF16))
            vals1.append(_with_ones(vc_ref[:, sl]))
        o1 = _softmax_pv(jnp.where(first, q, zero), keys, vals1)
        o2 = _softmax_pv(jnp.where(first, zero, q), keys, vals1)
        o = o1 - lam * o2
        o = o * lax.rsqrt(jnp.mean(o * o, axis=-1, keepdims=True) + EPS) * sub_ref[...] * (1.0 - lam_init)
        o_ref[:, sl] = o.astype(o_ref.dtype)


def _attn_b(q, k, v, cache_k, cache_v, lam_p, subln, lam_init):
    heads = 4
    gw = heads * LANES
    tq = SEQ
    o_ctx = pl.pallas_call(
        functools.partial(_attn_b_kernel, heads=heads, has_cache=False, lam_init=lam_init),
        grid=(BATCH, B_HEADS // heads),
        in_specs=[
            pl.BlockSpec((tq, gw), lambda b, h: (b, h)),
            pl.BlockSpec((SEQ, gw), lambda b, h: (b, h)),
            pl.BlockSpec((SEQ, gw), lambda b, h: (b, h)),
            pl.BlockSpec((4, B_QK_DIM), lambda b, h: (0, 0)),
            pl.BlockSpec((1, LANES), lambda b, h: (0, 0)),
        ],
        out_specs=pl.BlockSpec((tq, gw), lambda b, h: (b, h)),
        out_shape=jax.ShapeDtypeStruct((N_CTX, D_MODEL), BF16),
        compiler_params=pltpu.CompilerParams(
            dimension_semantics=("arbitrary", "arbitrary"),
            vmem_limit_bytes=_vmem_limit(4 * tq * gw * 2 + 4 * SEQ * gw * 4 + 8 * heads * tq * SEQ * 4)),
        name="attn_b_ctx",
    )(q, k, v, lam_p, subln)
    tq = 512
    nq = DEC_SEQ // tq
    row0 = N_CTX // tq
    t_all = PAST_LEN + DEC_SEQ
    heads = 2
    gw = heads * LANES
    o_lat = pl.pallas_call(
        functools.partial(_attn_b_kernel, heads=heads, has_cache=True, lam_init=lam_init),
        grid=(DEC_BATCH, B_HEADS // heads, nq),
        in_specs=[
            pl.BlockSpec((tq, gw), lambda b, h, i: (row0 + b * nq + i, h)),
            pl.BlockSpec((None, PAST_LEN, gw), lambda b, h, i: (b, 0, h)),
            pl.BlockSpec((None, PAST_LEN, gw), lambda b, h, i: (b, 0, h)),
            pl.BlockSpec((DEC_SEQ, gw), lambda b, h, i: (N_CTX // DEC_SEQ + b, h)),
            pl.BlockSpec((DEC_SEQ, gw), lambda b, h, i: (N_CTX // DEC_SEQ + b, h)),
            pl.BlockSpec((4, B_QK_DIM), lambda b, h, i: (0, 0)),
            pl.BlockSpec((1, LANES), lambda b, h, i: (0, 0)),
        ],
        out_specs=pl.BlockSpec((tq, gw), lambda b, h, i: (b * nq + i, h)),
        out_shape=jax.ShapeDtypeStruct((N_LAT, D_MODEL), BF16),
        compiler_params=pltpu.CompilerParams(
            dimension_semantics=("arbitrary", "arbitrary", "arbitrary"),
            vmem_limit_bytes=_vmem_limit(4 * tq * gw * 2 + 4 * t_all * gw * 4
                                         + 3 * 2 * heads * tq * t_all * 4)),
        name="attn_b_lat",
    )(q, cache_k, cache_v, k, v, lam_p, subln)
    return o_ctx, o_lat


def _diff_lambda_init(layer_idx):
    return 0.8 - 0.6 * math.exp(-0.3 * layer_idx)


def kernel(x_prompt, x_sample, cache_a_k, cache_a_v, cache_b_k, cache_b_v, c, c_ctx, ada_w, ada_b, ln_g, ln_b, ffn_w_in, ffn_w_out, a_w_qkv, a_q_norm, a_k_norm, a_w_o, b_w_qkv, b_lambda, b_subln, b_w_o):
    ln_g3 = ln_g.reshape(DEPTH * N_SUB, 1, D_MODEL)
    ln_b3 = ln_b.reshape(DEPTH * N_SUB, 1, D_MODEL)
    tables_a = _rope_tables(A_HEAD_DIM)
    tables_b = _rope_tables(B_QK_DIM)
    ones = jnp.ones((1, LANES), F32)

    cond = jnp.concatenate([c_ctx[None, :], c, jnp.zeros((COND_ROWS - N_GROUPS, D_MODEL), F32)], axis=0)
    mod = _modulation(cond, ada_w, ada_b)

    x = (x_prompt.reshape(N_CTX, D_MODEL), x_sample.reshape(N_LAT, D_MODEL))
    h = _premod(x[0], x[1], mod, 0, 0)

    new_kv = {"ak": [], "av": [], "bk": [], "bv": []}
    for i in range(DEPTH):
        j = i // N_MIXERS
        last = i + 1 == DEPTH
        a, w_down = _ffn_up(h, ffn_w_in, ffn_w_out, i, 0)
        x, h = _proj_deepnorm(a, w_down, (), x, mod, ln_g3, ln_b3, i, 0, MACARON_WEIGHT, (i, 1))
        if i % N_MIXERS == 0:
            qw = A_HEADS * A_HEAD_DIM
            kw = A_KV_HEADS * A_HEAD_DIM
            q = _qkv_proj(h, a_w_qkv, j, 0, qw, a_q_norm[j][None, :], tables_a, True, A_HEAD_DIM,
                          out_scale=A_HEAD_DIM ** -0.5 * LOG2_E)
            k, new_k = _qkv_proj(h, a_w_qkv, j, qw, kw, a_k_norm[j][None, :], tables_a, True, A_HEAD_DIM,
                                 cache_split=1)
            v, new_v = _qkv_proj(h, a_w_qkv, j, qw + kw, kw, ones, None, False, 0, cache_split=1)
            o = _attn_a(q, k, v, cache_a_k[:, j].reshape(DEC_BATCH, PAST_LEN, kw),
                        cache_a_v[:, j].reshape(DEC_BATCH, PAST_LEN, kw))
            new_kv["ak"].append(new_k)
            new_kv["av"].append(new_v)
            w_o = a_w_o
        else:
            lam_init = _diff_lambda_init(i)
            qw = B_HEADS * 2 * B_QK_DIM
            q = _qkv_proj(h, b_w_qkv, j, 0, qw, ones, tables_b, False, B_QK_DIM,
                          out_scale=B_QK_DIM ** -0.5 * LOG2_E)
            k, new_k = _qkv_proj(h, b_w_qkv, j, qw, qw, ones, tables_b, False, B_QK_DIM, cache_split=2)
            v, new_v = _qkv_proj(h, b_w_qkv, j, 2 * qw, B_HEADS * B_V_DIM, ones, None, False, 0,
                                 cache_split=DENSE)
            new_v = new_v.reshape(BATCH, 1, SEQ, B_HEADS, B_V_DIM)
            o = _attn_b(q, k, v, cache_b_k[:, j].reshape(DEC_BATCH, PAST_LEN, qw),
                        cache_b_v[:, j].reshape(DEC_BATCH, PAST_LEN, B_HEADS * B_V_DIM),
                        b_lambda[j], b_subln[j][None, :], lam_init)
            new_kv["bk"].append(new_k)
            new_kv["bv"].append(new_v)
            w_o = b_w_o
        x, h = _proj_deepnorm(o, w_o, (j,), x, mod, ln_g3, ln_b3, i, 1, 1.0, (i, 2))
        a, w_down = _ffn_up(h, ffn_w_in, ffn_w_out, i, 1)
        x, h = _proj_deepnorm(a, w_down, (), x, mod, ln_g3, ln_b3, i, 2, MACARON_WEIGHT,
                              None if last else (i + 1, 0), split_out=last)

    y_prompt = x[0].reshape(BATCH, SEQ, D_MODEL)
    y_sample = x[1].reshape(DEC_BATCH, DEC_SEQ, D_MODEL)
    return (y_prompt, y_sample,
            jnp.concatenate(new_kv["ak"], axis=1), jnp.concatenate(new_kv["av"], axis=1),
            jnp.concatenate(new_kv["bk"], axis=1), jnp.concatenate(new_kv["bv"], axis=1))
```

```python
import functools
import math

import jax
import jax.numpy as jnp
import numpy as np
from jax import lax
from jax.experimental import pallas as pl
from jax.experimental.pallas import tpu as pltpu

D_MODEL = 2048
BATCH = 16
SEQ = 256
DEPTH = 2
DEC_BATCH = 2
DEC_SEQ = 2048
PAST_LEN = 256
GRID_W = 64
N_MIXERS = 2
N_SUB = 3
A_HEAD_DIM = 128
A_HEADS = D_MODEL // A_HEAD_DIM
A_KV_HEADS = 4
A_REP = A_HEADS // A_KV_HEADS
B_QK_DIM = 64
B_V_DIM = 2 * B_QK_DIM
B_HEADS = D_MODEL // B_V_DIM
D_FF = ((8 * D_MODEL // 3 + 127) // 128) * 128
ROPE_THETA = 10000.0
EPS = 1e-6
MACARON_WEIGHT = 0.5
DEEPNORM_ALPHA = (2 * DEPTH) ** 0.25

N_CTX = BATCH * SEQ
N_LAT = DEC_BATCH * DEC_SEQ
N_ROWS = N_CTX + N_LAT
N_GROUPS = 1 + DEC_BATCH
COND_ROWS = 8
LANES = 128
V7X_VMEM_BYTES = 64 * 1024 * 1024

F32 = jnp.float32
BF16 = jnp.bfloat16


def _vmem_limit(nbytes):
    return int(min(nbytes * 3 // 2 + (2 << 20), V7X_VMEM_BYTES - (4 << 20)))


def _group_of_row(r):
    return jnp.where(r < N_CTX, 0, 1 + (r - N_CTX) // DEC_SEQ)


def _mod_spec(layer, sub, kind, grid_rank):
    col = sub * 3 + kind
    return pl.BlockSpec((None, COND_ROWS, D_MODEL), lambda *idx: (layer, 0, col))


def _mod_row(ref, row0):
    return ref[pl.ds(_group_of_row(row0), 1), :]


def _mod_kernel(c_ref, w_ref, b_ref, o_ref):
    s = jax.nn.silu(c_ref[...]).astype(BF16)
    o_ref[...] = jnp.dot(s, w_ref[...].astype(BF16), preferred_element_type=F32) + b_ref[...]


def _modulation(cond, ada_w, ada_b):
    tn = 1024
    n_out = N_SUB * 3 * D_MODEL
    return pl.pallas_call(
        _mod_kernel,
        grid=(DEPTH, n_out // tn),
        in_specs=[
            pl.BlockSpec((COND_ROWS, D_MODEL), lambda l, j: (0, 0)),
            pl.BlockSpec((None, D_MODEL, tn), lambda l, j: (l, 0, j)),
            pl.BlockSpec((None, 1, tn), lambda l, j: (l, 0, j)),
        ],
        out_specs=pl.BlockSpec((None, COND_ROWS, tn), lambda l, j: (l, 0, j)),
        out_shape=jax.ShapeDtypeStruct((DEPTH, COND_ROWS, n_out), F32),
        compiler_params=pltpu.CompilerParams(
            dimension_semantics=("arbitrary", "arbitrary"),
            vmem_limit_bytes=_vmem_limit(2 * D_MODEL * tn * 4 + D_MODEL * tn * 2)),
        name="modulation",
    )(cond, ada_w, ada_b.reshape(DEPTH, 1, n_out))


def _split_specs(tm, width, row_axis):
    n_ctx = N_CTX // tm

    def ctx_map(*idx):
        return (jnp.minimum(idx[row_axis], n_ctx - 1), 0)

    def lat_map(*idx):
        return (jnp.maximum(idx[row_axis] - n_ctx, 0), 0)

    return pl.BlockSpec((tm, width), ctx_map), pl.BlockSpec((tm, width), lat_map)


def _premod_kernel(xc_ref, xl_ref, sc_ref, sh_ref, h_ref, *, tm):
    row0 = pl.program_id(0) * tm
    sc = 1.0 + _mod_row(sc_ref, row0)
    sh = _mod_row(sh_ref, row0)

    def emit(x_ref):
        h_ref[...] = (x_ref[...] * sc + sh).astype(BF16)

    pl.when(row0 < N_CTX)(lambda: emit(xc_ref))
    pl.when(row0 >= N_CTX)(lambda: emit(xl_ref))


def _premod(x_ctx, x_lat, mod, layer, sub):
    tm = 512
    return pl.pallas_call(
        functools.partial(_premod_kernel, tm=tm),
        grid=(N_ROWS // tm,),
        in_specs=[*_split_specs(tm, D_MODEL, 0),
                  _mod_spec(layer, sub, 1, 1), _mod_spec(layer, sub, 0, 1)],
        out_specs=pl.BlockSpec((tm, D_MODEL), lambda i: (i, 0)),
        out_shape=jax.ShapeDtypeStruct((N_ROWS, D_MODEL), BF16),
        compiler_params=pltpu.CompilerParams(
            dimension_semantics=("arbitrary",),
            vmem_limit_bytes=_vmem_limit(4 * tm * D_MODEL * 4 + 2 * tm * D_MODEL * 2)),
        name="premod",
    )(x_ctx, x_lat, mod, mod)


def _up_kernel(h_ref, wg_ref, wu_ref, wo_ref, a_ref, wob_ref, wbf_ref, *, tf, nf, kc, tail):
    j, i = pl.program_id(0), pl.program_id(1)

    def stage():
        rows = pl.ds(pl.multiple_of(i * kc, kc), kc)
        wbf_ref[j % 2, rows, :tf] = wg_ref[...].astype(BF16)
        wbf_ref[j % 2, rows, tf:] = wu_ref[...].astype(BF16)
        wob_ref[...] = wo_ref[...].astype(BF16)

    def multiply():
        gu = jnp.dot(h_ref[...], wbf_ref[(j + 1) % 2], preferred_element_type=F32)
        a_ref[...] = (jax.nn.silu(gu[:, :tf]) * gu[:, tf:]).astype(BF16)

    @pl.when(j == 0)
    def _():
        stage()

    @pl.when(jnp.logical_and(j > 0, j < nf))
    def _():
        stage()
        multiply()

    @pl.when(j == nf)
    def _():
        multiply()
        a_ref[:, :tail] = a_ref[:, tf - tail:]


def _ffn_up(h, w_in, w_out, layer, half):
    tm, tf = 1024, 512
    nm = N_ROWS // tm
    nf = pl.cdiv(D_FF, tf)
    tail = D_FF - (nf - 1) * tf
    kc = D_MODEL // nm
    dc = 64
    n_down = D_FF // dc
    assert D_FF % dc == 0 and n_down <= nf * nm
    staged = lambda j: jnp.minimum(j, nf - 1)
    krow = lambda j, i: pl.multiple_of(jnp.where(j < nf, i, nm - 1) * kc, kc)
    start = lambda j: pl.multiple_of(jnp.minimum(staged(j) * tf, D_FF - tf), LANES)
    ustart = lambda j: pl.multiple_of(D_FF + jnp.minimum(staged(j) * tf, D_FF - tf), LANES)
    down = lambda j, i: jnp.minimum(j * nm + i, n_down - 1)
    rows = lambda j, i: jnp.where(j > 0, i, 0)
    elem = pl.Element
    return pl.pallas_call(
        functools.partial(_up_kernel, tf=tf, nf=nf, kc=kc, tail=tail),
        grid=(nf + 1, nm),
        in_specs=[
            pl.BlockSpec((tm, D_MODEL), lambda j, i: (rows(j, i), 0)),
            pl.BlockSpec((None, None, elem(kc), elem(tf)), lambda j, i: (layer, half, krow(j, i), start(j))),
            pl.BlockSpec((None, None, elem(kc), elem(tf)), lambda j, i: (layer, half, krow(j, i), ustart(j))),
            pl.BlockSpec((None, None, dc, D_MODEL), lambda j, i: (layer, half, down(j, i), 0)),
        ],
        out_specs=[pl.BlockSpec((tm, tf), lambda j, i: (rows(j, i), jnp.maximum(j - 1, 0))),
                   pl.BlockSpec((dc, D_MODEL), lambda j, i: (down(j, i), 0))],
        out_shape=[jax.ShapeDtypeStruct((N_ROWS, D_FF), BF16),
                   jax.ShapeDtypeStruct((D_FF, D_MODEL), BF16)],
        scratch_shapes=[pltpu.VMEM((2, D_MODEL, 2 * tf), BF16)],
        compiler_params=pltpu.CompilerParams(
            dimension_semantics=("arbitrary", "arbitrary"),
            vmem_limit_bytes=_vmem_limit(2 * tm * D_MODEL * 2 + 4 * kc * tf * 4 + 2 * dc * D_MODEL * 6
                                         + 4 * D_MODEL * tf * 2 + 2 * tm * tf * 2 + 3 * tm * tf * 4)),
        name="ffn_up",
    )(h, w_in, w_in, w_out)


def _deepnorm_kernel(*refs, tm, rows, coef, cast_w, split_lhs, split_x, split_out, emit_h):
    refs = list(refs)
    take = lambda n: [refs.pop(0) for _ in range(n)]
    lhs_refs = take(2 if split_lhs else 1)
    (w_ref,) = take(1)
    x_refs = take(2 if split_x else 1)
    gate_ref, lng_ref, lnb_ref = take(3)
    sc_ref, sh_ref = take(2) if emit_h else (None, None)
    xo_refs = take(2 if split_out else 1)
    (ho_ref,) = take(1) if emit_h else (None,)
    (wbf_ref,) = take(1) if cast_w else (w_ref,)

    if cast_w:
        @pl.when(pl.program_id(0) == 0)
        def _():
            wbf_ref[...] = w_ref[...].astype(BF16)

    row0 = pl.program_id(0) * tm
    gate = coef * _mod_row(gate_ref, row0)
    if emit_h:
        sc = 1.0 + _mod_row(sc_ref, row0)
        sh = _mod_row(sh_ref, row0)

    def body(lhs_ref, x_ref, xo_ref):
        for r in range(tm // rows):
            rs = slice(r * rows, (r + 1) * rows)
            y = jnp.dot(lhs_ref[rs, :], wbf_ref[...], preferred_element_type=F32)
            z = DEEPNORM_ALPHA * x_ref[rs, :] + gate * y
            mu = jnp.mean(z, axis=-1, keepdims=True)
            zc = z - mu
            var = jnp.mean(zc * zc, axis=-1, keepdims=True)
            o = zc * lax.rsqrt(var + EPS) * lng_ref[...] + lnb_ref[...]
            xo_ref[rs, :] = o
            if emit_h:
                ho_ref[rs, :] = (o * sc + sh).astype(BF16)

    if split_lhs or split_x or split_out:
        pl.when(row0 < N_CTX)(lambda: body(lhs_refs[0], x_refs[0], xo_refs[0]))
        pl.when(row0 >= N_CTX)(lambda: body(lhs_refs[-1], x_refs[-1], xo_refs[-1]))
    else:
        body(lhs_refs[0], x_refs[0], xo_refs[0])


def _proj_deepnorm(lhs, w, w_index, x, mod, ln_g, ln_b, layer, sub, coef, next_mod, split_out=False):
    split_lhs, split_x = isinstance(lhs, tuple), isinstance(x, tuple)
    k = (lhs[0] if split_lhs else lhs).shape[1]
    tm = rows = 256
    cast_w = w.dtype != BF16
    emit_h = next_mod is not None
    lead = (None,) * len(w_index)
    row_spec = lambda width: pl.BlockSpec((tm, width), lambda i: (i, 0))
    in_specs, args = [], []
    if split_lhs:
        in_specs += list(_split_specs(tm, k, 0)); args += list(lhs)
    else:
        in_specs.append(row_spec(k)); args.append(lhs)
    in_specs.append(pl.BlockSpec(lead + (k, D_MODEL), lambda i: w_index + (0, 0), pipeline_mode=pl.Buffered(1)))
    args.append(w)
    if split_x:
        in_specs += list(_split_specs(tm, D_MODEL, 0)); args += list(x)
    else:
        in_specs.append(row_spec(D_MODEL)); args.append(x)
    ln_spec = pl.BlockSpec((None, 1, D_MODEL), lambda i: (layer * N_SUB + sub, 0, 0))
    in_specs += [_mod_spec(layer, sub, 2, 1), ln_spec, ln_spec]
    args += [mod, ln_g, ln_b]
    if emit_h:
        in_specs += [_mod_spec(next_mod[0], next_mod[1], 1, 1), _mod_spec(next_mod[0], next_mod[1], 0, 1)]
        args += [mod, mod]
    if split_out:
        out_specs = list(_split_specs(tm, D_MODEL, 0))
        out_shape = [jax.ShapeDtypeStruct((N_CTX, D_MODEL), F32), jax.ShapeDtypeStruct((N_LAT, D_MODEL), F32)]
    else:
        out_specs = [row_spec(D_MODEL)]
        out_shape = [jax.ShapeDtypeStruct((N_ROWS, D_MODEL), F32)]
    if emit_h:
        out_specs.append(row_spec(D_MODEL))
        out_shape.append(jax.ShapeDtypeStruct((N_ROWS, D_MODEL), BF16))
    w_bytes = k * D_MODEL * (6 if cast_w else 2)
    est = (2 * (1 + split_lhs) * tm * k * 2 + w_bytes + 2 * (2 + split_x + split_out) * tm * D_MODEL * 4
           + 2 * tm * D_MODEL * 2 + 4 * tm * D_MODEL * 4)
    outs = pl.pallas_call(
        functools.partial(_deepnorm_kernel, tm=tm, rows=rows, coef=coef, cast_w=cast_w, split_lhs=split_lhs,
                          split_x=split_x, split_out=split_out, emit_h=emit_h),
        grid=(N_ROWS // tm,),
        in_specs=in_specs,
        out_specs=out_specs,
        out_shape=out_shape,
        scratch_shapes=[pltpu.VMEM((k, D_MODEL), BF16)] if cast_w else [],
        compiler_params=pltpu.CompilerParams(
            dimension_semantics=("arbitrary",), vmem_limit_bytes=_vmem_limit(est)),
        name="proj_deepnorm",
    )(*args)
    x_new = (outs[0], outs[1]) if split_out else outs[0]
    return x_new, (outs[-1] if emit_h else None)


def _rope_tables(dim):
    rows = DEC_SEQ // GRID_W
    row = np.repeat(np.arange(rows, dtype=np.float64), GRID_W)
    col = np.tile(np.arange(GRID_W, dtype=np.float64), rows)
    quarter = dim // 4
    inv = ROPE_THETA ** (-np.arange(quarter, dtype=np.float64) / quarter)
    ang = np.concatenate([row[:, None] * inv, col[:, None] * inv], axis=-1)
    cos, sin, zero = np.cos(ang), np.sin(ang), np.zeros_like(ang)
    reps = LANES // dim
    full = lambda a, b: np.tile(np.concatenate([a, b], axis=-1), (1, reps)).astype(np.float32)
    return jnp.asarray(full(cos, cos)), jnp.asarray(full(-sin, zero)), jnp.asarray(full(zero, sin))


DENSE = -1


def _qkv_kernel(*refs, tm, tn, norm, rope_dim, cache_split, out_scale):
    refs = list(refs)
    take = lambda n: [refs.pop(0) for _ in range(n)]
    h_ref, w_ref, gain_ref = take(3)
    cos_ref, slo_ref, shi_ref = take(3) if rope_dim else (None, None, None)
    (o_ref,) = take(1)
    (cache_ref,) = take(1) if cache_split else (None,)
    (wbf_ref,) = take(1)

    @pl.when(pl.program_id(1) == 0)
    def _():
        wbf_ref[...] = w_ref[...].astype(BF16)

    def finish(rope, write_cache):
        for r in range(tm // SEQ):
            rs = slice(r * SEQ, (r + 1) * SEQ)
            y = jnp.dot(h_ref[rs, :], wbf_ref[...], preferred_element_type=F32)
            for c in range(tn // LANES):
                yc = y[:, c * LANES:(c + 1) * LANES]
                if norm:
                    yc = yc * lax.rsqrt(jnp.mean(yc * yc, axis=-1, keepdims=True) + EPS) * gain_ref[...]
                if rope:
                    half = rope_dim // 2
                    yc = (yc * cos_ref[rs, :]
                          + pltpu.roll(yc, LANES - half, 1) * slo_ref[rs, :]
                          + pltpu.roll(yc, half, 1) * shi_ref[rs, :])
                if out_scale != 1.0:
                    yc = yc * out_scale
                o_ref[rs, c * LANES:(c + 1) * LANES] = yc.astype(o_ref.dtype)
                if write_cache and cache_split == DENSE:
                    cache_ref[rs, c * LANES:(c + 1) * LANES] = yc
                elif write_cache:
                    width = LANES // cache_split
                    for p in range(cache_split):
                        piece = yc[:, p * width:(p + 1) * width]
                        if cache_split == 1:
                            cache_ref[r, :, c, :] = piece
                        else:
                            cache_ref[r, :, c, p, :] = piece

    if rope_dim or cache_split:
        is_latent = pl.program_id(1) * tm >= N_CTX
        pl.when(jnp.logical_not(is_latent))(lambda: finish(False, bool(cache_split)))
        pl.when(is_latent)(lambda: finish(bool(rope_dim), False))
    else:
        finish(False, False)


def _qkv_proj(h, w, layer_j, col0, width, gain, tables, norm, rope_dim, out_scale=1.0, cache_split=0):
    tm, tn = 1024, 512
    n0 = col0 // tn
    in_specs = [
        pl.BlockSpec((tm, D_MODEL), lambda j, i: (i, 0)),
        pl.BlockSpec((None, D_MODEL, tn), lambda j, i: (layer_j, 0, n0 + j)),
        pl.BlockSpec((1, LANES), lambda j, i: (0, 0)),
    ]
    args = [h, w, gain]
    if rope_dim:
        def tab_map(j, i):
            r = i * tm
            return (jnp.where(r >= N_CTX, ((r - N_CTX) % DEC_SEQ) // tm, 0), 0)
        in_specs += [pl.BlockSpec((tm, LANES), tab_map)] * 3
        args += list(tables)
    out_specs = [pl.BlockSpec((tm, tn), lambda j, i: (i, j))]
    out_shape = [jax.ShapeDtypeStruct((N_ROWS, width), BF16)]
    n_ctx = N_CTX // tm
    if cache_split == DENSE:
        out_specs.append(pl.BlockSpec((tm, tn), lambda j, i: (jnp.minimum(i, n_ctx - 1), j)))
        out_shape.append(jax.ShapeDtypeStruct((N_CTX, width), F32))
    elif cache_split:
        tail = (LANES,) if cache_split == 1 else (cache_split, LANES // cache_split)
        zeros = (0,) * len(tail)
        out_specs.append(pl.BlockSpec((tm // SEQ, None, SEQ, tn // LANES) + tail,
                                      lambda j, i: (jnp.minimum(i, n_ctx - 1), 0, 0, j) + zeros))
        out_shape.append(jax.ShapeDtypeStruct((BATCH, 1, SEQ, width // LANES) + tail, F32))
    outs = pl.pallas_call(
        functools.partial(_qkv_kernel, tm=tm, tn=tn, norm=norm, rope_dim=rope_dim,
                          cache_split=cache_split, out_scale=out_scale),
        grid=(width // tn, N_ROWS // tm),
        in_specs=in_specs,
        out_specs=out_specs,
        out_shape=out_shape,
        scratch_shapes=[pltpu.VMEM((D_MODEL, tn), BF16)],
        compiler_params=pltpu.CompilerParams(
            dimension_semantics=("arbitrary", "arbitrary"),
            vmem_limit_bytes=_vmem_limit(2 * tm * D_MODEL * 2 + 2 * D_MODEL * tn * 4 + D_MODEL * tn * 2
                                         + 2 * tm * tn * 2 + 2 * abs(cache_split) * tm * tn * 4
                                         + 6 * tm * LANES * 4 + 3 * tm * tn * 4)),
        name="qkv_proj",
    )(*args)
    return (outs[0], outs[1]) if cache_split else outs[0]


_NT = (((1,), (1,)), ((), ()))


LOG2_E = math.log2(math.e)


def _with_ones(v):
    v = v.astype(BF16)
    return jnp.concatenate([v, jnp.ones_like(v)], axis=1)


def _softmax_pv(q, keys, vals1):
    logits = [lax.dot_general(q, k, _NT, preferred_element_type=F32) for k in keys]
    m = functools.reduce(jnp.maximum, [jnp.max(s, axis=-1, keepdims=True) for s in logits])
    acc = None
    for s, v1 in zip(logits, vals1):
        pv = jnp.dot(jnp.exp2(s - m).astype(BF16), v1, preferred_element_type=F32)
        acc = pv if acc is None else acc + pv
    return acc[:, :LANES] / acc[:, LANES:]


def _attn_a_kernel(*refs, kv_heads, has_cache):
    if has_cache:
        q_ref, kc_ref, vc_ref, kn_ref, vn_ref, o_ref = refs
    else:
        q_ref, kn_ref, vn_ref, o_ref = refs
    for kv in range(kv_heads):
        ks = slice(kv * LANES, (kv + 1) * LANES)
        keys = [kn_ref[:, ks].astype(BF16)]
        vals1 = [_with_ones(vn_ref[:, ks])]
        if has_cache:
            keys.append(kc_ref[:, ks].astype(BF16))
            vals1.append(_with_ones(vc_ref[:, ks]))
        for g in range(A_REP):
            sl = slice((kv * A_REP + g) * LANES, (kv * A_REP + g + 1) * LANES)
            o_ref[:, sl] = _softmax_pv(q_ref[:, sl], keys, vals1).astype(o_ref.dtype)


def _attn_a(q, k, v, cache_k, cache_v):
    tq = SEQ
    kvs = A_KV_HEADS
    kw, gw = kvs * LANES, kvs * A_REP * LANES
    o_ctx = pl.pallas_call(
        functools.partial(_attn_a_kernel, kv_heads=kvs, has_cache=False),
        grid=(BATCH, A_KV_HEADS // kvs),
        in_specs=[
            pl.BlockSpec((tq, gw), lambda b, h: (b, h)),
            pl.BlockSpec((SEQ, kw), lambda b, h: (b, h)),
            pl.BlockSpec((SEQ, kw), lambda b, h: (b, h)),
        ],
        out_specs=pl.BlockSpec((tq, gw), lambda b, h: (b, h)),
        out_shape=jax.ShapeDtypeStruct((N_CTX, D_MODEL), BF16),
        compiler_params=pltpu.CompilerParams(
            dimension_semantics=("arbitrary", "arbitrary"),
            vmem_limit_bytes=_vmem_limit(4 * tq * gw * 2 + 4 * SEQ * kw * 2 + 4 * kvs * A_REP * tq * SEQ * 4)),
        name="attn_a_ctx",
    )(q, k, v)
    tq = 512
    kvs = 2
    kw, gw = kvs * LANES, kvs * A_REP * LANES
    nq = DEC_SEQ // tq
    row0 = N_CTX // tq
    t_all = PAST_LEN + DEC_SEQ
    o_lat = pl.pallas_call(
        functools.partial(_attn_a_kernel, kv_heads=kvs, has_cache=True),
        grid=(DEC_BATCH, A_KV_HEADS // kvs, nq),
        in_specs=[
            pl.BlockSpec((tq, gw), lambda b, h, i: (row0 + b * nq + i, h)),
            pl.BlockSpec((None, PAST_LEN, kw), lambda b, h, i: (b, 0, h)),
            pl.BlockSpec((None, PAST_LEN, kw), lambda b, h, i: (b, 0, h)),
            pl.BlockSpec((DEC_SEQ, kw), lambda b, h, i: (N_CTX // DEC_SEQ + b, h)),
            pl.BlockSpec((DEC_SEQ, kw), lambda b, h, i: (N_CTX // DEC_SEQ + b, h)),
        ],
        out_specs=pl.BlockSpec((tq, gw), lambda b, h, i: (b * nq + i, h)),
        out_shape=jax.ShapeDtypeStruct((N_LAT, D_MODEL), BF16),
        compiler_params=pltpu.CompilerParams(
            dimension_semantics=("arbitrary", "arbitrary", "arbitrary"),
            vmem_limit_bytes=_vmem_limit(4 * tq * gw * 2 + 4 * t_all * kw * 4
                                         + 2 * kvs * A_REP * tq * t_all * 4)),
        name="attn_a_lat",
    )(q, cache_k, cache_v, k, v)
    return o_ctx, o_lat


def _attn_b_kernel(*refs, heads, has_cache, lam_init):
    if has_cache:
        q_ref, kc_ref, vc_ref, kn_ref, vn_ref, lam_ref, sub_ref, o_ref = refs
    else:
        q_ref, kn_ref, vn_ref, lam_ref, sub_ref, o_ref = refs
    lp = lam_ref[...]
    lam = (jnp.exp(jnp.sum(lp[0:1] * lp[1:2], axis=-1, keepdims=True))
           - jnp.exp(jnp.sum(lp[2:3] * lp[3:4], axis=-1, keepdims=True)) + lam_init)
    first = lax.broadcasted_iota(jnp.int32, (1, LANES), 1) < B_QK_DIM
    for h in range(heads):
        sl = slice(h * LANES, (h + 1) * LANES)
        q = q_ref[:, sl]
        zero = jnp.zeros_like(q)
        keys = [kn_ref[:, sl].astype(BF16)]
        vals1 = [_with_ones(vn_ref[:, sl])]
        if has_cache:
            keys.append(kc_ref[:, sl].astype(BF16))
            vals1.append(_with_ones(vc_ref[:, sl]))
        o1 = _softmax_pv(jnp.where(first, q, zero), keys, vals1)
        o2 = _softmax_pv(jnp.where(first, zero, q), keys, vals1)
        o = o1 - lam * o2
        o = o * lax.rsqrt(jnp.mean(o * o, axis=-1, keepdims=True) + EPS) * sub_ref[...] * (1.0 - lam_init)
        o_ref[:, sl] = o.astype(o_ref.dtype)


def _attn_b(q, k, v, cache_k, cache_v, lam_p, subln, lam_init):
    heads = 8
    gw = heads * LANES
    tq = SEQ
    o_ctx = pl.pallas_call(
        functools.partial(_attn_b_kernel, heads=heads, has_cache=False, lam_init=lam_init),
        grid=(BATCH, B_HEADS // heads),
        in_specs=[
            pl.BlockSpec((tq, gw), lambda b, h: (b, h)),
            pl.BlockSpec((SEQ, gw), lambda b, h: (b, h)),
            pl.BlockSpec((SEQ, gw), lambda b, h: (b, h)),
            pl.BlockSpec((4, B_QK_DIM), lambda b, h: (0, 0)),
            pl.BlockSpec((1, LANES), lambda b, h: (0, 0)),
        ],
        out_specs=pl.BlockSpec((tq, gw), lambda b, h: (b, h)),
        out_shape=jax.ShapeDtypeStruct((N_CTX, D_MODEL), BF16),
        compiler_params=pltpu.CompilerParams(
            dimension_semantics=("arbitrary", "arbitrary"),
            vmem_limit_bytes=_vmem_limit(4 * tq * gw * 2 + 4 * SEQ * gw * 4 + 8 * heads * tq * SEQ * 4)),
        name="attn_b_ctx",
    )(q, k, v, lam_p, subln)
    tq = 512
    nq = DEC_SEQ // tq
    row0 = N_CTX // tq
    t_all = PAST_LEN + DEC_SEQ
    heads = 4
    gw = heads * LANES
    o_lat = pl.pallas_call(
        functools.partial(_attn_b_kernel, heads=heads, has_cache=True, lam_init=lam_init),
        grid=(DEC_BATCH, B_HEADS // heads, nq),
        in_specs=[
            pl.BlockSpec((tq, gw), lambda b, h, i: (row0 + b * nq + i, h)),
            pl.BlockSpec((None, PAST_LEN, gw), lambda b, h, i: (b, 0, h)),
            pl.BlockSpec((None, PAST_LEN, gw), lambda b, h, i: (b, 0, h)),
            pl.BlockSpec((DEC_SEQ, gw), lambda b, h, i: (N_CTX // DEC_SEQ + b, h)),
            pl.BlockSpec((DEC_SEQ, gw), lambda b, h, i: (N_CTX // DEC_SEQ + b, h)),
            pl.BlockSpec((4, B_QK_DIM), lambda b, h, i: (0, 0)),
            pl.BlockSpec((1, LANES), lambda b, h, i: (0, 0)),
        ],
        out_specs=pl.BlockSpec((tq, gw), lambda b, h, i: (b * nq + i, h)),
        out_shape=jax.ShapeDtypeStruct((N_LAT, D_MODEL), BF16),
        compiler_params=pltpu.CompilerParams(
            dimension_semantics=("arbitrary", "arbitrary", "arbitrary"),
            vmem_limit_bytes=_vmem_limit(4 * tq * gw * 2 + 4 * t_all * gw * 4
                                         + 3 * 2 * heads * tq * t_all * 4)),
        name="attn_b_lat",
    )(q, cache_k, cache_v, k, v, lam_p, subln)
    return o_ctx, o_lat


def _diff_lambda_init(layer_idx):
    return 0.8 - 0.6 * math.exp(-0.3 * layer_idx)


def kernel(x_prompt, x_sample, cache_a_k, cache_a_v, cache_b_k, cache_b_v, c, c_ctx, ada_w, ada_b, ln_g, ln_b, ffn_w_in, ffn_w_out, a_w_qkv, a_q_norm, a_k_norm, a_w_o, b_w_qkv, b_lambda, b_subln, b_w_o):
    ln_g3 = ln_g.reshape(DEPTH * N_SUB, 1, D_MODEL)
    ln_b3 = ln_b.reshape(DEPTH * N_SUB, 1, D_MODEL)
    tables_a = _rope_tables(A_HEAD_DIM)
    tables_b = _rope_tables(B_QK_DIM)
    ones = jnp.ones((1, LANES), F32)

    cond = jnp.concatenate([c_ctx[None, :], c, jnp.zeros((COND_ROWS - N_GROUPS, D_MODEL), F32)], axis=0)
    mod = _modulation(cond, ada_w, ada_b)

    x = (x_prompt.reshape(N_CTX, D_MODEL), x_sample.reshape(N_LAT, D_MODEL))
    h = _premod(x[0], x[1], mod, 0, 0)

    new_kv = {"ak": [], "av": [], "bk": [], "bv": []}
    for i in range(DEPTH):
        j = i // N_MIXERS
        last = i + 1 == DEPTH
        a, w_down = _ffn_up(h, ffn_w_in, ffn_w_out, i, 0)
        x, h = _proj_deepnorm(a, w_down, (), x, mod, ln_g3, ln_b3, i, 0, MACARON_WEIGHT, (i, 1))
        if i % N_MIXERS == 0:
            qw = A_HEADS * A_HEAD_DIM
            kw = A_KV_HEADS * A_HEAD_DIM
            q = _qkv_proj(h, a_w_qkv, j, 0, qw, a_q_norm[j][None, :], tables_a, True, A_HEAD_DIM,
                          out_scale=A_HEAD_DIM ** -0.5 * LOG2_E)
            k, new_k = _qkv_proj(h, a_w_qkv, j, qw, kw, a_k_norm[j][None, :], tables_a, True, A_HEAD_DIM,
                                 cache_split=1)
            v, new_v = _qkv_proj(h, a_w_qkv, j, qw + kw, kw, ones, None, False, 0, cache_split=1)
            o = _attn_a(q, k, v, cache_a_k[:, j].reshape(DEC_BATCH, PAST_LEN, kw),
                        cache_a_v[:, j].reshape(DEC_BATCH, PAST_LEN, kw))
            new_kv["ak"].append(new_k)
            new_kv["av"].append(new_v)
            w_o = a_w_o
        else:
            lam_init = _diff_lambda_init(i)
            qw = B_HEADS * 2 * B_QK_DIM
            q = _qkv_proj(h, b_w_qkv, j, 0, qw, ones, tables_b, False, B_QK_DIM,
                          out_scale=B_QK_DIM ** -0.5 * LOG2_E)
            k, new_k = _qkv_proj(h, b_w_qkv, j, qw, qw, ones, tables_b, False, B_QK_DIM, cache_split=2)
            v, new_v = _qkv_proj(h, b_w_qkv, j, 2 * qw, B_HEADS * B_V_DIM, ones, None, False, 0,
                                 cache_split=DENSE)
            new_v = new_v.reshape(BATCH, 1, SEQ, B_HEADS, B_V_DIM)
            o = _attn_b(q, k, v, cache_b_k[:, j].reshape(DEC_BATCH, PAST_LEN, qw),
                        cache_b_v[:, j].reshape(DEC_BATCH, PAST_LEN, B_HEADS * B_V_DIM),
                        b_lambda[j], b_subln[j][None, :], lam_init)
            new_kv["bk"].append(new_k)
            new_kv["bv"].append(new_v)
            w_o = b_w_o
        x, h = _proj_deepnorm(o, w_o, (j,), x, mod, ln_g3, ln_b3, i, 1, 1.0, (i, 2))
        a, w_down = _ffn_up(h, ffn_w_in, ffn_w_out, i, 1)
        x, h = _proj_deepnorm(a, w_down, (), x, mod, ln_g3, ln_b3, i, 2, MACARON_WEIGHT,
                              None if last else (i + 1, 0), split_out=last)

    y_prompt = x[0].reshape(BATCH, SEQ, D_MODEL)
    y_sample = x[1].reshape(DEC_BATCH, DEC_SEQ, D_MODEL)
    return (y_prompt, y_sample,
            jnp.concatenate(new_kv["ak"], axis=1), jnp.concatenate(new_kv["av"], axis=1),
            jnp.concatenate(new_kv["bk"], axis=1), jnp.concatenate(new_kv["bv"], axis=1))
```

```python
import functools
import math

import jax
import jax.numpy as jnp
import numpy as np
from jax import lax
from jax.experimental import pallas as pl
from jax.experimental.pallas import tpu as pltpu

D_MODEL = 2048
BATCH = 16
SEQ = 256
DEPTH = 2
DEC_BATCH = 2
DEC_SEQ = 2048
PAST_LEN = 256
GRID_W = 64
N_MIXERS = 2
N_SUB = 3
A_HEAD_DIM = 128
A_HEADS = D_MODEL // A_HEAD_DIM
A_KV_HEADS = 4
A_REP = A_HEADS // A_KV_HEADS
B_QK_DIM = 64
B_V_DIM = 2 * B_QK_DIM
B_HEADS = D_MODEL // B_V_DIM
D_FF = ((8 * D_MODEL // 3 + 127) // 128) * 128
ROPE_THETA = 10000.0
EPS = 1e-6
MACARON_WEIGHT = 0.5
DEEPNORM_ALPHA = (2 * DEPTH) ** 0.25

N_CTX = BATCH * SEQ
N_LAT = DEC_BATCH * DEC_SEQ
N_ROWS = N_CTX + N_LAT
N_GROUPS = 1 + DEC_BATCH
COND_ROWS = 8
LANES = 128
V7X_VMEM_BYTES = 64 * 1024 * 1024

F32 = jnp.float32
BF16 = jnp.bfloat16


def _vmem_limit(nbytes):
    return int(min(nbytes * 3 // 2 + (2 << 20), V7X_VMEM_BYTES - (4 << 20)))


def _group_of_row(r):
    return jnp.where(r < N_CTX, 0, 1 + (r - N_CTX) // DEC_SEQ)


def _mod_spec(layer, sub, kind, grid_rank):
    col = sub * 3 + kind
    return pl.BlockSpec((None, COND_ROWS, D_MODEL), lambda *idx: (layer, 0, col))


def _mod_row(ref, row0):
    return ref[pl.ds(_group_of_row(row0), 1), :]


def _mod_kernel(c_ref, w_ref, b_ref, o_ref):
    s = jax.nn.silu(c_ref[...]).astype(BF16)
    o_ref[...] = jnp.dot(s, w_ref[...].astype(BF16), preferred_element_type=F32) + b_ref[...]


def _modulation(cond, ada_w, ada_b):
    tn = 1024
    n_out = N_SUB * 3 * D_MODEL
    return pl.pallas_call(
        _mod_kernel,
        grid=(DEPTH, n_out // tn),
        in_specs=[
            pl.BlockSpec((COND_ROWS, D_MODEL), lambda l, j: (0, 0)),
            pl.BlockSpec((None, D_MODEL, tn), lambda l, j: (l, 0, j)),
            pl.BlockSpec((None, 1, tn), lambda l, j: (l, 0, j)),
        ],
        out_specs=pl.BlockSpec((None, COND_ROWS, tn), lambda l, j: (l, 0, j)),
        out_shape=jax.ShapeDtypeStruct((DEPTH, COND_ROWS, n_out), F32),
        compiler_params=pltpu.CompilerParams(
            dimension_semantics=("arbitrary", "arbitrary"),
            vmem_limit_bytes=_vmem_limit(2 * D_MODEL * tn * 4 + D_MODEL * tn * 2)),
        name="modulation",
    )(cond, ada_w, ada_b.reshape(DEPTH, 1, n_out))


def _split_specs(tm, width, row_axis, shift=0):
    n_ctx, n_lat = N_CTX // tm, N_LAT // tm

    def ctx_map(*idx):
        return (jnp.clip(idx[row_axis] - shift, 0, n_ctx - 1), 0)

    def lat_map(*idx):
        return (jnp.clip(idx[row_axis] - shift - n_ctx, 0, n_lat - 1), 0)

    return pl.BlockSpec((tm, width), ctx_map), pl.BlockSpec((tm, width), lat_map)


def _premod_kernel(xc_ref, xl_ref, sc_ref, sh_ref, h_ref, *, tm):
    row0 = pl.program_id(0) * tm
    sc = 1.0 + _mod_row(sc_ref, row0)
    sh = _mod_row(sh_ref, row0)

    def emit(x_ref):
        h_ref[...] = (x_ref[...] * sc + sh).astype(BF16)

    pl.when(row0 < N_CTX)(lambda: emit(xc_ref))
    pl.when(row0 >= N_CTX)(lambda: emit(xl_ref))


def _premod(x_ctx, x_lat, mod, layer, sub):
    tm = 512
    return pl.pallas_call(
        functools.partial(_premod_kernel, tm=tm),
        grid=(N_ROWS // tm,),
        in_specs=[*_split_specs(tm, D_MODEL, 0),
                  _mod_spec(layer, sub, 1, 1), _mod_spec(layer, sub, 0, 1)],
        out_specs=pl.BlockSpec((tm, D_MODEL), lambda i: (i, 0)),
        out_shape=jax.ShapeDtypeStruct((N_ROWS, D_MODEL), BF16),
        compiler_params=pltpu.CompilerParams(
            dimension_semantics=("arbitrary",),
            vmem_limit_bytes=_vmem_limit(4 * tm * D_MODEL * 4 + 2 * tm * D_MODEL * 2)),
        name="premod",
    )(x_ctx, x_lat, mod, mod)


def _up_kernel(h_ref, wg_ref, wu_ref, wo_ref, a_ref, wob_ref, wbf_ref, *, tf, ts, nf, kc, tail):
    j, i = pl.program_id(0), pl.program_id(1)
    subs = [(slice(s * ts, (s + 1) * ts), slice(2 * s * ts, 2 * (s + 1) * ts)) for s in range(tf // ts)]

    def stage():
        rows = pl.ds(pl.multiple_of(i * kc, kc), kc)
        for cols, packed in subs:
            wbf_ref[j % 2, rows, packed.start:packed.start + ts] = wg_ref[:, cols].astype(BF16)
            wbf_ref[j % 2, rows, packed.start + ts:packed.stop] = wu_ref[:, cols].astype(BF16)
        wob_ref[...] = wo_ref[...].astype(BF16)

    def multiply(first_sub=0):
        for cols, packed in subs[first_sub:]:
            gu = jnp.dot(h_ref[...], wbf_ref[(j + 1) % 2, :, packed], preferred_element_type=F32)
            a_ref[:, cols] = (jax.nn.silu(gu[:, :ts]) * gu[:, ts:]).astype(BF16)

    @pl.when(j == 0)
    def _():
        stage()

    @pl.when(jnp.logical_and(j > 0, j < nf))
    def _():
        stage()
        multiply()

    @pl.when(j == nf)
    def _():
        multiply(first_sub=(tf - tail) // ts)
        a_ref[:, :tail] = a_ref[:, tf - tail:]


def _ffn_up(h, w_in, w_out, layer, half):
    tm, tf, ts = 1024, 1024, 512
    nm = N_ROWS // tm
    nf = pl.cdiv(D_FF, tf)
    tail = D_FF - (nf - 1) * tf
    kc = D_MODEL // nm
    dc = LANES
    n_down = D_FF // dc
    assert D_FF % dc == 0 and n_down <= nf * nm
    staged = lambda j: jnp.minimum(j, nf - 1)
    krow = lambda j, i: pl.multiple_of(jnp.where(j < nf, i, nm - 1) * kc, kc)
    start = lambda j: pl.multiple_of(jnp.minimum(staged(j) * tf, D_FF - tf), LANES)
    ustart = lambda j: pl.multiple_of(D_FF + jnp.minimum(staged(j) * tf, D_FF - tf), LANES)
    down = lambda j, i: jnp.minimum(j * nm + i, n_down - 1)
    rows = lambda j, i: jnp.where(j > 0, i, 0)
    elem = pl.Element
    return pl.pallas_call(
        functools.partial(_up_kernel, tf=tf, ts=ts, nf=nf, kc=kc, tail=tail),
        grid=(nf + 1, nm),
        in_specs=[
            pl.BlockSpec((tm, D_MODEL), lambda j, i: (rows(j, i), 0)),
            pl.BlockSpec((None, None, elem(kc), elem(tf)), lambda j, i: (layer, half, krow(j, i), start(j))),
            pl.BlockSpec((None, None, elem(kc), elem(tf)), lambda j, i: (layer, half, krow(j, i), ustart(j))),
            pl.BlockSpec((None, None, dc, D_MODEL), lambda j, i: (layer, half, down(j, i), 0)),
        ],
        out_specs=[pl.BlockSpec((tm, tf), lambda j, i: (rows(j, i), jnp.maximum(j - 1, 0))),
                   pl.BlockSpec((dc, D_MODEL), lambda j, i: (down(j, i), 0))],
        out_shape=[jax.ShapeDtypeStruct((N_ROWS, D_FF), BF16),
                   jax.ShapeDtypeStruct((D_FF, D_MODEL), BF16)],
        scratch_shapes=[pltpu.VMEM((2, D_MODEL, 2 * tf), BF16)],
        compiler_params=pltpu.CompilerParams(
            dimension_semantics=("arbitrary", "arbitrary"),
            vmem_limit_bytes=_vmem_limit(2 * tm * D_MODEL * 2 + 4 * kc * tf * 4 + 2 * dc * D_MODEL * 6
                                         + 4 * D_MODEL * tf * 2 + 2 * tm * tf * 2 + 3 * tm * tf * 4)),
        name="ffn_up",
    )(h, w_in, w_in, w_out)


def _deepnorm_kernel(*refs, tm, coef, cast_w, split_lhs, split_x, split_out, emit_h):
    refs = list(refs)
    take = lambda n: [refs.pop(0) for _ in range(n)]
    lhs_refs = take(2 if split_lhs else 1)
    (w_ref,) = take(1)
    x_refs = take(2 if split_x else 1)
    gate_ref, lng_ref, lnb_ref = take(3)
    sc_ref, sh_ref = take(2) if emit_h else (None, None)
    xo_refs = take(2 if split_out else 1)
    (ho_ref,) = take(1) if emit_h else (None,)
    (wbf_ref,) = take(1) if cast_w else (w_ref,)

    if cast_w:
        @pl.when(pl.program_id(0) == 0)
        def _():
            wbf_ref[...] = w_ref[...].astype(BF16)

    row0 = pl.program_id(0) * tm
    gate = coef * _mod_row(gate_ref, row0)
    if emit_h:
        sc = 1.0 + _mod_row(sc_ref, row0)
        sh = _mod_row(sh_ref, row0)

    def body(lhs_ref, x_ref, xo_ref):
        y = jnp.dot(lhs_ref[...], wbf_ref[...], preferred_element_type=F32)
        z = DEEPNORM_ALPHA * x_ref[...] + gate * y
        mu = jnp.mean(z, axis=-1, keepdims=True)
        zc = z - mu
        var = jnp.mean(zc * zc, axis=-1, keepdims=True)
        o = zc * lax.rsqrt(var + EPS) * lng_ref[...] + lnb_ref[...]
        xo_ref[...] = o
        if emit_h:
            ho_ref[...] = (o * sc + sh).astype(BF16)

    if split_lhs or split_x or split_out:
        pl.when(row0 < N_CTX)(lambda: body(lhs_refs[0], x_refs[0], xo_refs[0]))
        pl.when(row0 >= N_CTX)(lambda: body(lhs_refs[-1], x_refs[-1], xo_refs[-1]))
    else:
        body(lhs_refs[0], x_refs[0], xo_refs[0])


def _proj_deepnorm(lhs, w, w_index, x, mod, ln_g, ln_b, layer, sub, coef, next_mod, split_out=False):
    split_lhs, split_x = isinstance(lhs, tuple), isinstance(x, tuple)
    k = (lhs[0] if split_lhs else lhs).shape[1]
    tm = 256
    cast_w = w.dtype != BF16
    emit_h = next_mod is not None
    lead = (None,) * len(w_index)
    row_spec = lambda width: pl.BlockSpec((tm, width), lambda i: (i, 0))
    in_specs, args = [], []
    if split_lhs:
        in_specs += list(_split_specs(tm, k, 0)); args += list(lhs)
    else:
        in_specs.append(row_spec(k)); args.append(lhs)
    in_specs.append(pl.BlockSpec(lead + (k, D_MODEL), lambda i: w_index + (0, 0), pipeline_mode=pl.Buffered(1)))
    args.append(w)
    if split_x:
        in_specs += list(_split_specs(tm, D_MODEL, 0)); args += list(x)
    else:
        in_specs.append(row_spec(D_MODEL)); args.append(x)
    ln_spec = pl.BlockSpec((None, 1, D_MODEL), lambda i: (layer * N_SUB + sub, 0, 0))
    in_specs += [_mod_spec(layer, sub, 2, 1), ln_spec, ln_spec]
    args += [mod, ln_g, ln_b]
    if emit_h:
        in_specs += [_mod_spec(next_mod[0], next_mod[1], 1, 1), _mod_spec(next_mod[0], next_mod[1], 0, 1)]
        args += [mod, mod]
    if split_out:
        out_specs = list(_split_specs(tm, D_MODEL, 0))
        out_shape = [jax.ShapeDtypeStruct((N_CTX, D_MODEL), F32), jax.ShapeDtypeStruct((N_LAT, D_MODEL), F32)]
    else:
        out_specs = [row_spec(D_MODEL)]
        out_shape = [jax.ShapeDtypeStruct((N_ROWS, D_MODEL), F32)]
    if emit_h:
        out_specs.append(row_spec(D_MODEL))
        out_shape.append(jax.ShapeDtypeStruct((N_ROWS, D_MODEL), BF16))
    w_bytes = k * D_MODEL * (6 if cast_w else 2)
    est = (2 * (1 + split_lhs) * tm * k * 2 + w_bytes + 2 * (2 + split_x + split_out) * tm * D_MODEL * 4
           + 2 * tm * D_MODEL * 2 + 4 * tm * D_MODEL * 4)
    outs = pl.pallas_call(
        functools.partial(_deepnorm_kernel, tm=tm, coef=coef, cast_w=cast_w, split_lhs=split_lhs,
                          split_x=split_x, split_out=split_out, emit_h=emit_h),
        grid=(N_ROWS // tm,),
        in_specs=in_specs,
        out_specs=out_specs,
        out_shape=out_shape,
        scratch_shapes=[pltpu.VMEM((k, D_MODEL), BF16)] if cast_w else [],
        compiler_params=pltpu.CompilerParams(
            dimension_semantics=("arbitrary",), vmem_limit_bytes=_vmem_limit(est)),
        name="proj_deepnorm",
    )(*args)
    x_new = (outs[0], outs[1]) if split_out else outs[0]
    return x_new, (outs[-1] if emit_h else None)


def _rope_tables(dim):
    rows = DEC_SEQ // GRID_W
    row = np.repeat(np.arange(rows, dtype=np.float64), GRID_W)
    col = np.tile(np.arange(GRID_W, dtype=np.float64), rows)
    quarter = dim // 4
    inv = ROPE_THETA ** (-np.arange(quarter, dtype=np.float64) / quarter)
    ang = np.concatenate([row[:, None] * inv, col[:, None] * inv], axis=-1)
    cos, sin, zero = np.cos(ang), np.sin(ang), np.zeros_like(ang)
    reps = LANES // dim
    full = lambda a, b: np.tile(np.concatenate([a, b], axis=-1), (1, reps)).astype(np.float32)
    return jnp.asarray(full(cos, cos)), jnp.asarray(full(-sin, zero)), jnp.asarray(full(zero, sin))


DENSE = -1


def _qkv_kernel(*refs, tm, tn, norm, rope_dim, cache_split, out_scale):
    refs = list(refs)
    take = lambda n: [refs.pop(0) for _ in range(n)]
    h_ref, w_ref, gain_ref = take(3)
    cos_ref, slo_ref, shi_ref = take(3) if rope_dim else (None, None, None)
    (o_ref,) = take(1)
    (cache_ref,) = take(1) if cache_split else (None,)
    (wbf_ref,) = take(1)

    @pl.when(pl.program_id(1) == 0)
    def _():
        wbf_ref[...] = w_ref[...].astype(BF16)

    def finish(rope, write_cache):
        for r in range(tm // SEQ):
            rs = slice(r * SEQ, (r + 1) * SEQ)
            y = jnp.dot(h_ref[rs, :], wbf_ref[...], preferred_element_type=F32)
            for c in range(tn // LANES):
                yc = y[:, c * LANES:(c + 1) * LANES]
                if norm:
                    yc = yc * lax.rsqrt(jnp.mean(yc * yc, axis=-1, keepdims=True) + EPS) * gain_ref[...]
                if rope:
                    half = rope_dim // 2
                    yc = (yc * cos_ref[rs, :]
                          + pltpu.roll(yc, LANES - half, 1) * slo_ref[rs, :]
                          + pltpu.roll(yc, half, 1) * shi_ref[rs, :])
                if out_scale != 1.0:
                    yc = yc * out_scale
                o_ref[rs, c * LANES:(c + 1) * LANES] = yc.astype(o_ref.dtype)
                if write_cache and cache_split == DENSE:
                    cache_ref[rs, c * LANES:(c + 1) * LANES] = yc
                elif write_cache:
                    width = LANES // cache_split
                    for p in range(cache_split):
                        piece = yc[:, p * width:(p + 1) * width]
                        if cache_split == 1:
                            cache_ref[r, :, c, :] = piece
                        else:
                            cache_ref[r, :, c, p, :] = piece

    if rope_dim or cache_split:
        is_latent = pl.program_id(1) * tm >= N_CTX
        pl.when(jnp.logical_not(is_latent))(lambda: finish(False, bool(cache_split)))
        pl.when(is_latent)(lambda: finish(bool(rope_dim), False))
    else:
        finish(False, False)


def _qkv_proj(h, w, layer_j, col0, width, gain, tables, norm, rope_dim, out_scale=1.0, cache_split=0):
    tm, tn = 1024, 512
    n0 = col0 // tn
    in_specs = [
        pl.BlockSpec((tm, D_MODEL), lambda j, i: (i, 0)),
        pl.BlockSpec((None, D_MODEL, tn), lambda j, i: (layer_j, 0, n0 + j)),
        pl.BlockSpec((1, LANES), lambda j, i: (0, 0)),
    ]
    args = [h, w, gain]
    if rope_dim:
        def tab_map(j, i):
            r = i * tm
            return (jnp.where(r >= N_CTX, ((r - N_CTX) % DEC_SEQ) // tm, 0), 0)
        in_specs += [pl.BlockSpec((tm, LANES), tab_map)] * 3
        args += list(tables)
    out_specs = [pl.BlockSpec((tm, tn), lambda j, i: (i, j))]
    out_shape = [jax.ShapeDtypeStruct((N_ROWS, width), BF16)]
    n_ctx = N_CTX // tm
    if cache_split == DENSE:
        out_specs.append(pl.BlockSpec((tm, tn), lambda j, i: (jnp.minimum(i, n_ctx - 1), j)))
        out_shape.append(jax.ShapeDtypeStruct((N_CTX, width), F32))
    elif cache_split:
        tail = (LANES,) if cache_split == 1 else (cache_split, LANES // cache_split)
        zeros = (0,) * len(tail)
        out_specs.append(pl.BlockSpec((tm // SEQ, None, SEQ, tn // LANES) + tail,
                                      lambda j, i: (jnp.minimum(i, n_ctx - 1), 0, 0, j) + zeros))
        out_shape.append(jax.ShapeDtypeStruct((BATCH, 1, SEQ, width // LANES) + tail, F32))
    outs = pl.pallas_call(
        functools.partial(_qkv_kernel, tm=tm, tn=tn, norm=norm, rope_dim=rope_dim,
                          cache_split=cache_split, out_scale=out_scale),
        grid=(width // tn, N_ROWS // tm),
        in_specs=in_specs,
        out_specs=out_specs,
        out_shape=out_shape,
        scratch_shapes=[pltpu.VMEM((D_MODEL, tn), BF16)],
        compiler_params=pltpu.CompilerParams(
            dimension_semantics=("arbitrary", "arbitrary"),
            vmem_limit_bytes=_vmem_limit(2 * tm * D_MODEL * 2 + 2 * D_MODEL * tn * 4 + D_MODEL * tn * 2
                                         + 2 * tm * tn * 2 + 2 * abs(cache_split) * tm * tn * 4
                                         + 6 * tm * LANES * 4 + 3 * tm * tn * 4)),
        name="qkv_proj",
    )(*args)
    return (outs[0], outs[1]) if cache_split else outs[0]


_NT = (((1,), (1,)), ((), ()))


LOG2_E = math.log2(math.e)


def _with_ones(v):
    v = v.astype(BF16)
    return jnp.concatenate([v, jnp.ones_like(v)], axis=1)


def _softmax_pv(q, keys, vals1):
    logits = [lax.dot_general(q, k, _NT, preferred_element_type=F32) for k in keys]
    m = functools.reduce(jnp.maximum, [jnp.max(s, axis=-1, keepdims=True) for s in logits])
    acc = None
    for s, v1 in zip(logits, vals1):
        pv = jnp.dot(jnp.exp2(s - m).astype(BF16), v1, preferred_element_type=F32)
        acc = pv if acc is None else acc + pv
    return acc[:, :LANES] / acc[:, LANES:]


def _attn_a_kernel(*refs, kv_heads, has_cache):
    if has_cache:
        q_ref, kc_ref, vc_ref, kn_ref, vn_ref, o_ref = refs
    else:
        q_ref, kn_ref, vn_ref, o_ref = refs
    for kv in range(kv_heads):
        ks = slice(kv * LANES, (kv + 1) * LANES)
        keys = [kn_ref[:, ks].astype(BF16)]
        vals1 = [_with_ones(vn_ref[:, ks])]
        if has_cache:
            keys.append(kc_ref[:, ks].astype(BF16))
            vals1.append(_with_ones(vc_ref[:, ks]))
        for g in range(A_REP):
            sl = slice((kv * A_REP + g) * LANES, (kv * A_REP + g + 1) * LANES)
            o_ref[:, sl] = _softmax_pv(q_ref[:, sl], keys, vals1).astype(o_ref.dtype)


def _attn_a(q, k, v, cache_k, cache_v):
    tq = SEQ
    kvs = A_KV_HEADS
    kw, gw = kvs * LANES, kvs * A_REP * LANES
    o_ctx = pl.pallas_call(
        functools.partial(_attn_a_kernel, kv_heads=kvs, has_cache=False),
        grid=(BATCH, A_KV_HEADS // kvs),
        in_specs=[
            pl.BlockSpec((tq, gw), lambda b, h: (b, h)),
            pl.BlockSpec((SEQ, kw), lambda b, h: (b, h)),
            pl.BlockSpec((SEQ, kw), lambda b, h: (b, h)),
        ],
        out_specs=pl.BlockSpec((tq, gw), lambda b, h: (b, h)),
        out_shape=jax.ShapeDtypeStruct((N_CTX, D_MODEL), BF16),
        compiler_params=pltpu.CompilerParams(
            dimension_semantics=("arbitrary", "arbitrary"),
            vmem_limit_bytes=_vmem_limit(4 * tq * gw * 2 + 4 * SEQ * kw * 2 + 4 * kvs * A_REP * tq * SEQ * 4)),
        name="attn_a_ctx",
    )(q, k, v)
    tq = 512
    kvs = 2
    kw, gw = kvs * LANES, kvs * A_REP * LANES
    nq = DEC_SEQ // tq
    row0 = N_CTX // tq
    t_all = PAST_LEN + DEC_SEQ
    o_lat = pl.pallas_call(
        functools.partial(_attn_a_kernel, kv_heads=kvs, has_cache=True),
        grid=(DEC_BATCH, A_KV_HEADS // kvs, nq),
        in_specs=[
            pl.BlockSpec((tq, gw), lambda b, h, i: (row0 + b * nq + i, h)),
            pl.BlockSpec((None, PAST_LEN, kw), lambda b, h, i: (b, 0, h)),
            pl.BlockSpec((None, PAST_LEN, kw), lambda b, h, i: (b, 0, h)),
            pl.BlockSpec((DEC_SEQ, kw), lambda b, h, i: (N_CTX // DEC_SEQ + b, h)),
            pl.BlockSpec((DEC_SEQ, kw), lambda b, h, i: (N_CTX // DEC_SEQ + b, h)),
        ],
        out_specs=pl.BlockSpec((tq, gw), lambda b, h, i: (b * nq + i, h)),
        out_shape=jax.ShapeDtypeStruct((N_LAT, D_MODEL), BF16),
        compiler_params=pltpu.CompilerParams(
            dimension_semantics=("arbitrary", "arbitrary", "arbitrary"),
            vmem_limit_bytes=_vmem_limit(4 * tq * gw * 2 + 4 * t_all * kw * 4
                                         + 2 * kvs * A_REP * tq * t_all * 4)),
        name="attn_a_lat",
    )(q, cache_k, cache_v, k, v)
    return o_ctx, o_lat


def _attn_b_kernel(*refs, heads, has_cache, lam_init):
    if has_cache:
        q_ref, kc_ref, vc_ref, kn_ref, vn_ref, lam_ref, sub_ref, o_ref = refs
    else:
        q_ref, kn_ref, vn_ref, lam_ref, sub_ref, o_ref = refs
    lp = lam_ref[...]
    lam = (jnp.exp(jnp.sum(lp[0:1] * lp[1:2], axis=-1, keepdims=True))
           - jnp.exp(jnp.sum(lp[2:3] * lp[3:4], axis=-1, keepdims=True)) + lam_init)
    first = lax.broadcasted_iota(jnp.int32, (1, LANES), 1) < B_QK_DIM
    for h in range(heads):
        sl = slice(h * LANES, (h + 1) * LANES)
        q = q_ref[:, sl]
        zero = jnp.zeros_like(q)
        keys = [kn_ref[:, sl].astype(BF16)]
        vals1 = [_with_ones(vn_ref[:, sl])]
        if has_cache:
            keys.append(kc_ref[:, sl].astype(BF16))
            vals1.append(_with_ones(vc_ref[:, sl]))
        o1 = _softmax_pv(jnp.where(first, q, zero), keys, vals1)
        o2 = _softmax_pv(jnp.where(first, zero, q), keys, vals1)
        o = o1 - lam * o2
        o = o * lax.rsqrt(jnp.mean(o * o, axis=-1, keepdims=True) + EPS) * sub_ref[...] * (1.0 - lam_init)
        o_ref[:, sl] = o.astype(o_ref.dtype)


def _attn_b(q, k, v, cache_k, cache_v, lam_p, subln, lam_init):
    heads = 8
    gw = heads * LANES
    tq = SEQ
    o_ctx = pl.pallas_call(
        functools.partial(_attn_b_kernel, heads=heads, has_cache=False, lam_init=lam_init),
        grid=(BATCH, B_HEADS // heads),
        in_specs=[
            pl.BlockSpec((tq, gw), lambda b, h: (b, h)),
            pl.BlockSpec((SEQ, gw), lambda b, h: (b, h)),
            pl.BlockSpec((SEQ, gw), lambda b, h: (b, h)),
            pl.BlockSpec((4, B_QK_DIM), lambda b, h: (0, 0)),
            pl.BlockSpec((1, LANES), lambda b, h: (0, 0)),
        ],
        out_specs=pl.BlockSpec((tq, gw), lambda b, h: (b, h)),
        out_shape=jax.ShapeDtypeStruct((N_CTX, D_MODEL), BF16),
        compiler_params=pltpu.CompilerParams(
            dimension_semantics=("arbitrary", "arbitrary"),
            vmem_limit_bytes=_vmem_limit(4 * tq * gw * 2 + 4 * SEQ * gw * 4 + 8 * heads * tq * SEQ * 4)),
        name="attn_b_ctx",
    )(q, k, v, lam_p, subln)
    tq = 512
    nq = DEC_SEQ // tq
    row0 = N_CTX // tq
    t_all = PAST_LEN + DEC_SEQ
    heads = 4
    gw = heads * LANES
    o_lat = pl.pallas_call(
        functools.partial(_attn_b_kernel, heads=heads, has_cache=True, lam_init=lam_init),
        grid=(DEC_BATCH, B_HEADS // heads, nq),
        in_specs=[
            pl.BlockSpec((tq, gw), lambda b, h, i: (row0 + b * nq + i, h)),
            pl.BlockSpec((None, PAST_LEN, gw), lambda b, h, i: (b, 0, h)),
            pl.BlockSpec((None, PAST_LEN, gw), lambda b, h, i: (b, 0, h)),
            pl.BlockSpec((DEC_SEQ, gw), lambda b, h, i: (N_CTX // DEC_SEQ + b, h)),
            pl.BlockSpec((DEC_SEQ, gw), lambda b, h, i: (N_CTX // DEC_SEQ + b, h)),
            pl.BlockSpec((4, B_QK_DIM), lambda b, h, i: (0, 0)),
            pl.BlockSpec((1, LANES), lambda b, h, i: (0, 0)),
        ],
        out_specs=pl.BlockSpec((tq, gw), lambda b, h, i: (b * nq + i, h)),
        out_shape=jax.ShapeDtypeStruct((N_LAT, D_MODEL), BF16),
        compiler_params=pltpu.CompilerParams(
            dimension_semantics=("arbitrary", "arbitrary", "arbitrary"),
            vmem_limit_bytes=_vmem_limit(4 * tq * gw * 2 + 4 * t_all * gw * 4
                                         + 3 * 2 * heads * tq * t_all * 4)),
        name="attn_b_lat",
    )(q, cache_k, cache_v, k, v, lam_p, subln)
    return o_ctx, o_lat


def _diff_lambda_init(layer_idx):
    return 0.8 - 0.6 * math.exp(-0.3 * layer_idx)


def kernel(x_prompt, x_sample, cache_a_k, cache_a_v, cache_b_k, cache_b_v, c, c_ctx, ada_w, ada_b, ln_g, ln_b, ffn_w_in, ffn_w_out, a_w_qkv, a_q_norm, a_k_norm, a_w_o, b_w_qkv, b_lambda, b_subln, b_w_o):
    ln_g3 = ln_g.reshape(DEPTH * N_SUB, 1, D_MODEL)
    ln_b3 = ln_b.reshape(DEPTH * N_SUB, 1, D_MODEL)
    tables_a = _rope_tables(A_HEAD_DIM)
    tables_b = _rope_tables(B_QK_DIM)
    ones = jnp.ones((1, LANES), F32)

    cond = jnp.concatenate([c_ctx[None, :], c, jnp.zeros((COND_ROWS - N_GROUPS, D_MODEL), F32)], axis=0)
    mod = _modulation(cond, ada_w, ada_b)

    x = (x_prompt.reshape(N_CTX, D_MODEL), x_sample.reshape(N_LAT, D_MODEL))
    h = _premod(x[0], x[1], mod, 0, 0)

    new_kv = {"ak": [], "av": [], "bk": [], "bv": []}
    for i in range(DEPTH):
        j = i // N_MIXERS
        last = i + 1 == DEPTH
        a, w_down = _ffn_up(h, ffn_w_in, ffn_w_out, i, 0)
        x, h = _proj_deepnorm(a, w_down, (), x, mod, ln_g3, ln_b3, i, 0, MACARON_WEIGHT, (i, 1))
        if i % N_MIXERS == 0:
            qw = A_HEADS * A_HEAD_DIM
            kw = A_KV_HEADS * A_HEAD_DIM
            q = _qkv_proj(h, a_w_qkv, j, 0, qw, a_q_norm[j][None, :], tables_a, True, A_HEAD_DIM,
                          out_scale=A_HEAD_DIM ** -0.5 * LOG2_E)
            k, new_k = _qkv_proj(h, a_w_qkv, j, qw, kw, a_k_norm[j][None, :], tables_a, True, A_HEAD_DIM,
                                 cache_split=1)
            v, new_v = _qkv_proj(h, a_w_qkv, j, qw + kw, kw, ones, None, False, 0, cache_split=1)
            o = _attn_a(q, k, v, cache_a_k[:, j].reshape(DEC_BATCH, PAST_LEN, kw),
                        cache_a_v[:, j].reshape(DEC_BATCH, PAST_LEN, kw))
            new_kv["ak"].append(new_k)
            new_kv["av"].append(new_v)
            w_o = a_w_o
        else:
            lam_init = _diff_lambda_init(i)
            qw = B_HEADS * 2 * B_QK_DIM
            q = _qkv_proj(h, b_w_qkv, j, 0, qw, ones, tables_b, False, B_QK_DIM,
                          out_scale=B_QK_DIM ** -0.5 * LOG2_E)
            k, new_k = _qkv_proj(h, b_w_qkv, j, qw, qw, ones, tables_b, False, B_QK_DIM, cache_split=2)
            v, new_v = _qkv_proj(h, b_w_qkv, j, 2 * qw, B_HEADS * B_V_DIM, ones, None, False, 0,
                                 cache_split=DENSE)
            new_v = new_v.reshape(BATCH, 1, SEQ, B_HEADS, B_V_DIM)
            o = _attn_b(q, k, v, cache_b_k[:, j].reshape(DEC_BATCH, PAST_LEN, qw),
                        cache_b_v[:, j].reshape(DEC_BATCH, PAST_LEN, B_HEADS * B_V_DIM),
                        b_lambda[j], b_subln[j][None, :], lam_init)
            new_kv["bk"].append(new_k)
            new_kv["bv"].append(new_v)
            w_o = b_w_o
        x, h = _proj_deepnorm(o, w_o, (j,), x, mod, ln_g3, ln_b3, i, 1, 1.0, (i, 2))
        a, w_down = _ffn_up(h, ffn_w_in, ffn_w_out, i, 1)
        x, h = _proj_deepnorm(a, w_down, (), x, mod, ln_g3, ln_b3, i, 2, MACARON_WEIGHT,
                              None if last else (i + 1, 0), split_out=last)

    y_prompt = x[0].reshape(BATCH, SEQ, D_MODEL)
    y_sample = x[1].reshape(DEC_BATCH, DEC_SEQ, D_MODEL)
    return (y_prompt, y_sample,
            jnp.concatenate(new_kv["ak"], axis=1), jnp.concatenate(new_kv["av"], axis=1),
            jnp.concatenate(new_kv["bk"], axis=1), jnp.concatenate(new_kv["bv"], axis=1))
```

```python
import functools
import math

import jax
import jax.numpy as jnp
import numpy as np
from jax import lax
from jax.experimental import pallas as pl
from jax.experimental.pallas import tpu as pltpu

D_MODEL = 2048
BATCH = 16
SEQ = 256
DEPTH = 2
DEC_BATCH = 2
DEC_SEQ = 2048
PAST_LEN = 256
GRID_W = 64
N_MIXERS = 2
N_SUB = 3
A_HEAD_DIM = 128
A_HEADS = D_MODEL // A_HEAD_DIM
A_KV_HEADS = 4
A_REP = A_HEADS // A_KV_HEADS
B_QK_DIM = 64
B_V_DIM = 2 * B_QK_DIM
B_HEADS = D_MODEL // B_V_DIM
D_FF = ((8 * D_MODEL // 3 + 127) // 128) * 128
ROPE_THETA = 10000.0
EPS = 1e-6
MACARON_WEIGHT = 0.5
DEEPNORM_ALPHA = (2 * DEPTH) ** 0.25

N_CTX = BATCH * SEQ
N_LAT = DEC_BATCH * DEC_SEQ
N_ROWS = N_CTX + N_LAT
N_GROUPS = 1 + DEC_BATCH
COND_ROWS = 8
LANES = 128
V7X_VMEM_BYTES = 64 * 1024 * 1024

F32 = jnp.float32
BF16 = jnp.bfloat16


def _vmem_limit(nbytes):
    return int(min(nbytes * 3 // 2 + (2 << 20), V7X_VMEM_BYTES - (4 << 20)))


def _group_of_row(r):
    return jnp.where(r < N_CTX, 0, 1 + (r - N_CTX) // DEC_SEQ)


def _mod_spec(layer, sub, kind, grid_rank):
    col = sub * 3 + kind
    return pl.BlockSpec((None, COND_ROWS, D_MODEL), lambda *idx: (layer, 0, col))


def _mod_row(ref, row0):
    return ref[pl.ds(_group_of_row(row0), 1), :]


def _mod_kernel(c_ref, w_ref, b_ref, o_ref):
    s = jax.nn.silu(c_ref[...]).astype(BF16)
    o_ref[...] = jnp.dot(s, w_ref[...].astype(BF16), preferred_element_type=F32) + b_ref[...]


def _modulation(cond, ada_w, ada_b):
    tn = 1024
    n_out = N_SUB * 3 * D_MODEL
    return pl.pallas_call(
        _mod_kernel,
        grid=(DEPTH, n_out // tn),
        in_specs=[
            pl.BlockSpec((COND_ROWS, D_MODEL), lambda l, j: (0, 0)),
            pl.BlockSpec((None, D_MODEL, tn), lambda l, j: (l, 0, j)),
            pl.BlockSpec((None, 1, tn), lambda l, j: (l, 0, j)),
        ],
        out_specs=pl.BlockSpec((None, COND_ROWS, tn), lambda l, j: (l, 0, j)),
        out_shape=jax.ShapeDtypeStruct((DEPTH, COND_ROWS, n_out), F32),
        compiler_params=pltpu.CompilerParams(
            dimension_semantics=("arbitrary", "arbitrary"),
            vmem_limit_bytes=_vmem_limit(2 * D_MODEL * tn * 4 + D_MODEL * tn * 2)),
        name="modulation",
    )(cond, ada_w, ada_b.reshape(DEPTH, 1, n_out))


def _split_specs(tm, width, row_axis, shift=0):
    n_ctx, n_lat = N_CTX // tm, N_LAT // tm

    def ctx_map(*idx):
        return (jnp.clip(idx[row_axis] - shift, 0, n_ctx - 1), 0)

    def lat_map(*idx):
        return (jnp.clip(idx[row_axis] - shift - n_ctx, 0, n_lat - 1), 0)

    return pl.BlockSpec((tm, width), ctx_map), pl.BlockSpec((tm, width), lat_map)


def _premod_kernel(xc_ref, xl_ref, sc_ref, sh_ref, h_ref, *, tm):
    row0 = pl.program_id(0) * tm
    sc = 1.0 + _mod_row(sc_ref, row0)
    sh = _mod_row(sh_ref, row0)

    def emit(x_ref):
        h_ref[...] = (x_ref[...] * sc + sh).astype(BF16)

    pl.when(row0 < N_CTX)(lambda: emit(xc_ref))
    pl.when(row0 >= N_CTX)(lambda: emit(xl_ref))


def _premod(x_ctx, x_lat, mod, layer, sub):
    tm = 512
    return pl.pallas_call(
        functools.partial(_premod_kernel, tm=tm),
        grid=(N_ROWS // tm,),
        in_specs=[*_split_specs(tm, D_MODEL, 0),
                  _mod_spec(layer, sub, 1, 1), _mod_spec(layer, sub, 0, 1)],
        out_specs=pl.BlockSpec((tm, D_MODEL), lambda i: (i, 0)),
        out_shape=jax.ShapeDtypeStruct((N_ROWS, D_MODEL), BF16),
        compiler_params=pltpu.CompilerParams(
            dimension_semantics=("arbitrary",),
            vmem_limit_bytes=_vmem_limit(4 * tm * D_MODEL * 4 + 2 * tm * D_MODEL * 2)),
        name="premod",
    )(x_ctx, x_lat, mod, mod)


def _up_kernel(h_ref, wg_ref, wu_ref, wo_ref, a_ref, wob_ref, wbf_ref, *, tf, ts, nf, kc, tail):
    j, i = pl.program_id(0), pl.program_id(1)
    subs = [(slice(s * ts, (s + 1) * ts), slice(2 * s * ts, 2 * (s + 1) * ts)) for s in range(tf // ts)]

    def stage():
        rows = pl.ds(pl.multiple_of(i * kc, kc), kc)
        for cols, packed in subs:
            wbf_ref[j % 2, rows, packed.start:packed.start + ts] = wg_ref[:, cols].astype(BF16)
            wbf_ref[j % 2, rows, packed.start + ts:packed.stop] = wu_ref[:, cols].astype(BF16)
        wob_ref[...] = wo_ref[...].astype(BF16)

    def multiply(first_sub=0):
        for cols, packed in subs[first_sub:]:
            gu = jnp.dot(h_ref[...], wbf_ref[(j + 1) % 2, :, packed], preferred_element_type=F32)
            a_ref[:, cols] = (jax.nn.silu(gu[:, :ts]) * gu[:, ts:]).astype(BF16)

    @pl.when(j == 0)
    def _():
        stage()

    @pl.when(jnp.logical_and(j > 0, j < nf))
    def _():
        stage()
        multiply()

    @pl.when(j == nf)
    def _():
        multiply(first_sub=(tf - tail) // ts)
        a_ref[:, :tail] = a_ref[:, tf - tail:]


def _ffn_up(h, w_in, w_out, layer, half):
    tm, tf, ts = 1024, 1024, 512
    nm = N_ROWS // tm
    nf = pl.cdiv(D_FF, tf)
    tail = D_FF - (nf - 1) * tf
    kc = D_MODEL // nm
    dc = LANES
    n_down = D_FF // dc
    assert D_FF % dc == 0 and n_down <= nf * nm
    staged = lambda j: jnp.minimum(j, nf - 1)
    krow = lambda j, i: pl.multiple_of(jnp.where(j < nf, i, nm - 1) * kc, kc)
    start = lambda j: pl.multiple_of(jnp.minimum(staged(j) * tf, D_FF - tf), LANES)
    ustart = lambda j: pl.multiple_of(D_FF + jnp.minimum(staged(j) * tf, D_FF - tf), LANES)
    down = lambda j, i: jnp.minimum(j * nm + i, n_down - 1)
    rows = lambda j, i: jnp.where(j > 0, i, 0)
    elem = pl.Element
    return pl.pallas_call(
        functools.partial(_up_kernel, tf=tf, ts=ts, nf=nf, kc=kc, tail=tail),
        grid=(nf + 1, nm),
        in_specs=[
            pl.BlockSpec((tm, D_MODEL), lambda j, i: (rows(j, i), 0)),
            pl.BlockSpec((None, None, elem(kc), elem(tf)), lambda j, i: (layer, half, krow(j, i), start(j))),
            pl.BlockSpec((None, None, elem(kc), elem(tf)), lambda j, i: (layer, half, krow(j, i), ustart(j))),
            pl.BlockSpec((None, None, dc, D_MODEL), lambda j, i: (layer, half, down(j, i), 0)),
        ],
        out_specs=[pl.BlockSpec((tm, tf), lambda j, i: (rows(j, i), jnp.maximum(j - 1, 0))),
                   pl.BlockSpec((dc, D_MODEL), lambda j, i: (down(j, i), 0))],
        out_shape=[jax.ShapeDtypeStruct((N_ROWS, D_FF), BF16),
                   jax.ShapeDtypeStruct((D_FF, D_MODEL), BF16)],
        scratch_shapes=[pltpu.VMEM((2, D_MODEL, 2 * tf), BF16)],
        compiler_params=pltpu.CompilerParams(
            dimension_semantics=("arbitrary", "arbitrary"),
            vmem_limit_bytes=_vmem_limit(2 * tm * D_MODEL * 2 + 4 * kc * tf * 4 + 2 * dc * D_MODEL * 6
                                         + 4 * D_MODEL * tf * 2 + 2 * tm * tf * 2 + 3 * tm * tf * 4)),
        name="ffn_up",
    )(h, w_in, w_in, w_out)


def _deepnorm_kernel(*refs, tm, coef, cast_w, split_lhs, split_x, split_out, emit_h):
    refs = list(refs)
    take = lambda n: [refs.pop(0) for _ in range(n)]
    lhs_refs = take(2 if split_lhs else 1)
    (w_ref,) = take(1)
    x_refs = take(2 if split_x else 1)
    gate_ref, lng_ref, lnb_ref = take(3)
    sc_ref, sh_ref = take(2) if emit_h else (None, None)
    xo_refs = take(2 if split_out else 1)
    (ho_ref,) = take(1) if emit_h else (None,)
    (wbf_ref,) = take(1) if cast_w else (w_ref,)

    if cast_w:
        @pl.when(pl.program_id(0) == 0)
        def _():
            wbf_ref[...] = w_ref[...].astype(BF16)

    row0 = pl.program_id(0) * tm
    gate = coef * _mod_row(gate_ref, row0)
    if emit_h:
        sc = 1.0 + _mod_row(sc_ref, row0)
        sh = _mod_row(sh_ref, row0)

    def body(lhs_ref, x_ref, xo_ref):
        y = jnp.dot(lhs_ref[...], wbf_ref[...], preferred_element_type=F32)
        z = DEEPNORM_ALPHA * x_ref[...] + gate * y
        mu = jnp.mean(z, axis=-1, keepdims=True)
        zc = z - mu
        var = jnp.mean(zc * zc, axis=-1, keepdims=True)
        o = zc * lax.rsqrt(var + EPS) * lng_ref[...] + lnb_ref[...]
        xo_ref[...] = o
        if emit_h:
            ho_ref[...] = (o * sc + sh).astype(BF16)

    if split_lhs or split_x or split_out:
        pl.when(row0 < N_CTX)(lambda: body(lhs_refs[0], x_refs[0], xo_refs[0]))
        pl.when(row0 >= N_CTX)(lambda: body(lhs_refs[-1], x_refs[-1], xo_refs[-1]))
    else:
        body(lhs_refs[0], x_refs[0], xo_refs[0])


def _proj_deepnorm(lhs, w, w_index, x, mod, ln_g, ln_b, layer, sub, coef, next_mod, split_out=False):
    split_lhs, split_x = isinstance(lhs, tuple), isinstance(x, tuple)
    k = (lhs[0] if split_lhs else lhs).shape[1]
    tm = 256
    cast_w = w.dtype != BF16
    emit_h = next_mod is not None
    lead = (None,) * len(w_index)
    row_spec = lambda width: pl.BlockSpec((tm, width), lambda i: (i, 0))
    in_specs, args = [], []
    if split_lhs:
        in_specs += list(_split_specs(tm, k, 0)); args += list(lhs)
    else:
        in_specs.append(row_spec(k)); args.append(lhs)
    in_specs.append(pl.BlockSpec(lead + (k, D_MODEL), lambda i: w_index + (0, 0), pipeline_mode=pl.Buffered(1)))
    args.append(w)
    if split_x:
        in_specs += list(_split_specs(tm, D_MODEL, 0)); args += list(x)
    else:
        in_specs.append(row_spec(D_MODEL)); args.append(x)
    ln_spec = pl.BlockSpec((None, 1, D_MODEL), lambda i: (layer * N_SUB + sub, 0, 0))
    in_specs += [_mod_spec(layer, sub, 2, 1), ln_spec, ln_spec]
    args += [mod, ln_g, ln_b]
    if emit_h:
        in_specs += [_mod_spec(next_mod[0], next_mod[1], 1, 1), _mod_spec(next_mod[0], next_mod[1], 0, 1)]
        args += [mod, mod]
    if split_out:
        out_specs = list(_split_specs(tm, D_MODEL, 0))
        out_shape = [jax.ShapeDtypeStruct((N_CTX, D_MODEL), F32), jax.ShapeDtypeStruct((N_LAT, D_MODEL), F32)]
    else:
        out_specs = [row_spec(D_MODEL)]
        out_shape = [jax.ShapeDtypeStruct((N_ROWS, D_MODEL), F32)]
    if emit_h:
        out_specs.append(row_spec(D_MODEL))
        out_shape.append(jax.ShapeDtypeStruct((N_ROWS, D_MODEL), BF16))
    w_bytes = k * D_MODEL * (6 if cast_w else 2)
    est = (2 * (1 + split_lhs) * tm * k * 2 + w_bytes + 2 * (2 + split_x + split_out) * tm * D_MODEL * 4
           + 2 * tm * D_MODEL * 2 + 4 * tm * D_MODEL * 4)
    outs = pl.pallas_call(
        functools.partial(_deepnorm_kernel, tm=tm, coef=coef, cast_w=cast_w, split_lhs=split_lhs,
                          split_x=split_x, split_out=split_out, emit_h=emit_h),
        grid=(N_ROWS // tm,),
        in_specs=in_specs,
        out_specs=out_specs,
        out_shape=out_shape,
        scratch_shapes=[pltpu.VMEM((k, D_MODEL), BF16)] if cast_w else [],
        compiler_params=pltpu.CompilerParams(
            dimension_semantics=("arbitrary",), vmem_limit_bytes=_vmem_limit(est)),
        name="proj_deepnorm",
    )(*args)
    x_new = (outs[0], outs[1]) if split_out else outs[0]
    return x_new, (outs[-1] if emit_h else None)


def _rope_tables(dim):
    rows = DEC_SEQ // GRID_W
    row = np.repeat(np.arange(rows, dtype=np.float64), GRID_W)
    col = np.tile(np.arange(GRID_W, dtype=np.float64), rows)
    quarter = dim // 4
    inv = ROPE_THETA ** (-np.arange(quarter, dtype=np.float64) / quarter)
    ang = np.concatenate([row[:, None] * inv, col[:, None] * inv], axis=-1)
    cos, sin, zero = np.cos(ang), np.sin(ang), np.zeros_like(ang)
    reps = LANES // dim
    full = lambda a, b: np.tile(np.concatenate([a, b], axis=-1), (1, reps)).astype(np.float32)
    return jnp.asarray(full(cos, cos)), jnp.asarray(full(-sin, zero)), jnp.asarray(full(zero, sin))


DENSE = -1


def _qkv_kernel(*refs, tm, tn, norm, rope_dim, cache_split, out_scale):
    refs = list(refs)
    take = lambda n: [refs.pop(0) for _ in range(n)]
    h_ref, w_ref, gain_ref = take(3)
    cos_ref, slo_ref, shi_ref = take(3) if rope_dim else (None, None, None)
    (o_ref,) = take(1)
    (cache_ref,) = take(1) if cache_split else (None,)
    (wbf_ref,) = take(1)

    @pl.when(pl.program_id(1) == 0)
    def _():
        wbf_ref[...] = w_ref[...].astype(BF16)

    def finish(rope, write_cache):
        for r in range(tm // SEQ):
            rs = slice(r * SEQ, (r + 1) * SEQ)
            y = jnp.dot(h_ref[rs, :], wbf_ref[...], preferred_element_type=F32)
            for c in range(tn // LANES):
                yc = y[:, c * LANES:(c + 1) * LANES]
                if norm:
                    yc = yc * lax.rsqrt(jnp.mean(yc * yc, axis=-1, keepdims=True) + EPS) * gain_ref[...]
                if rope:
                    half = rope_dim // 2
                    yc = (yc * cos_ref[rs, :]
                          + pltpu.roll(yc, LANES - half, 1) * slo_ref[rs, :]
                          + pltpu.roll(yc, half, 1) * shi_ref[rs, :])
                if out_scale != 1.0:
                    yc = yc * out_scale
                o_ref[rs, c * LANES:(c + 1) * LANES] = yc.astype(o_ref.dtype)
                if write_cache and cache_split == DENSE:
                    cache_ref[rs, c * LANES:(c + 1) * LANES] = yc
                elif write_cache:
                    width = LANES // cache_split
                    for p in range(cache_split):
                        piece = yc[:, p * width:(p + 1) * width]
                        if cache_split == 1:
                            cache_ref[r, :, c, :] = piece
                        else:
                            cache_ref[r, :, c, p, :] = piece

    if rope_dim or cache_split:
        is_latent = pl.program_id(1) * tm >= N_CTX
        pl.when(jnp.logical_not(is_latent))(lambda: finish(False, bool(cache_split)))
        pl.when(is_latent)(lambda: finish(bool(rope_dim), False))
    else:
        finish(False, False)


def _qkv_proj(h, w, layer_j, col0, width, gain, tables, norm, rope_dim, out_scale=1.0, cache_split=0):
    tm = 1024
    tn = 1024 if (width % 1024 == 0 and cache_split != 2) else 512
    n0 = col0 // tn
    in_specs = [
        pl.BlockSpec((tm, D_MODEL), lambda j, i: (i, 0)),
        pl.BlockSpec((None, D_MODEL, tn), lambda j, i: (layer_j, 0, n0 + j)),
        pl.BlockSpec((1, LANES), lambda j, i: (0, 0)),
    ]
    args = [h, w, gain]
    if rope_dim:
        def tab_map(j, i):
            r = i * tm
            return (jnp.where(r >= N_CTX, ((r - N_CTX) % DEC_SEQ) // tm, 0), 0)
        in_specs += [pl.BlockSpec((tm, LANES), tab_map)] * 3
        args += list(tables)
    out_specs = [pl.BlockSpec((tm, tn), lambda j, i: (i, j))]
    out_shape = [jax.ShapeDtypeStruct((N_ROWS, width), BF16)]
    n_ctx = N_CTX // tm
    if cache_split == DENSE:
        out_specs.append(pl.BlockSpec((tm, tn), lambda j, i: (jnp.minimum(i, n_ctx - 1), j)))
        out_shape.append(jax.ShapeDtypeStruct((N_CTX, width), F32))
    elif cache_split:
        tail = (LANES,) if cache_split == 1 else (cache_split, LANES // cache_split)
        zeros = (0,) * len(tail)
        out_specs.append(pl.BlockSpec((tm // SEQ, None, SEQ, tn // LANES) + tail,
                                      lambda j, i: (jnp.minimum(i, n_ctx - 1), 0, 0, j) + zeros))
        out_shape.append(jax.ShapeDtypeStruct((BATCH, 1, SEQ, width // LANES) + tail, F32))
    outs = pl.pallas_call(
        functools.partial(_qkv_kernel, tm=tm, tn=tn, norm=norm, rope_dim=rope_dim,
                          cache_split=cache_split, out_scale=out_scale),
        grid=(width // tn, N_ROWS // tm),
        in_specs=in_specs,
        out_specs=out_specs,
        out_shape=out_shape,
        scratch_shapes=[pltpu.VMEM((D_MODEL, tn), BF16)],
        compiler_params=pltpu.CompilerParams(
            dimension_semantics=("arbitrary", "arbitrary"),
            vmem_limit_bytes=_vmem_limit(2 * tm * D_MODEL * 2 + 2 * D_MODEL * tn * 4 + D_MODEL * tn * 2
                                         + 2 * tm * tn * 2 + 2 * abs(cache_split) * tm * tn * 4
                                         + 6 * tm * LANES * 4 + 3 * tm * tn * 4)),
        name="qkv_proj",
    )(*args)
    return (outs[0], outs[1]) if cache_split else outs[0]


_NT = (((1,), (1,)), ((), ()))


LOG2_E = math.log2(math.e)


def _with_ones(v):
    v = v.astype(BF16)
    return jnp.concatenate([v, jnp.ones_like(v)], axis=1)


def _softmax_pv(q, keys, vals1):
    logits = [lax.dot_general(q, k, _NT, preferred_element_type=F32) for k in keys]
    m = functools.reduce(jnp.maximum, [jnp.max(s, axis=-1, keepdims=True) for s in logits])
    acc = None
    for s, v1 in zip(logits, vals1):
        pv = jnp.dot(jnp.exp2(s - m).astype(BF16), v1, preferred_element_type=F32)
        acc = pv if acc is None else acc + pv
    return acc[:, :LANES] / acc[:, LANES:]


def _attn_a_kernel(*refs, kv_heads, has_cache):
    if has_cache:
        q_ref, kc_ref, vc_ref, kn_ref, vn_ref, o_ref = refs
    else:
        q_ref, kn_ref, vn_ref, o_ref = refs
    for kv in range(kv_heads):
        ks = slice(kv * LANES, (kv + 1) * LANES)
        keys = [kn_ref[:, ks].astype(BF16)]
        vals1 = [_with_ones(vn_ref[:, ks])]
        if has_cache:
            keys.append(kc_ref[:, ks].astype(BF16))
            vals1.append(_with_ones(vc_ref[:, ks]))
        for g in range(A_REP):
            sl = slice((kv * A_REP + g) * LANES, (kv * A_REP + g + 1) * LANES)
            o_ref[:, sl] = _softmax_pv(q_ref[:, sl], keys, vals1).astype(o_ref.dtype)


def _attn_a(q, k, v, cache_k, cache_v):
    tq = SEQ
    kvs = A_KV_HEADS
    kw, gw = kvs * LANES, kvs * A_REP * LANES
    o_ctx = pl.pallas_call(
        functools.partial(_attn_a_kernel, kv_heads=kvs, has_cache=False),
        grid=(BATCH, A_KV_HEADS // kvs),
        in_specs=[
            pl.BlockSpec((tq, gw), lambda b, h: (b, h)),
            pl.BlockSpec((SEQ, kw), lambda b, h: (b, h)),
            pl.BlockSpec((SEQ, kw), lambda b, h: (b, h)),
        ],
        out_specs=pl.BlockSpec((tq, gw), lambda b, h: (b, h)),
        out_shape=jax.ShapeDtypeStruct((N_CTX, D_MODEL), BF16),
        compiler_params=pltpu.CompilerParams(
            dimension_semantics=("arbitrary", "arbitrary"),
            vmem_limit_bytes=_vmem_limit(4 * tq * gw * 2 + 4 * SEQ * kw * 2 + 4 * kvs * A_REP * tq * SEQ * 4)),
        name="attn_a_ctx",
    )(q, k, v)
    tq = 512
    kvs = A_KV_HEADS
    kw, gw = kvs * LANES, kvs * A_REP * LANES
    nq = DEC_SEQ // tq
    row0 = N_CTX // tq
    t_all = PAST_LEN + DEC_SEQ
    o_lat = pl.pallas_call(
        functools.partial(_attn_a_kernel, kv_heads=kvs, has_cache=True),
        grid=(DEC_BATCH, A_KV_HEADS // kvs, nq),
        in_specs=[
            pl.BlockSpec((tq, gw), lambda b, h, i: (row0 + b * nq + i, h)),
            pl.BlockSpec((None, PAST_LEN, kw), lambda b, h, i: (b, 0, h)),
            pl.BlockSpec((None, PAST_LEN, kw), lambda b, h, i: (b, 0, h)),
            pl.BlockSpec((DEC_SEQ, kw), lambda b, h, i: (N_CTX // DEC_SEQ + b, h)),
            pl.BlockSpec((DEC_SEQ, kw), lambda b, h, i: (N_CTX // DEC_SEQ + b, h)),
        ],
        out_specs=pl.BlockSpec((tq, gw), lambda b, h, i: (b * nq + i, h)),
        out_shape=jax.ShapeDtypeStruct((N_LAT, D_MODEL), BF16),
        compiler_params=pltpu.CompilerParams(
            dimension_semantics=("arbitrary", "arbitrary", "arbitrary"),
            vmem_limit_bytes=_vmem_limit(4 * tq * gw * 2 + 4 * t_all * kw * 4
                                         + 2 * kvs * A_REP * tq * t_all * 4)),
        name="attn_a_lat",
    )(q, cache_k, cache_v, k, v)
    return o_ctx, o_lat


def _attn_b_kernel(*refs, heads, has_cache, lam_init):
    if has_cache:
        q_ref, kc_ref, vc_ref, kn_ref, vn_ref, lam_ref, sub_ref, o_ref = refs
    else:
        q_ref, kn_ref, vn_ref, lam_ref, sub_ref, o_ref = refs
    lp = lam_ref[...]
    lam = (jnp.exp(jnp.sum(lp[0:1] * lp[1:2], axis=-1, keepdims=True))
           - jnp.exp(jnp.sum(lp[2:3] * lp[3:4], axis=-1, keepdims=True)) + lam_init)
    first = lax.broadcasted_iota(jnp.int32, (1, LANES), 1) < B_QK_DIM
    for h in range(heads):
        sl = slice(h * LANES, (h + 1) * LANES)
        q = q_ref[:, sl]
        zero = jnp.zeros_like(q)
        keys = [kn_ref[:, sl].astype(BF16)]
        vals1 = [_with_ones(vn_ref[:, sl])]
        if has_cache:
            keys.append(kc_ref[:, sl].astype(BF16))
            vals1.append(_with_ones(vc_ref[:, sl]))
        o1 = _softmax_pv(jnp.where(first, q, zero), keys, vals1)
        o2 = _softmax_pv(jnp.where(first, zero, q), keys, vals1)
        o = o1 - lam * o2
        o = o * lax.rsqrt(jnp.mean(o * o, axis=-1, keepdims=True) + EPS) * sub_ref[...] * (1.0 - lam_init)
        o_ref[:, sl] = o.astype(o_ref.dtype)


def _attn_b(q, k, v, cache_k, cache_v, lam_p, subln, lam_init):
    heads = 8
    gw = heads * LANES
    tq = SEQ
    o_ctx = pl.pallas_call(
        functools.partial(_attn_b_kernel, heads=heads, has_cache=False, lam_init=lam_init),
        grid=(BATCH, B_HEADS // heads),
        in_specs=[
            pl.BlockSpec((tq, gw), lambda b, h: (b, h)),
            pl.BlockSpec((SEQ, gw), lambda b, h: (b, h)),
            pl.BlockSpec((SEQ, gw), lambda b, h: (b, h)),
            pl.BlockSpec((4, B_QK_DIM), lambda b, h: (0, 0)),
            pl.BlockSpec((1, LANES), lambda b, h: (0, 0)),
        ],
        out_specs=pl.BlockSpec((tq, gw), lambda b, h: (b, h)),
        out_shape=jax.ShapeDtypeStruct((N_CTX, D_MODEL), BF16),
        compiler_params=pltpu.CompilerParams(
            dimension_semantics=("arbitrary", "arbitrary"),
            vmem_limit_bytes=_vmem_limit(4 * tq * gw * 2 + 4 * SEQ * gw * 4 + 8 * heads * tq * SEQ * 4)),
        name="attn_b_ctx",
    )(q, k, v, lam_p, subln)
    tq = 512
    nq = DEC_SEQ // tq
    row0 = N_CTX // tq
    t_all = PAST_LEN + DEC_SEQ
    heads = 8
    gw = heads * LANES
    o_lat = pl.pallas_call(
        functools.partial(_attn_b_kernel, heads=heads, has_cache=True, lam_init=lam_init),
        grid=(DEC_BATCH, B_HEADS // heads, nq),
        in_specs=[
            pl.BlockSpec((tq, gw), lambda b, h, i: (row0 + b * nq + i, h)),
            pl.BlockSpec((None, PAST_LEN, gw), lambda b, h, i: (b, 0, h)),
            pl.BlockSpec((None, PAST_LEN, gw), lambda b, h, i: (b, 0, h)),
            pl.BlockSpec((DEC_SEQ, gw), lambda b, h, i: (N_CTX // DEC_SEQ + b, h)),
            pl.BlockSpec((DEC_SEQ, gw), lambda b, h, i: (N_CTX // DEC_SEQ + b, h)),
            pl.BlockSpec((4, B_QK_DIM), lambda b, h, i: (0, 0)),
            pl.BlockSpec((1, LANES), lambda b, h, i: (0, 0)),
        ],
        out_specs=pl.BlockSpec((tq, gw), lambda b, h, i: (b * nq + i, h)),
        out_shape=jax.ShapeDtypeStruct((N_LAT, D_MODEL), BF16),
        compiler_params=pltpu.CompilerParams(
            dimension_semantics=("arbitrary", "arbitrary", "arbitrary"),
            vmem_limit_bytes=_vmem_limit(4 * tq * gw * 2 + 4 * t_all * gw * 4
                                         + 3 * 2 * heads * tq * t_all * 4)),
        name="attn_b_lat",
    )(q, cache_k, cache_v, k, v, lam_p, subln)
    return o_ctx, o_lat


def _diff_lambda_init(layer_idx):
    return 0.8 - 0.6 * math.exp(-0.3 * layer_idx)


def kernel(x_prompt, x_sample, cache_a_k, cache_a_v, cache_b_k, cache_b_v, c, c_ctx, ada_w, ada_b, ln_g, ln_b, ffn_w_in, ffn_w_out, a_w_qkv, a_q_norm, a_k_norm, a_w_o, b_w_qkv, b_lambda, b_subln, b_w_o):
    ln_g3 = ln_g.reshape(DEPTH * N_SUB, 1, D_MODEL)
    ln_b3 = ln_b.reshape(DEPTH * N_SUB, 1, D_MODEL)
    tables_a = _rope_tables(A_HEAD_DIM)
    tables_b = _rope_tables(B_QK_DIM)
    ones = jnp.ones((1, LANES), F32)

    cond = jnp.concatenate([c_ctx[None, :], c, jnp.zeros((COND_ROWS - N_GROUPS, D_MODEL), F32)], axis=0)
    mod = _modulation(cond, ada_w, ada_b)

    x = (x_prompt.reshape(N_CTX, D_MODEL), x_sample.reshape(N_LAT, D_MODEL))
    h = _premod(x[0], x[1], mod, 0, 0)

    new_kv = {"ak": [], "av": [], "bk": [], "bv": []}
    for i in range(DEPTH):
        j = i // N_MIXERS
        last = i + 1 == DEPTH
        a, w_down = _ffn_up(h, ffn_w_in, ffn_w_out, i, 0)
        x, h = _proj_deepnorm(a, w_down, (), x, mod, ln_g3, ln_b3, i, 0, MACARON_WEIGHT, (i, 1))
        if i % N_MIXERS == 0:
            qw = A_HEADS * A_HEAD_DIM
            kw = A_KV_HEADS * A_HEAD_DIM
            q = _qkv_proj(h, a_w_qkv, j, 0, qw, a_q_norm[j][None, :], tables_a, True, A_HEAD_DIM,
                          out_scale=A_HEAD_DIM ** -0.5 * LOG2_E)
            k, new_k = _qkv_proj(h, a_w_qkv, j, qw, kw, a_k_norm[j][None, :], tables_a, True, A_HEAD_DIM,
                                 cache_split=1)
            v, new_v = _qkv_proj(h, a_w_qkv, j, qw + kw, kw, ones, None, False, 0, cache_split=1)
            o = _attn_a(q, k, v, cache_a_k[:, j].reshape(DEC_BATCH, PAST_LEN, kw),
                        cache_a_v[:, j].reshape(DEC_BATCH, PAST_LEN, kw))
            new_kv["ak"].append(new_k)
            new_kv["av"].append(new_v)
            w_o = a_w_o
        else:
            lam_init = _diff_lambda_init(i)
            qw = B_HEADS * 2 * B_QK_DIM
            q = _qkv_proj(h, b_w_qkv, j, 0, qw, ones, tables_b, False, B_QK_DIM,
                          out_scale=B_QK_DIM ** -0.5 * LOG2_E)
            k, new_k = _qkv_proj(h, b_w_qkv, j, qw, qw, ones, tables_b, False, B_QK_DIM, cache_split=2)
            v, new_v = _qkv_proj(h, b_w_qkv, j, 2 * qw, B_HEADS * B_V_DIM, ones, None, False, 0,
                                 cache_split=DENSE)
            new_v = new_v.reshape(BATCH, 1, SEQ, B_HEADS, B_V_DIM)
            o = _attn_b(q, k, v, cache_b_k[:, j].reshape(DEC_BATCH, PAST_LEN, qw),
                        cache_b_v[:, j].reshape(DEC_BATCH, PAST_LEN, B_HEADS * B_V_DIM),
                        b_lambda[j], b_subln[j][None, :], lam_init)
            new_kv["bk"].append(new_k)
            new_kv["bv"].append(new_v)
            w_o = b_w_o
        x, h = _proj_deepnorm(o, w_o, (j,), x, mod, ln_g3, ln_b3, i, 1, 1.0, (i, 2))
        a, w_down = _ffn_up(h, ffn_w_in, ffn_w_out, i, 1)
        x, h = _proj_deepnorm(a, w_down, (), x, mod, ln_g3, ln_b3, i, 2, MACARON_WEIGHT,
                              None if last else (i + 1, 0), split_out=last)

    y_prompt = x[0].reshape(BATCH, SEQ, D_MODEL)
    y_sample = x[1].reshape(DEC_BATCH, DEC_SEQ, D_MODEL)
    return (y_prompt, y_sample,
            jnp.concatenate(new_kv["ak"], axis=1), jnp.concatenate(new_kv["av"], axis=1),
            jnp.concatenate(new_kv["bk"], axis=1), jnp.concatenate(new_kv["bv"], axis=1))
```

```python
import functools
import math

import jax
import jax.numpy as jnp
import numpy as np
from jax import lax
from jax.experimental import pallas as pl
from jax.experimental.pallas import tpu as pltpu

D_MODEL = 2048
BATCH = 16
SEQ = 256
DEPTH = 2
DEC_BATCH = 2
DEC_SEQ = 2048
PAST_LEN = 256
GRID_W = 64
N_MIXERS = 2
N_SUB = 3
A_HEAD_DIM = 128
A_HEADS = D_MODEL // A_HEAD_DIM
A_KV_HEADS = 4
A_REP = A_HEADS // A_KV_HEADS
B_QK_DIM = 64
B_V_DIM = 2 * B_QK_DIM
B_HEADS = D_MODEL // B_V_DIM
D_FF = ((8 * D_MODEL // 3 + 127) // 128) * 128
ROPE_THETA = 10000.0
EPS = 1e-6
MACARON_WEIGHT = 0.5
DEEPNORM_ALPHA = (2 * DEPTH) ** 0.25

N_CTX = BATCH * SEQ
N_LAT = DEC_BATCH * DEC_SEQ
N_ROWS = N_CTX + N_LAT
N_GROUPS = 1 + DEC_BATCH
COND_ROWS = 8
LANES = 128
V7X_VMEM_BYTES = 64 * 1024 * 1024

F32 = jnp.float32
BF16 = jnp.bfloat16


def _vmem_limit(nbytes):
    return int(min(nbytes * 3 // 2 + (2 << 20), V7X_VMEM_BYTES - (4 << 20)))


def _group_of_row(r):
    return jnp.where(r < N_CTX, 0, 1 + (r - N_CTX) // DEC_SEQ)


def _mod_spec(layer, sub, kind, grid_rank):
    col = sub * 3 + kind
    return pl.BlockSpec((None, COND_ROWS, D_MODEL), lambda *idx: (layer, 0, col))


def _mod_row(ref, row0):
    return ref[pl.ds(_group_of_row(row0), 1), :]


def _mod_kernel(c_ref, w_ref, b_ref, o_ref):
    s = jax.nn.silu(c_ref[...]).astype(BF16)
    o_ref[...] = jnp.dot(s, w_ref[...].astype(BF16), preferred_element_type=F32) + b_ref[...]


def _modulation(cond, ada_w, ada_b):
    tn = 1024
    n_out = N_SUB * 3 * D_MODEL
    return pl.pallas_call(
        _mod_kernel,
        grid=(DEPTH, n_out // tn),
        in_specs=[
            pl.BlockSpec((COND_ROWS, D_MODEL), lambda l, j: (0, 0)),
            pl.BlockSpec((None, D_MODEL, tn), lambda l, j: (l, 0, j)),
            pl.BlockSpec((None, 1, tn), lambda l, j: (l, 0, j)),
        ],
        out_specs=pl.BlockSpec((None, COND_ROWS, tn), lambda l, j: (l, 0, j)),
        out_shape=jax.ShapeDtypeStruct((DEPTH, COND_ROWS, n_out), F32),
        compiler_params=pltpu.CompilerParams(
            dimension_semantics=("arbitrary", "arbitrary"),
            vmem_limit_bytes=_vmem_limit(2 * D_MODEL * tn * 4 + D_MODEL * tn * 2)),
        name="modulation",
    )(cond, ada_w, ada_b.reshape(DEPTH, 1, n_out))


def _split_specs(tm, width, row_axis, shift=0):
    n_ctx, n_lat = N_CTX // tm, N_LAT // tm

    def ctx_map(*idx):
        return (jnp.clip(idx[row_axis] - shift, 0, n_ctx - 1), 0)

    def lat_map(*idx):
        return (jnp.clip(idx[row_axis] - shift - n_ctx, 0, n_lat - 1), 0)

    return pl.BlockSpec((tm, width), ctx_map), pl.BlockSpec((tm, width), lat_map)


def _premod_kernel(xc_ref, xl_ref, sc_ref, sh_ref, h_ref, *, tm):
    row0 = pl.program_id(0) * tm
    sc = 1.0 + _mod_row(sc_ref, row0)
    sh = _mod_row(sh_ref, row0)

    def emit(x_ref):
        h_ref[...] = (x_ref[...] * sc + sh).astype(BF16)

    pl.when(row0 < N_CTX)(lambda: emit(xc_ref))
    pl.when(row0 >= N_CTX)(lambda: emit(xl_ref))


def _premod(x_ctx, x_lat, mod, layer, sub):
    tm = 512
    return pl.pallas_call(
        functools.partial(_premod_kernel, tm=tm),
        grid=(N_ROWS // tm,),
        in_specs=[*_split_specs(tm, D_MODEL, 0),
                  _mod_spec(layer, sub, 1, 1), _mod_spec(layer, sub, 0, 1)],
        out_specs=pl.BlockSpec((tm, D_MODEL), lambda i: (i, 0)),
        out_shape=jax.ShapeDtypeStruct((N_ROWS, D_MODEL), BF16),
        compiler_params=pltpu.CompilerParams(
            dimension_semantics=("arbitrary",),
            vmem_limit_bytes=_vmem_limit(4 * tm * D_MODEL * 4 + 2 * tm * D_MODEL * 2)),
        name="premod",
    )(x_ctx, x_lat, mod, mod)


def _up_kernel(h_ref, wg_ref, wu_ref, wo_ref, a_ref, wob_ref, wbf_ref, *, tf, ts, nf, kc, tail):
    j, i = pl.program_id(0), pl.program_id(1)
    n_sub = tf // ts
    lead = ts - tail

    def stage():
        rows = pl.ds(pl.multiple_of(i * kc, kc), kc)
        for s in range(n_sub):
            p0 = 2 * s * ts
            wbf_ref[j % 2, rows, p0:p0 + ts] = wg_ref[:, s * ts:(s + 1) * ts].astype(BF16)
            if s < n_sub - 1:
                wbf_ref[j % 2, rows, p0 + ts:p0 + 2 * ts] = wu_ref[:, s * ts:(s + 1) * ts].astype(BF16)
            else:
                wbf_ref[j % 2, rows, p0 + ts:p0 + ts + tail] = wu_ref[:, s * ts + lead:(s + 1) * ts].astype(BF16)
                wbf_ref[j % 2, rows, p0 + ts + tail:p0 + 2 * ts] = wu_ref[:, s * ts:s * ts + lead].astype(BF16)
        wob_ref[...] = wo_ref[...].astype(BF16)

    def swiglu(g, u):
        return (jax.nn.silu(g) * u).astype(BF16)

    def multiply():
        for s in range(n_sub):
            p0, c0 = 2 * s * ts, s * ts
            gu = jnp.dot(h_ref[...], wbf_ref[(j + 1) % 2, :, p0:p0 + 2 * ts], preferred_element_type=F32)
            if s < n_sub - 1:
                a_ref[:, c0:c0 + ts] = swiglu(gu[:, :ts], gu[:, ts:])
            else:
                a_ref[:, c0:c0 + lead] = swiglu(gu[:, :lead], gu[:, ts + tail:])
                a_ref[:, c0 + lead:c0 + ts] = swiglu(gu[:, lead:ts], gu[:, ts:ts + tail])

    @pl.when(j == 0)
    def _():
        stage()

    @pl.when(jnp.logical_and(j > 0, j < nf))
    def _():
        stage()
        multiply()

    @pl.when(j == nf)
    def _():
        p0 = 2 * (n_sub - 1) * ts + lead
        gu = jnp.dot(h_ref[...], wbf_ref[(j + 1) % 2, :, p0:p0 + 2 * tail], preferred_element_type=F32)
        a_ref[:, :tail] = swiglu(gu[:, :tail], gu[:, tail:])


def _ffn_up(h, w_in, w_out, layer, half):
    tm, tf, ts = 1024, 1024, 512
    nm = N_ROWS // tm
    nf = pl.cdiv(D_FF, tf)
    tail = D_FF - (nf - 1) * tf
    kc = D_MODEL // nm
    dc = LANES
    n_down = D_FF // dc
    assert 0 < tail <= ts
    assert D_FF % dc == 0 and n_down <= nf * nm
    staged = lambda j: jnp.minimum(j, nf - 1)
    krow = lambda j, i: pl.multiple_of(jnp.where(j < nf, i, nm - 1) * kc, kc)
    start = lambda j: pl.multiple_of(jnp.minimum(staged(j) * tf, D_FF - tf), LANES)
    ustart = lambda j: pl.multiple_of(D_FF + jnp.minimum(staged(j) * tf, D_FF - tf), LANES)
    down = lambda j, i: jnp.minimum(j * nm + i, n_down - 1)
    rows = lambda j, i: jnp.where(j > 0, i, 0)
    elem = pl.Element
    return pl.pallas_call(
        functools.partial(_up_kernel, tf=tf, ts=ts, nf=nf, kc=kc, tail=tail),
        grid=(nf + 1, nm),
        in_specs=[
            pl.BlockSpec((tm, D_MODEL), lambda j, i: (rows(j, i), 0)),
            pl.BlockSpec((None, None, elem(kc), elem(tf)), lambda j, i: (layer, half, krow(j, i), start(j))),
            pl.BlockSpec((None, None, elem(kc), elem(tf)), lambda j, i: (layer, half, krow(j, i), ustart(j))),
            pl.BlockSpec((None, None, dc, D_MODEL), lambda j, i: (layer, half, down(j, i), 0)),
        ],
        out_specs=[pl.BlockSpec((tm, tf), lambda j, i: (rows(j, i), jnp.maximum(j - 1, 0))),
                   pl.BlockSpec((dc, D_MODEL), lambda j, i: (down(j, i), 0))],
        out_shape=[jax.ShapeDtypeStruct((N_ROWS, D_FF), BF16),
                   jax.ShapeDtypeStruct((D_FF, D_MODEL), BF16)],
        scratch_shapes=[pltpu.VMEM((2, D_MODEL, 2 * tf), BF16)],
        compiler_params=pltpu.CompilerParams(
            dimension_semantics=("arbitrary", "arbitrary"),
            vmem_limit_bytes=_vmem_limit(2 * tm * D_MODEL * 2 + 4 * kc * tf * 4 + 2 * dc * D_MODEL * 6
                                         + 4 * D_MODEL * tf * 2 + 2 * tm * tf * 2 + 3 * tm * tf * 4)),
        name="ffn_up",
    )(h, w_in, w_in, w_out)


def _deepnorm_kernel(*refs, tm, coef, cast_w, split_lhs, split_x, split_out, emit_h):
    refs = list(refs)
    take = lambda n: [refs.pop(0) for _ in range(n)]
    lhs_refs = take(2 if split_lhs else 1)
    (w_ref,) = take(1)
    x_refs = take(2 if split_x else 1)
    gate_ref, lng_ref, lnb_ref = take(3)
    sc_ref, sh_ref = take(2) if emit_h else (None, None)
    xo_refs = take(2 if split_out else 1)
    (ho_ref,) = take(1) if emit_h else (None,)
    (wbf_ref,) = take(1) if cast_w else (w_ref,)

    if cast_w:
        @pl.when(pl.program_id(0) == 0)
        def _():
            wbf_ref[...] = w_ref[...].astype(BF16)

    row0 = pl.program_id(0) * tm
    gate = coef * _mod_row(gate_ref, row0)
    if emit_h:
        sc = 1.0 + _mod_row(sc_ref, row0)
        sh = _mod_row(sh_ref, row0)

    def body(lhs_ref, x_ref, xo_ref):
        y = jnp.dot(lhs_ref[...], wbf_ref[...], preferred_element_type=F32)
        z = DEEPNORM_ALPHA * x_ref[...] + gate * y
        mu = jnp.mean(z, axis=-1, keepdims=True)
        zc = z - mu
        var = jnp.mean(zc * zc, axis=-1, keepdims=True)
        o = zc * lax.rsqrt(var + EPS) * lng_ref[...] + lnb_ref[...]
        xo_ref[...] = o
        if emit_h:
            ho_ref[...] = (o * sc + sh).astype(BF16)

    if split_lhs or split_x or split_out:
        pl.when(row0 < N_CTX)(lambda: body(lhs_refs[0], x_refs[0], xo_refs[0]))
        pl.when(row0 >= N_CTX)(lambda: body(lhs_refs[-1], x_refs[-1], xo_refs[-1]))
    else:
        body(lhs_refs[0], x_refs[0], xo_refs[0])


def _proj_deepnorm(lhs, w, w_index, x, mod, ln_g, ln_b, layer, sub, coef, next_mod, split_out=False):
    split_lhs, split_x = isinstance(lhs, tuple), isinstance(x, tuple)
    k = (lhs[0] if split_lhs else lhs).shape[1]
    tm = 256
    cast_w = w.dtype != BF16
    emit_h = next_mod is not None
    lead = (None,) * len(w_index)
    row_spec = lambda width: pl.BlockSpec((tm, width), lambda i: (i, 0))
    in_specs, args = [], []
    if split_lhs:
        in_specs += list(_split_specs(tm, k, 0)); args += list(lhs)
    else:
        in_specs.append(row_spec(k)); args.append(lhs)
    in_specs.append(pl.BlockSpec(lead + (k, D_MODEL), lambda i: w_index + (0, 0), pipeline_mode=pl.Buffered(1)))
    args.append(w)
    if split_x:
        in_specs += list(_split_specs(tm, D_MODEL, 0)); args += list(x)
    else:
        in_specs.append(row_spec(D_MODEL)); args.append(x)
    ln_spec = pl.BlockSpec((None, 1, D_MODEL), lambda i: (layer * N_SUB + sub, 0, 0))
    in_specs += [_mod_spec(layer, sub, 2, 1), ln_spec, ln_spec]
    args += [mod, ln_g, ln_b]
    if emit_h:
        in_specs += [_mod_spec(next_mod[0], next_mod[1], 1, 1), _mod_spec(next_mod[0], next_mod[1], 0, 1)]
        args += [mod, mod]
    if split_out:
        out_specs = list(_split_specs(tm, D_MODEL, 0))
        out_shape = [jax.ShapeDtypeStruct((N_CTX, D_MODEL), F32), jax.ShapeDtypeStruct((N_LAT, D_MODEL), F32)]
    else:
        out_specs = [row_spec(D_MODEL)]
        out_shape = [jax.ShapeDtypeStruct((N_ROWS, D_MODEL), F32)]
    if emit_h:
        out_specs.append(row_spec(D_MODEL))
        out_shape.append(jax.ShapeDtypeStruct((N_ROWS, D_MODEL), BF16))
    w_bytes = k * D_MODEL * (6 if cast_w else 2)
    est = (2 * (1 + split_lhs) * tm * k * 2 + w_bytes + 2 * (2 + split_x + split_out) * tm * D_MODEL * 4
           + 2 * tm * D_MODEL * 2 + 4 * tm * D_MODEL * 4)
    outs = pl.pallas_call(
        functools.partial(_deepnorm_kernel, tm=tm, coef=coef, cast_w=cast_w, split_lhs=split_lhs,
                          split_x=split_x, split_out=split_out, emit_h=emit_h),
        grid=(N_ROWS // tm,),
        in_specs=in_specs,
        out_specs=out_specs,
        out_shape=out_shape,
        scratch_shapes=[pltpu.VMEM((k, D_MODEL), BF16)] if cast_w else [],
        compiler_params=pltpu.CompilerParams(
            dimension_semantics=("arbitrary",), vmem_limit_bytes=_vmem_limit(est)),
        name="proj_deepnorm",
    )(*args)
    x_new = (outs[0], outs[1]) if split_out else outs[0]
    return x_new, (outs[-1] if emit_h else None)


def _rope_tables(dim):
    rows = DEC_SEQ // GRID_W
    row = np.repeat(np.arange(rows, dtype=np.float64), GRID_W)
    col = np.tile(np.arange(GRID_W, dtype=np.float64), rows)
    quarter = dim // 4
    inv = ROPE_THETA ** (-np.arange(quarter, dtype=np.float64) / quarter)
    ang = np.concatenate([row[:, None] * inv, col[:, None] * inv], axis=-1)
    cos, sin, zero = np.cos(ang), np.sin(ang), np.zeros_like(ang)
    reps = LANES // dim
    full = lambda a, b: np.tile(np.concatenate([a, b], axis=-1), (1, reps)).astype(np.float32)
    return jnp.asarray(full(cos, cos)), jnp.asarray(full(-sin, zero)), jnp.asarray(full(zero, sin))


DENSE = -1


def _qkv_kernel(*refs, tm, tn, norm, rope_dim, cache_split, out_scale):
    refs = list(refs)
    take = lambda n: [refs.pop(0) for _ in range(n)]
    h_ref, w_ref, gain_ref = take(3)
    cos_ref, slo_ref, shi_ref = take(3) if rope_dim else (None, None, None)
    (o_ref,) = take(1)
    (cache_ref,) = take(1) if cache_split else (None,)
    (wbf_ref,) = take(1)

    @pl.when(pl.program_id(1) == 0)
    def _():
        wbf_ref[...] = w_ref[...].astype(BF16)

    def finish(rope, write_cache):
        for r in range(tm // SEQ):
            rs = slice(r * SEQ, (r + 1) * SEQ)
            y = jnp.dot(h_ref[rs, :], wbf_ref[...], preferred_element_type=F32)
            pieces = []
            for c in range(tn // LANES):
                yc = y[:, c * LANES:(c + 1) * LANES]
                if norm:
                    yc = yc * lax.rsqrt(jnp.mean(yc * yc, axis=-1, keepdims=True) + EPS) * gain_ref[...]
                if rope:
                    half = rope_dim // 2
                    yc = (yc * cos_ref[rs, :]
                          + pltpu.roll(yc, LANES - half, 1) * slo_ref[rs, :]
                          + pltpu.roll(yc, half, 1) * shi_ref[rs, :])
                if out_scale != 1.0:
                    yc = yc * out_scale
                o_ref[rs, c * LANES:(c + 1) * LANES] = yc.astype(o_ref.dtype)
                pieces.append(yc)
                if write_cache and cache_split == DENSE:
                    cache_ref[rs, c * LANES:(c + 1) * LANES] = yc
                elif write_cache and cache_split == 1:
                    cache_ref[r, :, c, :] = yc
            if write_cache and cache_split > 1:
                full = jnp.concatenate(pieces, axis=1)
                cache_ref[r] = full.reshape(SEQ, tn // LANES, cache_split, LANES // cache_split)

    if rope_dim or cache_split:
        is_latent = pl.program_id(1) * tm >= N_CTX
        pl.when(jnp.logical_not(is_latent))(lambda: finish(False, bool(cache_split)))
        pl.when(is_latent)(lambda: finish(bool(rope_dim), False))
    else:
        finish(False, False)


def _qkv_proj(h, w, layer_j, col0, width, gain, tables, norm, rope_dim, out_scale=1.0, cache_split=0):
    tm = 1024
    tn = 1024 if (width % 1024 == 0 and cache_split != 2) else 512
    n0 = col0 // tn
    in_specs = [
        pl.BlockSpec((tm, D_MODEL), lambda j, i: (i, 0)),
        pl.BlockSpec((None, D_MODEL, tn), lambda j, i: (layer_j, 0, n0 + j)),
        pl.BlockSpec((1, LANES), lambda j, i: (0, 0)),
    ]
    args = [h, w, gain]
    if rope_dim:
        def tab_map(j, i):
            r = i * tm
            return (jnp.where(r >= N_CTX, ((r - N_CTX) % DEC_SEQ) // tm, 0), 0)
        in_specs += [pl.BlockSpec((tm, LANES), tab_map)] * 3
        args += list(tables)
    out_specs = [pl.BlockSpec((tm, tn), lambda j, i: (i, j))]
    out_shape = [jax.ShapeDtypeStruct((N_ROWS, width), BF16)]
    n_ctx = N_CTX // tm
    if cache_split == DENSE:
        out_specs.append(pl.BlockSpec((tm, tn), lambda j, i: (jnp.minimum(i, n_ctx - 1), j)))
        out_shape.append(jax.ShapeDtypeStruct((N_CTX, width), F32))
    elif cache_split:
        tail = (LANES,) if cache_split == 1 else (cache_split, LANES // cache_split)
        zeros = (0,) * len(tail)
        out_specs.append(pl.BlockSpec((tm // SEQ, None, SEQ, tn // LANES) + tail,
                                      lambda j, i: (jnp.minimum(i, n_ctx - 1), 0, 0, j) + zeros))
        out_shape.append(jax.ShapeDtypeStruct((BATCH, 1, SEQ, width // LANES) + tail, F32))
    outs = pl.pallas_call(
        functools.partial(_qkv_kernel, tm=tm, tn=tn, norm=norm, rope_dim=rope_dim,
                          cache_split=cache_split, out_scale=out_scale),
        grid=(width // tn, N_ROWS // tm),
        in_specs=in_specs,
        out_specs=out_specs,
        out_shape=out_shape,
        scratch_shapes=[pltpu.VMEM((D_MODEL, tn), BF16)],
        compiler_params=pltpu.CompilerParams(
            dimension_semantics=("arbitrary", "arbitrary"),
            vmem_limit_bytes=_vmem_limit(2 * tm * D_MODEL * 2 + 2 * D_MODEL * tn * 4 + D_MODEL * tn * 2
                                         + 2 * tm * tn * 2 + 2 * abs(cache_split) * tm * tn * 4
                                         + 6 * tm * LANES * 4 + 3 * tm * tn * 4)),
        name="qkv_proj",
    )(*args)
    return (outs[0], outs[1]) if cache_split else outs[0]


_NT = (((1,), (1,)), ((), ()))


LOG2_E = math.log2(math.e)


def _with_ones(v):
    v = v.astype(BF16)
    return jnp.concatenate([v, jnp.ones_like(v)], axis=1)


def _softmax_pv(q, keys, vals1):
    logits = [lax.dot_general(q, k, _NT, preferred_element_type=F32) for k in keys]
    m = functools.reduce(jnp.maximum, [jnp.max(s, axis=-1, keepdims=True) for s in logits])
    acc = None
    for s, v1 in zip(logits, vals1):
        pv = jnp.dot(jnp.exp2(s - m).astype(BF16), v1, preferred_element_type=F32)
        acc = pv if acc is None else acc + pv
    return acc[:, :LANES] / acc[:, LANES:]


def _attn_a_kernel(*refs, kv_heads, has_cache):
    if has_cache:
        q_ref, kc_ref, vc_ref, kn_ref, vn_ref, o_ref = refs
    else:
        q_ref, kn_ref, vn_ref, o_ref = refs
    for kv in range(kv_heads):
        ks = slice(kv * LANES, (kv + 1) * LANES)
        keys = [kn_ref[:, ks].astype(BF16)]
        vals1 = [_with_ones(vn_ref[:, ks])]
        if has_cache:
            keys.append(kc_ref[:, ks].astype(BF16))
            vals1.append(_with_ones(vc_ref[:, ks]))
        for g in range(A_REP):
            sl = slice((kv * A_REP + g) * LANES, (kv * A_REP + g + 1) * LANES)
            o_ref[:, sl] = _softmax_pv(q_ref[:, sl], keys, vals1).astype(o_ref.dtype)


def _attn_a(q, k, v, cache_k, cache_v):
    tq = SEQ
    kvs = A_KV_HEADS
    kw, gw = kvs * LANES, kvs * A_REP * LANES
    o_ctx = pl.pallas_call(
        functools.partial(_attn_a_kernel, kv_heads=kvs, has_cache=False),
        grid=(BATCH, A_KV_HEADS // kvs),
        in_specs=[
            pl.BlockSpec((tq, gw), lambda b, h: (b, h)),
            pl.BlockSpec((SEQ, kw), lambda b, h: (b, h)),
            pl.BlockSpec((SEQ, kw), lambda b, h: (b, h)),
        ],
        out_specs=pl.BlockSpec((tq, gw), lambda b, h: (b, h)),
        out_shape=jax.ShapeDtypeStruct((N_CTX, D_MODEL), BF16),
        compiler_params=pltpu.CompilerParams(
            dimension_semantics=("arbitrary", "arbitrary"),
            vmem_limit_bytes=_vmem_limit(4 * tq * gw * 2 + 4 * SEQ * kw * 2 + 4 * kvs * A_REP * tq * SEQ * 4)),
        name="attn_a_ctx",
    )(q, k, v)
    tq = 512
    kvs = A_KV_HEADS
    kw, gw = kvs * LANES, kvs * A_REP * LANES
    nq = DEC_SEQ // tq
    row0 = N_CTX // tq
    t_all = PAST_LEN + DEC_SEQ
    o_lat = pl.pallas_call(
        functools.partial(_attn_a_kernel, kv_heads=kvs, has_cache=True),
        grid=(DEC_BATCH, A_KV_HEADS // kvs, nq),
        in_specs=[
            pl.BlockSpec((tq, gw), lambda b, h, i: (row0 + b * nq + i, h)),
            pl.BlockSpec((None, PAST_LEN, kw), lambda b, h, i: (b, 0, h)),
            pl.BlockSpec((None, PAST_LEN, kw), lambda b, h, i: (b, 0, h)),
            pl.BlockSpec((DEC_SEQ, kw), lambda b, h, i: (N_CTX // DEC_SEQ + b, h)),
            pl.BlockSpec((DEC_SEQ, kw), lambda b, h, i: (N_CTX // DEC_SEQ + b, h)),
        ],
        out_specs=pl.BlockSpec((tq, gw), lambda b, h, i: (b * nq + i, h)),
        out_shape=jax.ShapeDtypeStruct((N_LAT, D_MODEL), BF16),
        compiler_params=pltpu.CompilerParams(
            dimension_semantics=("arbitrary", "arbitrary", "arbitrary"),
            vmem_limit_bytes=_vmem_limit(4 * tq * gw * 2 + 4 * t_all * kw * 4
                                         + 2 * kvs * A_REP * tq * t_all * 4)),
        name="attn_a_lat",
    )(q, cache_k, cache_v, k, v)
    return o_ctx, o_lat


def _attn_b_kernel(*refs, heads, has_cache, lam_init):
    if has_cache:
        q_ref, kc_ref, vc_ref, kn_ref, vn_ref, lam_ref, sub_ref, o_ref = refs
    else:
        q_ref, kn_ref, vn_ref, lam_ref, sub_ref, o_ref = refs
    lp = lam_ref[...]
    lam = (jnp.exp(jnp.sum(lp[0:1] * lp[1:2], axis=-1, keepdims=True))
           - jnp.exp(jnp.sum(lp[2:3] * lp[3:4], axis=-1, keepdims=True)) + lam_init)
    first = lax.broadcasted_iota(jnp.int32, (1, LANES), 1) < B_QK_DIM
    for h in range(heads):
        sl = slice(h * LANES, (h + 1) * LANES)
        q = q_ref[:, sl]
        zero = jnp.zeros_like(q)
        keys = [kn_ref[:, sl].astype(BF16)]
        vals1 = [_with_ones(vn_ref[:, sl])]
        if has_cache:
            keys.append(kc_ref[:, sl].astype(BF16))
            vals1.append(_with_ones(vc_ref[:, sl]))
        o1 = _softmax_pv(jnp.where(first, q, zero), keys, vals1)
        o2 = _softmax_pv(jnp.where(first, zero, q), keys, vals1)
        o = o1 - lam * o2
        o = o * lax.rsqrt(jnp.mean(o * o, axis=-1, keepdims=True) + EPS) * sub_ref[...] * (1.0 - lam_init)
        o_ref[:, sl] = o.astype(o_ref.dtype)


def _attn_b(q, k, v, cache_k, cache_v, lam_p, subln, lam_init):
    heads = 8
    gw = heads * LANES
    tq = SEQ
    o_ctx = pl.pallas_call(
        functools.partial(_attn_b_kernel, heads=heads, has_cache=False, lam_init=lam_init),
        grid=(BATCH, B_HEADS // heads),
        in_specs=[
            pl.BlockSpec((tq, gw), lambda b, h: (b, h)),
            pl.BlockSpec((SEQ, gw), lambda b, h: (b, h)),
            pl.BlockSpec((SEQ, gw), lambda b, h: (b, h)),
            pl.BlockSpec((4, B_QK_DIM), lambda b, h: (0, 0)),
            pl.BlockSpec((1, LANES), lambda b, h: (0, 0)),
        ],
        out_specs=pl.BlockSpec((tq, gw), lambda b, h: (b, h)),
        out_shape=jax.ShapeDtypeStruct((N_CTX, D_MODEL), BF16),
        compiler_params=pltpu.CompilerParams(
            dimension_semantics=("arbitrary", "arbitrary"),
            vmem_limit_bytes=_vmem_limit(4 * tq * gw * 2 + 4 * SEQ * gw * 4 + 8 * heads * tq * SEQ * 4)),
        name="attn_b_ctx",
    )(q, k, v, lam_p, subln)
    tq = 512
    nq = DEC_SEQ // tq
    row0 = N_CTX // tq
    t_all = PAST_LEN + DEC_SEQ
    heads = 8
    gw = heads * LANES
    o_lat = pl.pallas_call(
        functools.partial(_attn_b_kernel, heads=heads, has_cache=True, lam_init=lam_init),
        grid=(DEC_BATCH, B_HEADS // heads, nq),
        in_specs=[
            pl.BlockSpec((tq, gw), lambda b, h, i: (row0 + b * nq + i, h)),
            pl.BlockSpec((None, PAST_LEN, gw), lambda b, h, i: (b, 0, h)),
            pl.BlockSpec((None, PAST_LEN, gw), lambda b, h, i: (b, 0, h)),
            pl.BlockSpec((DEC_SEQ, gw), lambda b, h, i: (N_CTX // DEC_SEQ + b, h)),
            pl.BlockSpec((DEC_SEQ, gw), lambda b, h, i: (N_CTX // DEC_SEQ + b, h)),
            pl.BlockSpec((4, B_QK_DIM), lambda b, h, i: (0, 0)),
            pl.BlockSpec((1, LANES), lambda b, h, i: (0, 0)),
        ],
        out_specs=pl.BlockSpec((tq, gw), lambda b, h, i: (b * nq + i, h)),
        out_shape=jax.ShapeDtypeStruct((N_LAT, D_MODEL), BF16),
        compiler_params=pltpu.CompilerParams(
            dimension_semantics=("arbitrary", "arbitrary", "arbitrary"),
            vmem_limit_bytes=_vmem_limit(4 * tq * gw * 2 + 4 * t_all * gw * 4
                                         + 3 * 2 * heads * tq * t_all * 4)),
        name="attn_b_lat",
    )(q, cache_k, cache_v, k, v, lam_p, subln)
    return o_ctx, o_lat


def _diff_lambda_init(layer_idx):
    return 0.8 - 0.6 * math.exp(-0.3 * layer_idx)


def kernel(x_prompt, x_sample, cache_a_k, cache_a_v, cache_b_k, cache_b_v, c, c_ctx, ada_w, ada_b, ln_g, ln_b, ffn_w_in, ffn_w_out, a_w_qkv, a_q_norm, a_k_norm, a_w_o, b_w_qkv, b_lambda, b_subln, b_w_o):
    ln_g3 = ln_g.reshape(DEPTH * N_SUB, 1, D_MODEL)
    ln_b3 = ln_b.reshape(DEPTH * N_SUB, 1, D_MODEL)
    tables_a = _rope_tables(A_HEAD_DIM)
    tables_b = _rope_tables(B_QK_DIM)
    ones = jnp.ones((1, LANES), F32)

    cond = jnp.concatenate([c_ctx[None, :], c, jnp.zeros((COND_ROWS - N_GROUPS, D_MODEL), F32)], axis=0)
    mod = _modulation(cond, ada_w, ada_b)

    x = (x_prompt.reshape(N_CTX, D_MODEL), x_sample.reshape(N_LAT, D_MODEL))
    h = _premod(x[0], x[1], mod, 0, 0)

    new_kv = {"ak": [], "av": [], "bk": [], "bv": []}
    for i in range(DEPTH):
        j = i // N_MIXERS
        last = i + 1 == DEPTH
        a, w_down = _ffn_up(h, ffn_w_in, ffn_w_out, i, 0)
        x, h = _proj_deepnorm(a, w_down, (), x, mod, ln_g3, ln_b3, i, 0, MACARON_WEIGHT, (i, 1))
        if i % N_MIXERS == 0:
            qw = A_HEADS * A_HEAD_DIM
            kw = A_KV_HEADS * A_HEAD_DIM
            q = _qkv_proj(h, a_w_qkv, j, 0, qw, a_q_norm[j][None, :], tables_a, True, A_HEAD_DIM,
                          out_scale=A_HEAD_DIM ** -0.5 * LOG2_E)
            k, new_k = _qkv_proj(h, a_w_qkv, j, qw, kw, a_k_norm[j][None, :], tables_a, True, A_HEAD_DIM,
                                 cache_split=1)
            v, new_v = _qkv_proj(h, a_w_qkv, j, qw + kw, kw, ones, None, False, 0, cache_split=1)
            o = _attn_a(q, k, v, cache_a_k[:, j].reshape(DEC_BATCH, PAST_LEN, kw),
                        cache_a_v[:, j].reshape(DEC_BATCH, PAST_LEN, kw))
            new_kv["ak"].append(new_k)
            new_kv["av"].append(new_v)
            w_o = a_w_o
        else:
            lam_init = _diff_lambda_init(i)
            qw = B_HEADS * 2 * B_QK_DIM
            q = _qkv_proj(h, b_w_qkv, j, 0, qw, ones, tables_b, False, B_QK_DIM,
                          out_scale=B_QK_DIM ** -0.5 * LOG2_E)
            k, new_k = _qkv_proj(h, b_w_qkv, j, qw, qw, ones, tables_b, False, B_QK_DIM, cache_split=2)
            v, new_v = _qkv_proj(h, b_w_qkv, j, 2 * qw, B_HEADS * B_V_DIM, ones, None, False, 0,
                                 cache_split=DENSE)
            new_v = new_v.reshape(BATCH, 1, SEQ, B_HEADS, B_V_DIM)
            o = _attn_b(q, k, v, cache_b_k[:, j].reshape(DEC_BATCH, PAST_LEN, qw),
                        cache_b_v[:, j].reshape(DEC_BATCH, PAST_LEN, B_HEADS * B_V_DIM),
                        b_lambda[j], b_subln[j][None, :], lam_init)
            new_kv["bk"].append(new_k)
            new_kv["bv"].append(new_v)
            w_o = b_w_o
        x, h = _proj_deepnorm(o, w_o, (j,), x, mod, ln_g3, ln_b3, i, 1, 1.0, (i, 2))
        a, w_down = _ffn_up(h, ffn_w_in, ffn_w_out, i, 1)
        x, h = _proj_deepnorm(a, w_down, (), x, mod, ln_g3, ln_b3, i, 2, MACARON_WEIGHT,
                              None if last else (i + 1, 0), split_out=last)

    y_prompt = x[0].reshape(BATCH, SEQ, D_MODEL)
    y_sample = x[1].reshape(DEC_BATCH, DEC_SEQ, D_MODEL)
    return (y_prompt, y_sample,
            jnp.concatenate(new_kv["ak"], axis=1), jnp.concatenate(new_kv["av"], axis=1),
            jnp.concatenate(new_kv["bk"], axis=1), jnp.concatenate(new_kv["bv"], axis=1))
```

```python
import functools
import math

import jax
import jax.numpy as jnp
import numpy as np
from jax import lax
from jax.experimental import pallas as pl
from jax.experimental.pallas import tpu as pltpu

D_MODEL = 2048
BATCH = 16
SEQ = 256
DEPTH = 2
DEC_BATCH = 2
DEC_SEQ = 2048
PAST_LEN = 256
GRID_W = 64
N_MIXERS = 2
N_SUB = 3
A_HEAD_DIM = 128
A_HEADS = D_MODEL // A_HEAD_DIM
A_KV_HEADS = 4
A_REP = A_HEADS // A_KV_HEADS
B_QK_DIM = 64
B_V_DIM = 2 * B_QK_DIM
B_HEADS = D_MODEL // B_V_DIM
D_FF = ((8 * D_MODEL // 3 + 127) // 128) * 128
ROPE_THETA = 10000.0
EPS = 1e-6
MACARON_WEIGHT = 0.5
DEEPNORM_ALPHA = (2 * DEPTH) ** 0.25

N_CTX = BATCH * SEQ
N_LAT = DEC_BATCH * DEC_SEQ
N_ROWS = N_CTX + N_LAT
N_GROUPS = 1 + DEC_BATCH
COND_ROWS = 8
LANES = 128
V7X_VMEM_BYTES = 64 * 1024 * 1024

F32 = jnp.float32
BF16 = jnp.bfloat16


def _vmem_limit(nbytes):
    return int(min(nbytes * 3 // 2 + (2 << 20), V7X_VMEM_BYTES - (4 << 20)))


def _group_of_row(r):
    return jnp.where(r < N_CTX, 0, 1 + (r - N_CTX) // DEC_SEQ)


def _mod_spec(layer, sub, kind, grid_rank):
    col = sub * 3 + kind
    return pl.BlockSpec((None, COND_ROWS, D_MODEL), lambda *idx: (layer, 0, col))


def _mod_row(ref, row0):
    return ref[pl.ds(_group_of_row(row0), 1), :]


def _mod_kernel(c_ref, w_ref, b_ref, o_ref):
    s = jax.nn.silu(c_ref[...]).astype(BF16)
    o_ref[...] = jnp.dot(s, w_ref[...].astype(BF16), preferred_element_type=F32) + b_ref[...]


def _modulation(cond, ada_w, ada_b):
    tn = 1024
    n_out = N_SUB * 3 * D_MODEL
    return pl.pallas_call(
        _mod_kernel,
        grid=(DEPTH, n_out // tn),
        in_specs=[
            pl.BlockSpec((COND_ROWS, D_MODEL), lambda l, j: (0, 0)),
            pl.BlockSpec((None, D_MODEL, tn), lambda l, j: (l, 0, j)),
            pl.BlockSpec((None, 1, tn), lambda l, j: (l, 0, j)),
        ],
        out_specs=pl.BlockSpec((None, COND_ROWS, tn), lambda l, j: (l, 0, j)),
        out_shape=jax.ShapeDtypeStruct((DEPTH, COND_ROWS, n_out), F32),
        compiler_params=pltpu.CompilerParams(
            dimension_semantics=("arbitrary", "arbitrary"),
            vmem_limit_bytes=_vmem_limit(2 * D_MODEL * tn * 4 + D_MODEL * tn * 2)),
        name="modulation",
    )(cond, ada_w, ada_b.reshape(DEPTH, 1, n_out))


def _split_specs(tm, width, row_axis, shift=0):
    n_ctx, n_lat = N_CTX // tm, N_LAT // tm

    def ctx_map(*idx):
        return (jnp.clip(idx[row_axis] - shift, 0, n_ctx - 1), 0)

    def lat_map(*idx):
        return (jnp.clip(idx[row_axis] - shift - n_ctx, 0, n_lat - 1), 0)

    return pl.BlockSpec((tm, width), ctx_map), pl.BlockSpec((tm, width), lat_map)


def _premod_kernel(xc_ref, xl_ref, sc_ref, sh_ref, h_ref, *, tm):
    row0 = pl.program_id(0) * tm
    sc = 1.0 + _mod_row(sc_ref, row0)
    sh = _mod_row(sh_ref, row0)

    def emit(x_ref):
        h_ref[...] = (x_ref[...] * sc + sh).astype(BF16)

    pl.when(row0 < N_CTX)(lambda: emit(xc_ref))
    pl.when(row0 >= N_CTX)(lambda: emit(xl_ref))


def _premod(x_ctx, x_lat, mod, layer, sub):
    tm = 512
    return pl.pallas_call(
        functools.partial(_premod_kernel, tm=tm),
        grid=(N_ROWS // tm,),
        in_specs=[*_split_specs(tm, D_MODEL, 0),
                  _mod_spec(layer, sub, 1, 1), _mod_spec(layer, sub, 0, 1)],
        out_specs=pl.BlockSpec((tm, D_MODEL), lambda i: (i, 0)),
        out_shape=jax.ShapeDtypeStruct((N_ROWS, D_MODEL), BF16),
        compiler_params=pltpu.CompilerParams(
            dimension_semantics=("arbitrary",),
            vmem_limit_bytes=_vmem_limit(4 * tm * D_MODEL * 4 + 2 * tm * D_MODEL * 2)),
        name="premod",
    )(x_ctx, x_lat, mod, mod)


def _up_kernel(*refs, tf, ts, nf, kc, tail, extra):
    if extra:
        h_ref, wg_ref, wu_ref, wo_ref, xw_ref, a_ref, wob_ref, xwb_ref, wbf_ref = refs
    else:
        h_ref, wg_ref, wu_ref, wo_ref, a_ref, wob_ref, wbf_ref = refs
    j, i = pl.program_id(0), pl.program_id(1)
    n_sub = tf // ts
    lead = ts - tail

    def stage():
        rows = pl.ds(pl.multiple_of(i * kc, kc), kc)
        for s in range(n_sub):
            p0 = 2 * s * ts
            wbf_ref[j % 2, rows, p0:p0 + ts] = wg_ref[:, s * ts:(s + 1) * ts].astype(BF16)
            if s < n_sub - 1:
                wbf_ref[j % 2, rows, p0 + ts:p0 + 2 * ts] = wu_ref[:, s * ts:(s + 1) * ts].astype(BF16)
            else:
                wbf_ref[j % 2, rows, p0 + ts:p0 + ts + tail] = wu_ref[:, s * ts + lead:(s + 1) * ts].astype(BF16)
                wbf_ref[j % 2, rows, p0 + ts + tail:p0 + 2 * ts] = wu_ref[:, s * ts:s * ts + lead].astype(BF16)
        wob_ref[...] = wo_ref[...].astype(BF16)
        if extra:
            xwb_ref[...] = xw_ref[...].astype(BF16)

    def swiglu(g, u):
        return (jax.nn.silu(g) * u).astype(BF16)

    def multiply():
        for s in range(n_sub):
            p0, c0 = 2 * s * ts, s * ts
            gu = jnp.dot(h_ref[...], wbf_ref[(j + 1) % 2, :, p0:p0 + 2 * ts], preferred_element_type=F32)
            if s < n_sub - 1:
                a_ref[:, c0:c0 + ts] = swiglu(gu[:, :ts], gu[:, ts:])
            else:
                a_ref[:, c0:c0 + lead] = swiglu(gu[:, :lead], gu[:, ts + tail:])
                a_ref[:, c0 + lead:c0 + ts] = swiglu(gu[:, lead:ts], gu[:, ts:ts + tail])

    @pl.when(j == 0)
    def _():
        stage()

    @pl.when(jnp.logical_and(j > 0, j < nf))
    def _():
        stage()
        multiply()

    @pl.when(j == nf)
    def _():
        p0 = 2 * (n_sub - 1) * ts + lead
        gu = jnp.dot(h_ref[...], wbf_ref[(j + 1) % 2, :, p0:p0 + 2 * tail], preferred_element_type=F32)
        a_ref[:, :tail] = swiglu(gu[:, :tail], gu[:, tail:])


def _ffn_up(h, w_in, w_out, layer, half, extra=None):
    tm, tf, ts = 1024, 1024, 512
    nm = N_ROWS // tm
    nf = pl.cdiv(D_FF, tf)
    tail = D_FF - (nf - 1) * tf
    kc = D_MODEL // nm
    dc = LANES
    n_down = D_FF // dc
    assert 0 < tail <= ts
    assert D_FF % dc == 0 and n_down <= nf * nm
    staged = lambda j: jnp.minimum(j, nf - 1)
    krow = lambda j, i: pl.multiple_of(jnp.where(j < nf, i, nm - 1) * kc, kc)
    start = lambda j: pl.multiple_of(jnp.minimum(staged(j) * tf, D_FF - tf), LANES)
    ustart = lambda j: pl.multiple_of(D_FF + jnp.minimum(staged(j) * tf, D_FF - tf), LANES)
    down = lambda j, i: jnp.minimum(j * nm + i, n_down - 1)
    rows = lambda j, i: jnp.where(j > 0, i, 0)
    elem = pl.Element
    in_specs = [
        pl.BlockSpec((tm, D_MODEL), lambda j, i: (rows(j, i), 0)),
        pl.BlockSpec((None, None, elem(kc), elem(tf)), lambda j, i: (layer, half, krow(j, i), start(j))),
        pl.BlockSpec((None, None, elem(kc), elem(tf)), lambda j, i: (layer, half, krow(j, i), ustart(j))),
        pl.BlockSpec((None, None, dc, D_MODEL), lambda j, i: (layer, half, down(j, i), 0)),
    ]
    args = [h, w_in, w_in, w_out]
    out_specs = [pl.BlockSpec((tm, tf), lambda j, i: (rows(j, i), jnp.maximum(j - 1, 0))),
                 pl.BlockSpec((dc, D_MODEL), lambda j, i: (down(j, i), 0))]
    out_shape = [jax.ShapeDtypeStruct((N_ROWS, D_FF), BF16), jax.ShapeDtypeStruct((D_FF, D_MODEL), BF16)]
    ec = 0
    if extra:
        xw, xindex = extra
        xrows = xw.shape[len(xindex)]
        ec = 64
        n_extra = xrows // ec
        assert xrows % ec == 0 and n_extra <= nf * nm and xw.shape[-1] == D_MODEL
        xchunk = lambda j, i: jnp.minimum(j * nm + i, n_extra - 1)
        in_specs.append(pl.BlockSpec((None,) * len(xindex) + (ec, D_MODEL), lambda j, i: xindex + (xchunk(j, i), 0)))
        args.append(xw)
        out_specs.append(pl.BlockSpec((ec, D_MODEL), lambda j, i: (xchunk(j, i), 0)))
        out_shape.append(jax.ShapeDtypeStruct((xrows, D_MODEL), BF16))
    outs = pl.pallas_call(
        functools.partial(_up_kernel, tf=tf, ts=ts, nf=nf, kc=kc, tail=tail, extra=bool(extra)),
        grid=(nf + 1, nm),
        in_specs=in_specs,
        out_specs=out_specs,
        out_shape=out_shape,
        scratch_shapes=[pltpu.VMEM((2, D_MODEL, 2 * tf), BF16)],
        compiler_params=pltpu.CompilerParams(
            dimension_semantics=("arbitrary", "arbitrary"),
            vmem_limit_bytes=_vmem_limit(2 * tm * D_MODEL * 2 + 4 * kc * tf * 4 + 2 * (dc + ec) * D_MODEL * 6
                                         + 4 * D_MODEL * tf * 2 + 2 * tm * tf * 2 + 3 * tm * tf * 4)),
        name="ffn_up",
    )(*args)
    return outs[0], outs[1], (outs[2] if extra else None)


def _deepnorm_kernel(*refs, tm, coef, cast_w, split_lhs, split_x, split_out, emit_h):
    refs = list(refs)
    take = lambda n: [refs.pop(0) for _ in range(n)]
    lhs_refs = take(2 if split_lhs else 1)
    (w_ref,) = take(1)
    x_refs = take(2 if split_x else 1)
    gate_ref, lng_ref, lnb_ref = take(3)
    sc_ref, sh_ref = take(2) if emit_h else (None, None)
    xo_refs = take(2 if split_out else 1)
    (ho_ref,) = take(1) if emit_h else (None,)
    (wbf_ref,) = take(1) if cast_w else (w_ref,)

    if cast_w:
        @pl.when(pl.program_id(0) == 0)
        def _():
            wbf_ref[...] = w_ref[...].astype(BF16)

    row0 = pl.program_id(0) * tm
    gate = coef * _mod_row(gate_ref, row0)
    if emit_h:
        sc = 1.0 + _mod_row(sc_ref, row0)
        sh = _mod_row(sh_ref, row0)

    def body(lhs_ref, x_ref, xo_ref):
        y = jnp.dot(lhs_ref[...], wbf_ref[...], preferred_element_type=F32)
        z = DEEPNORM_ALPHA * x_ref[...] + gate * y
        mu = jnp.mean(z, axis=-1, keepdims=True)
        zc = z - mu
        var = jnp.mean(zc * zc, axis=-1, keepdims=True)
        o = zc * lax.rsqrt(var + EPS) * lng_ref[...] + lnb_ref[...]
        xo_ref[...] = o
        if emit_h:
            ho_ref[...] = (o * sc + sh).astype(BF16)

    if split_lhs or split_x or split_out:
        pl.when(row0 < N_CTX)(lambda: body(lhs_refs[0], x_refs[0], xo_refs[0]))
        pl.when(row0 >= N_CTX)(lambda: body(lhs_refs[-1], x_refs[-1], xo_refs[-1]))
    else:
        body(lhs_refs[0], x_refs[0], xo_refs[0])


def _proj_deepnorm(lhs, w, w_index, x, mod, ln_g, ln_b, layer, sub, coef, next_mod, split_out=False):
    split_lhs, split_x = isinstance(lhs, tuple), isinstance(x, tuple)
    k = (lhs[0] if split_lhs else lhs).shape[1]
    tm = 256
    cast_w = w.dtype != BF16
    emit_h = next_mod is not None
    lead = (None,) * len(w_index)
    row_spec = lambda width: pl.BlockSpec((tm, width), lambda i: (i, 0))
    in_specs, args = [], []
    if split_lhs:
        in_specs += list(_split_specs(tm, k, 0)); args += list(lhs)
    else:
        in_specs.append(row_spec(k)); args.append(lhs)
    in_specs.append(pl.BlockSpec(lead + (k, D_MODEL), lambda i: w_index + (0, 0), pipeline_mode=pl.Buffered(1)))
    args.append(w)
    if split_x:
        in_specs += list(_split_specs(tm, D_MODEL, 0)); args += list(x)
    else:
        in_specs.append(row_spec(D_MODEL)); args.append(x)
    ln_spec = pl.BlockSpec((None, 1, D_MODEL), lambda i: (layer * N_SUB + sub, 0, 0))
    in_specs += [_mod_spec(layer, sub, 2, 1), ln_spec, ln_spec]
    args += [mod, ln_g, ln_b]
    if emit_h:
        in_specs += [_mod_spec(next_mod[0], next_mod[1], 1, 1), _mod_spec(next_mod[0], next_mod[1], 0, 1)]
        args += [mod, mod]
    if split_out:
        out_specs = list(_split_specs(tm, D_MODEL, 0))
        out_shape = [jax.ShapeDtypeStruct((N_CTX, D_MODEL), F32), jax.ShapeDtypeStruct((N_LAT, D_MODEL), F32)]
    else:
        out_specs = [row_spec(D_MODEL)]
        out_shape = [jax.ShapeDtypeStruct((N_ROWS, D_MODEL), F32)]
    if emit_h:
        out_specs.append(row_spec(D_MODEL))
        out_shape.append(jax.ShapeDtypeStruct((N_ROWS, D_MODEL), BF16))
    w_bytes = k * D_MODEL * (6 if cast_w else 2)
    est = (2 * (1 + split_lhs) * tm * k * 2 + w_bytes + 2 * (2 + split_x + split_out) * tm * D_MODEL * 4
           + 2 * tm * D_MODEL * 2 + 4 * tm * D_MODEL * 4)
    outs = pl.pallas_call(
        functools.partial(_deepnorm_kernel, tm=tm, coef=coef, cast_w=cast_w, split_lhs=split_lhs,
                          split_x=split_x, split_out=split_out, emit_h=emit_h),
        grid=(N_ROWS // tm,),
        in_specs=in_specs,
        out_specs=out_specs,
        out_shape=out_shape,
        scratch_shapes=[pltpu.VMEM((k, D_MODEL), BF16)] if cast_w else [],
        compiler_params=pltpu.CompilerParams(
            dimension_semantics=("arbitrary",), vmem_limit_bytes=_vmem_limit(est)),
        name="proj_deepnorm",
    )(*args)
    x_new = (outs[0], outs[1]) if split_out else outs[0]
    return x_new, (outs[-1] if emit_h else None)


def _rope_tables(dim):
    rows = DEC_SEQ // GRID_W
    row = np.repeat(np.arange(rows, dtype=np.float64), GRID_W)
    col = np.tile(np.arange(GRID_W, dtype=np.float64), rows)
    quarter = dim // 4
    inv = ROPE_THETA ** (-np.arange(quarter, dtype=np.float64) / quarter)
    ang = np.concatenate([row[:, None] * inv, col[:, None] * inv], axis=-1)
    cos, sin, zero = np.cos(ang), np.sin(ang), np.zeros_like(ang)
    reps = LANES // dim
    full = lambda a, b: np.tile(np.concatenate([a, b], axis=-1), (1, reps)).astype(np.float32)
    return jnp.asarray(full(cos, cos)), jnp.asarray(full(-sin, zero)), jnp.asarray(full(zero, sin))


DENSE = -1


def _qkv_kernel(*refs, tm, tn, norm, rope_dim, cache_split, out_scale):
    refs = list(refs)
    take = lambda n: [refs.pop(0) for _ in range(n)]
    h_ref, w_ref, gain_ref = take(3)
    cos_ref, slo_ref, shi_ref = take(3) if rope_dim else (None, None, None)
    (o_ref,) = take(1)
    (cache_ref,) = take(1) if cache_split else (None,)
    (wbf_ref,) = take(1)

    @pl.when(pl.program_id(1) == 0)
    def _():
        wbf_ref[...] = w_ref[...].astype(BF16)

    def finish(rope, write_cache):
        for r in range(tm // SEQ):
            rs = slice(r * SEQ, (r + 1) * SEQ)
            y = jnp.dot(h_ref[rs, :], wbf_ref[...], preferred_element_type=F32)
            pieces = []
            for c in range(tn // LANES):
                yc = y[:, c * LANES:(c + 1) * LANES]
                if norm:
                    yc = yc * lax.rsqrt(jnp.mean(yc * yc, axis=-1, keepdims=True) + EPS) * gain_ref[...]
                if rope:
                    half = rope_dim // 2
                    yc = (yc * cos_ref[rs, :]
                          + pltpu.roll(yc, LANES - half, 1) * slo_ref[rs, :]
                          + pltpu.roll(yc, half, 1) * shi_ref[rs, :])
                if out_scale != 1.0:
                    yc = yc * out_scale
                o_ref[rs, c * LANES:(c + 1) * LANES] = yc.astype(o_ref.dtype)
                pieces.append(yc)
                if write_cache and cache_split == DENSE:
                    cache_ref[rs, c * LANES:(c + 1) * LANES] = yc
                elif write_cache and cache_split == 1:
                    cache_ref[r, :, c, :] = yc
            if write_cache and cache_split > 1:
                full = jnp.concatenate(pieces, axis=1)
                cache_ref[r] = full.reshape(SEQ, tn // LANES, cache_split, LANES // cache_split)

    if rope_dim or cache_split:
        is_latent = pl.program_id(1) * tm >= N_CTX
        pl.when(jnp.logical_not(is_latent))(lambda: finish(False, bool(cache_split)))
        pl.when(is_latent)(lambda: finish(bool(rope_dim), False))
    else:
        finish(False, False)


def _qkv_proj(h, w, layer_j, col0, width, gain, tables, norm, rope_dim, out_scale=1.0, cache_split=0):
    tm = 1024
    tn = 1024 if width % 1024 == 0 else 512
    n0 = col0 // tn
    in_specs = [
        pl.BlockSpec((tm, D_MODEL), lambda j, i: (i, 0)),
        pl.BlockSpec((None, D_MODEL, tn), lambda j, i: (layer_j, 0, n0 + j)),
        pl.BlockSpec((1, LANES), lambda j, i: (0, 0)),
    ]
    args = [h, w, gain]
    if rope_dim:
        def tab_map(j, i):
            r = i * tm
            return (jnp.where(r >= N_CTX, ((r - N_CTX) % DEC_SEQ) // tm, 0), 0)
        in_specs += [pl.BlockSpec((tm, LANES), tab_map)] * 3
        args += list(tables)
    out_specs = [pl.BlockSpec((tm, tn), lambda j, i: (i, j))]
    out_shape = [jax.ShapeDtypeStruct((N_ROWS, width), BF16)]
    n_ctx = N_CTX // tm
    if cache_split == DENSE:
        out_specs.append(pl.BlockSpec((tm, tn), lambda j, i: (jnp.minimum(i, n_ctx - 1), j)))
        out_shape.append(jax.ShapeDtypeStruct((N_CTX, width), F32))
    elif cache_split:
        tail = (LANES,) if cache_split == 1 else (cache_split, LANES // cache_split)
        zeros = (0,) * len(tail)
        out_specs.append(pl.BlockSpec((tm // SEQ, None, SEQ, tn // LANES) + tail,
                                      lambda j, i: (jnp.minimum(i, n_ctx - 1), 0, 0, j) + zeros))
        out_shape.append(jax.ShapeDtypeStruct((BATCH, 1, SEQ, width // LANES) + tail, F32))
    outs = pl.pallas_call(
        functools.partial(_qkv_kernel, tm=tm, tn=tn, norm=norm, rope_dim=rope_dim,
                          cache_split=cache_split, out_scale=out_scale),
        grid=(width // tn, N_ROWS // tm),
        in_specs=in_specs,
        out_specs=out_specs,
        out_shape=out_shape,
        scratch_shapes=[pltpu.VMEM((D_MODEL, tn), BF16)],
        compiler_params=pltpu.CompilerParams(
            dimension_semantics=("arbitrary", "arbitrary"),
            vmem_limit_bytes=_vmem_limit(2 * tm * D_MODEL * 2 + 2 * D_MODEL * tn * 4 + D_MODEL * tn * 2
                                         + 2 * tm * tn * 2 + 2 * abs(cache_split) * tm * tn * 4
                                         + 6 * tm * LANES * 4 + 3 * tm * tn * 4)),
        name="qkv_proj",
    )(*args)
    return (outs[0], outs[1]) if cache_split else outs[0]


_NT = (((1,), (1,)), ((), ()))


LOG2_E = math.log2(math.e)


def _with_ones(v):
    v = v.astype(BF16)
    return jnp.concatenate([v, jnp.ones_like(v)], axis=1)


def _softmax_pv(q, keys, vals1):
    logits = [lax.dot_general(q, k, _NT, preferred_element_type=F32) for k in keys]
    m = functools.reduce(jnp.maximum, [jnp.max(s, axis=-1, keepdims=True) for s in logits])
    acc = None
    for s, v1 in zip(logits, vals1):
        pv = jnp.dot(jnp.exp2(s - m).astype(BF16), v1, preferred_element_type=F32)
        acc = pv if acc is None else acc + pv
    return acc[:, :LANES] / acc[:, LANES:]


def _attn_a_kernel(*refs, kv_heads, has_cache):
    if has_cache:
        q_ref, kc_ref, vc_ref, kn_ref, vn_ref, o_ref = refs
    else:
        q_ref, kn_ref, vn_ref, o_ref = refs
    for kv in range(kv_heads):
        ks = slice(kv * LANES, (kv + 1) * LANES)
        keys = [kn_ref[:, ks].astype(BF16)]
        vals1 = [_with_ones(vn_ref[:, ks])]
        if has_cache:
            keys.append(kc_ref[:, ks].astype(BF16))
            vals1.append(_with_ones(vc_ref[:, ks]))
        for g in range(A_REP):
            sl = slice((kv * A_REP + g) * LANES, (kv * A_REP + g + 1) * LANES)
            o_ref[:, sl] = _softmax_pv(q_ref[:, sl], keys, vals1).astype(o_ref.dtype)


def _attn_a(q, k, v, cache_k, cache_v):
    tq = SEQ
    kvs = A_KV_HEADS
    kw, gw = kvs * LANES, kvs * A_REP * LANES
    o_ctx = pl.pallas_call(
        functools.partial(_attn_a_kernel, kv_heads=kvs, has_cache=False),
        grid=(BATCH, A_KV_HEADS // kvs),
        in_specs=[
            pl.BlockSpec((tq, gw), lambda b, h: (b, h)),
            pl.BlockSpec((SEQ, kw), lambda b, h: (b, h)),
            pl.BlockSpec((SEQ, kw), lambda b, h: (b, h)),
        ],
        out_specs=pl.BlockSpec((tq, gw), lambda b, h: (b, h)),
        out_shape=jax.ShapeDtypeStruct((N_CTX, D_MODEL), BF16),
        compiler_params=pltpu.CompilerParams(
            dimension_semantics=("arbitrary", "arbitrary"),
            vmem_limit_bytes=_vmem_limit(4 * tq * gw * 2 + 4 * SEQ * kw * 2 + 4 * kvs * A_REP * tq * SEQ * 4)),
        name="attn_a_ctx",
    )(q, k, v)
    tq = 512
    kvs = A_KV_HEADS
    kw, gw = kvs * LANES, kvs * A_REP * LANES
    nq = DEC_SEQ // tq
    row0 = N_CTX // tq
    t_all = PAST_LEN + DEC_SEQ
    o_lat = pl.pallas_call(
        functools.partial(_attn_a_kernel, kv_heads=kvs, has_cache=True),
        grid=(DEC_BATCH, A_KV_HEADS // kvs, nq),
        in_specs=[
            pl.BlockSpec((tq, gw), lambda b, h, i: (row0 + b * nq + i, h)),
            pl.BlockSpec((None, PAST_LEN, kw), lambda b, h, i: (b, 0, h)),
            pl.BlockSpec((None, PAST_LEN, kw), lambda b, h, i: (b, 0, h)),
            pl.BlockSpec((DEC_SEQ, kw), lambda b, h, i: (N_CTX // DEC_SEQ + b, h)),
            pl.BlockSpec((DEC_SEQ, kw), lambda b, h, i: (N_CTX // DEC_SEQ + b, h)),
        ],
        out_specs=pl.BlockSpec((tq, gw), lambda b, h, i: (b * nq + i, h)),
        out_shape=jax.ShapeDtypeStruct((N_LAT, D_MODEL), BF16),
        compiler_params=pltpu.CompilerParams(
            dimension_semantics=("arbitrary", "arbitrary", "arbitrary"),
            vmem_limit_bytes=_vmem_limit(4 * tq * gw * 2 + 4 * t_all * kw * 4
                                         + 2 * kvs * A_REP * tq * t_all * 4)),
        name="attn_a_lat",
    )(q, cache_k, cache_v, k, v)
    return o_ctx, o_lat


def _attn_b_kernel(*refs, heads, has_cache, lam_init):
    if has_cache:
        q_ref, kc_ref, vc_ref, kn_ref, vn_ref, lam_ref, sub_ref, o_ref = refs
    else:
        q_ref, kn_ref, vn_ref, lam_ref, sub_ref, o_ref = refs
    lp = lam_ref[...]
    lam = (jnp.exp(jnp.sum(lp[0:1] * lp[1:2], axis=-1, keepdims=True))
           - jnp.exp(jnp.sum(lp[2:3] * lp[3:4], axis=-1, keepdims=True)) + lam_init)
    first = lax.broadcasted_iota(jnp.int32, (1, LANES), 1) < B_QK_DIM
    for h in range(heads):
        sl = slice(h * LANES, (h + 1) * LANES)
        q = q_ref[:, sl]
        zero = jnp.zeros_like(q)
        keys = [kn_ref[:, sl].astype(BF16)]
        vals1 = [_with_ones(vn_ref[:, sl])]
        if has_cache:
            keys.append(kc_ref[:, sl].astype(BF16))
            vals1.append(_with_ones(vc_ref[:, sl]))
        o1 = _softmax_pv(jnp.where(first, q, zero), keys, vals1)
        o2 = _softmax_pv(jnp.where(first, zero, q), keys, vals1)
        o = o1 - lam * o2
        o = o * lax.rsqrt(jnp.mean(o * o, axis=-1, keepdims=True) + EPS) * sub_ref[...] * (1.0 - lam_init)
        o_ref[:, sl] = o.astype(o_ref.dtype)


def _attn_b(q, k, v, cache_k, cache_v, lam_p, subln, lam_init):
    heads = 8
    gw = heads * LANES
    tq = SEQ
    o_ctx = pl.pallas_call(
        functools.partial(_attn_b_kernel, heads=heads, has_cache=False, lam_init=lam_init),
        grid=(BATCH, B_HEADS // heads),
        in_specs=[
            pl.BlockSpec((tq, gw), lambda b, h: (b, h)),
            pl.BlockSpec((SEQ, gw), lambda b, h: (b, h)),
            pl.BlockSpec((SEQ, gw), lambda b, h: (b, h)),
            pl.BlockSpec((4, B_QK_DIM), lambda b, h: (0, 0)),
            pl.BlockSpec((1, LANES), lambda b, h: (0, 0)),
        ],
        out_specs=pl.BlockSpec((tq, gw), lambda b, h: (b, h)),
        out_shape=jax.ShapeDtypeStruct((N_CTX, D_MODEL), BF16),
        compiler_params=pltpu.CompilerParams(
            dimension_semantics=("arbitrary", "arbitrary"),
            vmem_limit_bytes=_vmem_limit(4 * tq * gw * 2 + 4 * SEQ * gw * 4 + 8 * heads * tq * SEQ * 4)),
        name="attn_b_ctx",
    )(q, k, v, lam_p, subln)
    tq = 512
    nq = DEC_SEQ // tq
    row0 = N_CTX // tq
    t_all = PAST_LEN + DEC_SEQ
    heads = 8
    gw = heads * LANES
    o_lat = pl.pallas_call(
        functools.partial(_attn_b_kernel, heads=heads, has_cache=True, lam_init=lam_init),
        grid=(DEC_BATCH, B_HEADS // heads, nq),
        in_specs=[
            pl.BlockSpec((tq, gw), lambda b, h, i: (row0 + b * nq + i, h)),
            pl.BlockSpec((None, PAST_LEN, gw), lambda b, h, i: (b, 0, h)),
            pl.BlockSpec((None, PAST_LEN, gw), lambda b, h, i: (b, 0, h)),
            pl.BlockSpec((DEC_SEQ, gw), lambda b, h, i: (N_CTX // DEC_SEQ + b, h)),
            pl.BlockSpec((DEC_SEQ, gw), lambda b, h, i: (N_CTX // DEC_SEQ + b, h)),
            pl.BlockSpec((4, B_QK_DIM), lambda b, h, i: (0, 0)),
            pl.BlockSpec((1, LANES), lambda b, h, i: (0, 0)),
        ],
        out_specs=pl.BlockSpec((tq, gw), lambda b, h, i: (b * nq + i, h)),
        out_shape=jax.ShapeDtypeStruct((N_LAT, D_MODEL), BF16),
        compiler_params=pltpu.CompilerParams(
            dimension_semantics=("arbitrary", "arbitrary", "arbitrary"),
            vmem_limit_bytes=_vmem_limit(4 * tq * gw * 2 + 4 * t_all * gw * 4
                                         + 3 * 2 * heads * tq * t_all * 4)),
        name="attn_b_lat",
    )(q, cache_k, cache_v, k, v, lam_p, subln)
    return o_ctx, o_lat


def _diff_lambda_init(layer_idx):
    return 0.8 - 0.6 * math.exp(-0.3 * layer_idx)


def kernel(x_prompt, x_sample, cache_a_k, cache_a_v, cache_b_k, cache_b_v, c, c_ctx, ada_w, ada_b, ln_g, ln_b, ffn_w_in, ffn_w_out, a_w_qkv, a_q_norm, a_k_norm, a_w_o, b_w_qkv, b_lambda, b_subln, b_w_o):
    ln_g3 = ln_g.reshape(DEPTH * N_SUB, 1, D_MODEL)
    ln_b3 = ln_b.reshape(DEPTH * N_SUB, 1, D_MODEL)
    tables_a = _rope_tables(A_HEAD_DIM)
    tables_b = _rope_tables(B_QK_DIM)
    ones = jnp.ones((1, LANES), F32)

    cond = jnp.concatenate([c_ctx[None, :], c, jnp.zeros((COND_ROWS - N_GROUPS, D_MODEL), F32)], axis=0)
    mod = _modulation(cond, ada_w, ada_b)

    x = (x_prompt.reshape(N_CTX, D_MODEL), x_sample.reshape(N_LAT, D_MODEL))
    h = _premod(x[0], x[1], mod, 0, 0)

    new_kv = {"ak": [], "av": [], "bk": [], "bv": []}
    for i in range(DEPTH):
        j = i // N_MIXERS
        last = i + 1 == DEPTH
        w_o_f32 = a_w_o if i % N_MIXERS == 0 else b_w_o
        a, w_down, w_o = _ffn_up(h, ffn_w_in, ffn_w_out, i, 0, extra=(w_o_f32, (j,)))
        x, h = _proj_deepnorm(a, w_down, (), x, mod, ln_g3, ln_b3, i, 0, MACARON_WEIGHT, (i, 1))
        if i % N_MIXERS == 0:
            qw = A_HEADS * A_HEAD_DIM
            kw = A_KV_HEADS * A_HEAD_DIM
            q = _qkv_proj(h, a_w_qkv, j, 0, qw, a_q_norm[j][None, :], tables_a, True, A_HEAD_DIM,
                          out_scale=A_HEAD_DIM ** -0.5 * LOG2_E)
            k, new_k = _qkv_proj(h, a_w_qkv, j, qw, kw, a_k_norm[j][None, :], tables_a, True, A_HEAD_DIM,
                                 cache_split=1)
            v, new_v = _qkv_proj(h, a_w_qkv, j, qw + kw, kw, ones, None, False, 0, cache_split=1)
            o = _attn_a(q, k, v, cache_a_k[:, j].reshape(DEC_BATCH, PAST_LEN, kw),
                        cache_a_v[:, j].reshape(DEC_BATCH, PAST_LEN, kw))
            new_kv["ak"].append(new_k)
            new_kv["av"].append(new_v)
        else:
            lam_init = _diff_lambda_init(i)
            qw = B_HEADS * 2 * B_QK_DIM
            q = _qkv_proj(h, b_w_qkv, j, 0, qw, ones, tables_b, False, B_QK_DIM,
                          out_scale=B_QK_DIM ** -0.5 * LOG2_E)
            k, new_k = _qkv_proj(h, b_w_qkv, j, qw, qw, ones, tables_b, False, B_QK_DIM, cache_split=2)
            v, new_v = _qkv_proj(h, b_w_qkv, j, 2 * qw, B_HEADS * B_V_DIM, ones, None, False, 0,
                                 cache_split=DENSE)
            new_v = new_v.reshape(BATCH, 1, SEQ, B_HEADS, B_V_DIM)
            o = _attn_b(q, k, v, cache_b_k[:, j].reshape(DEC_BATCH, PAST_LEN, qw),
                        cache_b_v[:, j].reshape(DEC_BATCH, PAST_LEN, B_HEADS * B_V_DIM),
                        b_lambda[j], b_subln[j][None, :], lam_init)
            new_kv["bk"].append(new_k)
            new_kv["bv"].append(new_v)
        x, h = _proj_deepnorm(o, w_o, (), x, mod, ln_g3, ln_b3, i, 1, 1.0, (i, 2))
        a, w_down, _ = _ffn_up(h, ffn_w_in, ffn_w_out, i, 1)
        x, h = _proj_deepnorm(a, w_down, (), x, mod, ln_g3, ln_b3, i, 2, MACARON_WEIGHT,
                              None if last else (i + 1, 0), split_out=last)

    y_prompt = x[0].reshape(BATCH, SEQ, D_MODEL)
    y_sample = x[1].reshape(DEC_BATCH, DEC_SEQ, D_MODEL)
    return (y_prompt, y_sample,
            jnp.concatenate(new_kv["ak"], axis=1), jnp.concatenate(new_kv["av"], axis=1),
            jnp.concatenate(new_kv["bk"], axis=1), jnp.concatenate(new_kv["bv"], axis=1))
```

```python
import functools
import math

import jax
import jax.numpy as jnp
import numpy as np
from jax import lax
from jax.experimental import pallas as pl
from jax.experimental.pallas import tpu as pltpu

D_MODEL = 2048
BATCH = 16
SEQ = 256
DEPTH = 2
DEC_BATCH = 2
DEC_SEQ = 2048
PAST_LEN = 256
GRID_W = 64
N_MIXERS = 2
N_SUB = 3
A_HEAD_DIM = 128
A_HEADS = D_MODEL // A_HEAD_DIM
A_KV_HEADS = 4
A_REP = A_HEADS // A_KV_HEADS
B_QK_DIM = 64
B_V_DIM = 2 * B_QK_DIM
B_HEADS = D_MODEL // B_V_DIM
D_FF = ((8 * D_MODEL // 3 + 127) // 128) * 128
ROPE_THETA = 10000.0
EPS = 1e-6
MACARON_WEIGHT = 0.5
DEEPNORM_ALPHA = (2 * DEPTH) ** 0.25

N_CTX = BATCH * SEQ
N_LAT = DEC_BATCH * DEC_SEQ
N_ROWS = N_CTX + N_LAT
N_GROUPS = 1 + DEC_BATCH
COND_ROWS = 8
LANES = 128
V7X_VMEM_BYTES = 64 * 1024 * 1024

F32 = jnp.float32
BF16 = jnp.bfloat16


def _vmem_limit(nbytes):
    return int(min(nbytes * 3 // 2 + (2 << 20), V7X_VMEM_BYTES - (4 << 20)))


def _group_of_row(r):
    return jnp.where(r < N_CTX, 0, 1 + (r - N_CTX) // DEC_SEQ)


def _mod_spec(sub, kind):
    col = sub * 3 + kind
    return pl.BlockSpec((COND_ROWS, D_MODEL), lambda *idx: (0, col))


def _mod_row(ref, row0):
    return ref[pl.ds(_group_of_row(row0), 1), :]


def _mod_kernel(c_ref, w_ref, b_ref, o_ref):
    s = jax.nn.silu(c_ref[...]).astype(BF16)
    o_ref[...] = jnp.dot(s, w_ref[...].astype(BF16), preferred_element_type=F32) + b_ref[...]


def _modulation(cond, ada_w, ada_b, layers):
    tn = 1024
    n_out = N_SUB * 3 * D_MODEL
    return pl.pallas_call(
        _mod_kernel,
        grid=(layers, n_out // tn),
        in_specs=[
            pl.BlockSpec((COND_ROWS, D_MODEL), lambda l, j: (0, 0)),
            pl.BlockSpec((None, D_MODEL, tn), lambda l, j: (l, 0, j)),
            pl.BlockSpec((None, 1, tn), lambda l, j: (l, 0, j)),
        ],
        out_specs=pl.BlockSpec((None, COND_ROWS, tn), lambda l, j: (l, 0, j)),
        out_shape=jax.ShapeDtypeStruct((layers, COND_ROWS, n_out), F32),
        compiler_params=pltpu.CompilerParams(
            dimension_semantics=("arbitrary", "arbitrary"),
            vmem_limit_bytes=_vmem_limit(2 * D_MODEL * tn * 4 + D_MODEL * tn * 2)),
        name="modulation",
    )(cond, ada_w, ada_b.reshape(DEPTH, 1, n_out))


def _split_specs(tm, width, row_axis, shift=0):
    n_ctx, n_lat = N_CTX // tm, N_LAT // tm

    def ctx_map(*idx):
        return (jnp.clip(idx[row_axis] - shift, 0, n_ctx - 1), 0)

    def lat_map(*idx):
        return (jnp.clip(idx[row_axis] - shift - n_ctx, 0, n_lat - 1), 0)

    return pl.BlockSpec((tm, width), ctx_map), pl.BlockSpec((tm, width), lat_map)


def _premod_kernel(xc_ref, xl_ref, sc_ref, sh_ref, h_ref, *, tm):
    row0 = pl.program_id(0) * tm
    sc = 1.0 + _mod_row(sc_ref, row0)
    sh = _mod_row(sh_ref, row0)

    def emit(x_ref):
        h_ref[...] = (x_ref[...] * sc + sh).astype(BF16)

    pl.when(row0 < N_CTX)(lambda: emit(xc_ref))
    pl.when(row0 >= N_CTX)(lambda: emit(xl_ref))


def _premod(x_ctx, x_lat, mod, layer, sub):
    tm = 512
    return pl.pallas_call(
        functools.partial(_premod_kernel, tm=tm),
        grid=(N_ROWS // tm,),
        in_specs=[*_split_specs(tm, D_MODEL, 0),
                  _mod_spec(sub, 1), _mod_spec(sub, 0)],
        out_specs=pl.BlockSpec((tm, D_MODEL), lambda i: (i, 0)),
        out_shape=jax.ShapeDtypeStruct((N_ROWS, D_MODEL), BF16),
        compiler_params=pltpu.CompilerParams(
            dimension_semantics=("arbitrary",),
            vmem_limit_bytes=_vmem_limit(4 * tm * D_MODEL * 4 + 2 * tm * D_MODEL * 2)),
        name="premod",
    )(x_ctx, x_lat, mod[layer], mod[layer])


def _up_kernel(*refs, tf, ts, nf, kc, tail, extra, side_mod):
    refs = list(refs)
    take = lambda n: [refs.pop(0) for _ in range(n)]
    h_ref, wg_ref, wu_ref, wo_ref = take(4)
    (xw_ref,) = take(1) if extra else (None,)
    ct_ref, aw_ref, ab_ref = take(3) if side_mod else (None, None, None)
    a_ref, wob_ref = take(2)
    (xwb_ref,) = take(1) if extra else (None,)
    (mo_ref,) = take(1) if side_mod else (None,)
    (wbf_ref,) = take(1)
    j, i = pl.program_id(0), pl.program_id(1)
    n_sub = tf // ts
    lead = ts - tail

    def stage():
        rows = pl.ds(pl.multiple_of(i * kc, kc), kc)
        for s in range(n_sub):
            p0 = 2 * s * ts
            wbf_ref[j % 2, rows, p0:p0 + ts] = wg_ref[:, s * ts:(s + 1) * ts].astype(BF16)
            if s < n_sub - 1:
                wbf_ref[j % 2, rows, p0 + ts:p0 + 2 * ts] = wu_ref[:, s * ts:(s + 1) * ts].astype(BF16)
            else:
                wbf_ref[j % 2, rows, p0 + ts:p0 + ts + tail] = wu_ref[:, s * ts + lead:(s + 1) * ts].astype(BF16)
                wbf_ref[j % 2, rows, p0 + ts + tail:p0 + 2 * ts] = wu_ref[:, s * ts:s * ts + lead].astype(BF16)
        wob_ref[...] = wo_ref[...].astype(BF16)
        if extra:
            xwb_ref[...] = xw_ref[...].astype(BF16)

    def modulate():
        if side_mod:
            w = aw_ref[...]
            s_t = jax.nn.silu(ct_ref[...])
            rows = [jnp.sum(w * s_t[:, r:r + 1], axis=0, keepdims=True) for r in range(N_GROUPS)]
            rows += [jnp.zeros_like(rows[0])] * (COND_ROWS - N_GROUPS)
            mo_ref[...] = jnp.concatenate(rows, axis=0) + ab_ref[...]

    def swiglu(g, u):
        return (jax.nn.silu(g) * u).astype(BF16)

    def multiply():
        for s in range(n_sub):
            p0, c0 = 2 * s * ts, s * ts
            gu = jnp.dot(h_ref[...], wbf_ref[(j + 1) % 2, :, p0:p0 + 2 * ts], preferred_element_type=F32)
            if s < n_sub - 1:
                a_ref[:, c0:c0 + ts] = swiglu(gu[:, :ts], gu[:, ts:])
            else:
                a_ref[:, c0:c0 + lead] = swiglu(gu[:, :lead], gu[:, ts + tail:])
                a_ref[:, c0 + lead:c0 + ts] = swiglu(gu[:, lead:ts], gu[:, ts:ts + tail])

    @pl.when(j == 0)
    def _():
        stage()

    @pl.when(jnp.logical_and(j > 0, j < nf))
    def _():
        stage()
        modulate()
        multiply()

    @pl.when(j == nf)
    def _():
        modulate()
        p0 = 2 * (n_sub - 1) * ts + lead
        gu = jnp.dot(h_ref[...], wbf_ref[(j + 1) % 2, :, p0:p0 + 2 * tail], preferred_element_type=F32)
        a_ref[:, :tail] = swiglu(gu[:, :tail], gu[:, tail:])


def _ffn_up(h, w_in, w_out, layer, half, extra=None, side_mod=None):
    tm, tf, ts = 1024, 1024, 512
    nm = N_ROWS // tm
    nf = pl.cdiv(D_FF, tf)
    tail = D_FF - (nf - 1) * tf
    kc = D_MODEL // nm
    dc = LANES
    n_down = D_FF // dc
    assert 0 < tail <= ts
    assert D_FF % dc == 0 and n_down <= nf * nm
    staged = lambda j: jnp.minimum(j, nf - 1)
    krow = lambda j, i: pl.multiple_of(jnp.where(j < nf, i, nm - 1) * kc, kc)
    start = lambda j: pl.multiple_of(jnp.minimum(staged(j) * tf, D_FF - tf), LANES)
    ustart = lambda j: pl.multiple_of(D_FF + jnp.minimum(staged(j) * tf, D_FF - tf), LANES)
    down = lambda j, i: jnp.minimum(j * nm + i, n_down - 1)
    rows = lambda j, i: jnp.where(j > 0, i, 0)
    elem = pl.Element
    in_specs = [
        pl.BlockSpec((tm, D_MODEL), lambda j, i: (rows(j, i), 0)),
        pl.BlockSpec((None, None, elem(kc), elem(tf)), lambda j, i: (layer, half, krow(j, i), start(j))),
        pl.BlockSpec((None, None, elem(kc), elem(tf)), lambda j, i: (layer, half, krow(j, i), ustart(j))),
        pl.BlockSpec((None, None, dc, D_MODEL), lambda j, i: (layer, half, down(j, i), 0)),
    ]
    args = [h, w_in, w_in, w_out]
    out_specs = [pl.BlockSpec((tm, tf), lambda j, i: (rows(j, i), jnp.maximum(j - 1, 0))),
                 pl.BlockSpec((dc, D_MODEL), lambda j, i: (down(j, i), 0))]
    out_shape = [jax.ShapeDtypeStruct((N_ROWS, D_FF), BF16), jax.ShapeDtypeStruct((D_FF, D_MODEL), BF16)]
    ec = 0
    if extra:
        xw, xindex = extra
        xrows = xw.shape[len(xindex)]
        ec = 64
        n_extra = xrows // ec
        assert xrows % ec == 0 and n_extra <= nf * nm and xw.shape[-1] == D_MODEL
        xchunk = lambda j, i: jnp.minimum(j * nm + i, n_extra - 1)
        in_specs.append(pl.BlockSpec((None,) * len(xindex) + (ec, D_MODEL), lambda j, i: xindex + (xchunk(j, i), 0)))
        args.append(xw)
        out_specs.append(pl.BlockSpec((ec, D_MODEL), lambda j, i: (xchunk(j, i), 0)))
        out_shape.append(jax.ShapeDtypeStruct((xrows, D_MODEL), BF16))
    mc = 0
    if side_mod:
        cond_t, ada_w, ada_b, mod_layer = side_mod
        n_out = ada_w.shape[-1]
        mc = n_out // (nf * nm)
        assert n_out % (nf * nm) == 0 and mc % LANES == 0
        mchunk = lambda j, i: jnp.where(j > 0, (j - 1) * nm + i, 0)
        in_specs += [pl.BlockSpec((D_MODEL, COND_ROWS), lambda j, i: (0, 0)),
                     pl.BlockSpec((None, D_MODEL, mc), lambda j, i: (mod_layer, 0, mchunk(j, i))),
                     pl.BlockSpec((None, 1, mc), lambda j, i: (mod_layer, 0, mchunk(j, i)))]
        args += [cond_t, ada_w, ada_b.reshape(ada_b.shape[0], 1, n_out)]
        out_specs.append(pl.BlockSpec((COND_ROWS, mc), lambda j, i: (0, mchunk(j, i))))
        out_shape.append(jax.ShapeDtypeStruct((COND_ROWS, n_out), F32))
    outs = pl.pallas_call(
        functools.partial(_up_kernel, tf=tf, ts=ts, nf=nf, kc=kc, tail=tail, extra=bool(extra),
                          side_mod=bool(side_mod)),
        grid=(nf + 1, nm),
        in_specs=in_specs,
        out_specs=out_specs,
        out_shape=out_shape,
        scratch_shapes=[pltpu.VMEM((2, D_MODEL, 2 * tf), BF16)],
        compiler_params=pltpu.CompilerParams(
            dimension_semantics=("arbitrary", "arbitrary"),
            vmem_limit_bytes=_vmem_limit(2 * tm * D_MODEL * 2 + 4 * kc * tf * 4 + 2 * (dc + ec) * D_MODEL * 6 + 3 * D_MODEL * mc * 4
                                         + 4 * D_MODEL * tf * 2 + 2 * tm * tf * 2 + 3 * tm * tf * 4)),
        name="ffn_up",
    )(*args)
    return outs[0], outs[1], (outs[2] if extra else None), (outs[-1] if side_mod else None)


def _deepnorm_kernel(*refs, tm, coef, cast_w, split_lhs, split_x, split_out, emit_h):
    refs = list(refs)
    take = lambda n: [refs.pop(0) for _ in range(n)]
    lhs_refs = take(2 if split_lhs else 1)
    (w_ref,) = take(1)
    x_refs = take(2 if split_x else 1)
    gate_ref, lng_ref, lnb_ref = take(3)
    sc_ref, sh_ref = take(2) if emit_h else (None, None)
    xo_refs = take(2 if split_out else 1)
    (ho_ref,) = take(1) if emit_h else (None,)
    (wbf_ref,) = take(1) if cast_w else (w_ref,)

    if cast_w:
        @pl.when(pl.program_id(0) == 0)
        def _():
            wbf_ref[...] = w_ref[...].astype(BF16)

    row0 = pl.program_id(0) * tm
    gate = coef * _mod_row(gate_ref, row0)
    if emit_h:
        sc = 1.0 + _mod_row(sc_ref, row0)
        sh = _mod_row(sh_ref, row0)

    def body(lhs_ref, x_ref, xo_ref):
        y = jnp.dot(lhs_ref[...], wbf_ref[...], preferred_element_type=F32)
        z = DEEPNORM_ALPHA * x_ref[...] + gate * y
        mu = jnp.mean(z, axis=-1, keepdims=True)
        zc = z - mu
        var = jnp.mean(zc * zc, axis=-1, keepdims=True)
        o = zc * lax.rsqrt(var + EPS) * lng_ref[...] + lnb_ref[...]
        xo_ref[...] = o
        if emit_h:
            ho_ref[...] = (o * sc + sh).astype(BF16)

    if split_lhs or split_x or split_out:
        pl.when(row0 < N_CTX)(lambda: body(lhs_refs[0], x_refs[0], xo_refs[0]))
        pl.when(row0 >= N_CTX)(lambda: body(lhs_refs[-1], x_refs[-1], xo_refs[-1]))
    else:
        body(lhs_refs[0], x_refs[0], xo_refs[0])


def _proj_deepnorm(lhs, w, w_index, x, mod, ln_g, ln_b, layer, sub, coef, next_mod, split_out=False):
    split_lhs, split_x = isinstance(lhs, tuple), isinstance(x, tuple)
    k = (lhs[0] if split_lhs else lhs).shape[1]
    tm = 256
    cast_w = w.dtype != BF16
    emit_h = next_mod is not None
    lead = (None,) * len(w_index)
    row_spec = lambda width: pl.BlockSpec((tm, width), lambda i: (i, 0))
    in_specs, args = [], []
    if split_lhs:
        in_specs += list(_split_specs(tm, k, 0)); args += list(lhs)
    else:
        in_specs.append(row_spec(k)); args.append(lhs)
    in_specs.append(pl.BlockSpec(lead + (k, D_MODEL), lambda i: w_index + (0, 0), pipeline_mode=pl.Buffered(1)))
    args.append(w)
    if split_x:
        in_specs += list(_split_specs(tm, D_MODEL, 0)); args += list(x)
    else:
        in_specs.append(row_spec(D_MODEL)); args.append(x)
    ln_spec = pl.BlockSpec((None, 1, D_MODEL), lambda i: (layer * N_SUB + sub, 0, 0))
    in_specs += [_mod_spec(sub, 2), ln_spec, ln_spec]
    args += [mod[layer], ln_g, ln_b]
    if emit_h:
        in_specs += [_mod_spec(next_mod[1], 1), _mod_spec(next_mod[1], 0)]
        args += [mod[next_mod[0]], mod[next_mod[0]]]
    if split_out:
        out_specs = list(_split_specs(tm, D_MODEL, 0))
        out_shape = [jax.ShapeDtypeStruct((N_CTX, D_MODEL), F32), jax.ShapeDtypeStruct((N_LAT, D_MODEL), F32)]
    else:
        out_specs = [row_spec(D_MODEL)]
        out_shape = [jax.ShapeDtypeStruct((N_ROWS, D_MODEL), F32)]
    if emit_h:
        out_specs.append(row_spec(D_MODEL))
        out_shape.append(jax.ShapeDtypeStruct((N_ROWS, D_MODEL), BF16))
    w_bytes = k * D_MODEL * (6 if cast_w else 2)
    est = (2 * (1 + split_lhs) * tm * k * 2 + w_bytes + 2 * (2 + split_x + split_out) * tm * D_MODEL * 4
           + 2 * tm * D_MODEL * 2 + 4 * tm * D_MODEL * 4)
    outs = pl.pallas_call(
        functools.partial(_deepnorm_kernel, tm=tm, coef=coef, cast_w=cast_w, split_lhs=split_lhs,
                          split_x=split_x, split_out=split_out, emit_h=emit_h),
        grid=(N_ROWS // tm,),
        in_specs=in_specs,
        out_specs=out_specs,
        out_shape=out_shape,
        scratch_shapes=[pltpu.VMEM((k, D_MODEL), BF16)] if cast_w else [],
        compiler_params=pltpu.CompilerParams(
            dimension_semantics=("arbitrary",), vmem_limit_bytes=_vmem_limit(est)),
        name="proj_deepnorm",
    )(*args)
    x_new = (outs[0], outs[1]) if split_out else outs[0]
    return x_new, (outs[-1] if emit_h else None)


def _rope_tables(dim):
    rows = DEC_SEQ // GRID_W
    row = np.repeat(np.arange(rows, dtype=np.float64), GRID_W)
    col = np.tile(np.arange(GRID_W, dtype=np.float64), rows)
    quarter = dim // 4
    inv = ROPE_THETA ** (-np.arange(quarter, dtype=np.float64) / quarter)
    ang = np.concatenate([row[:, None] * inv, col[:, None] * inv], axis=-1)
    cos, sin, zero = np.cos(ang), np.sin(ang), np.zeros_like(ang)
    reps = LANES // dim
    full = lambda a, b: np.tile(np.concatenate([a, b], axis=-1), (1, reps)).astype(np.float32)
    return jnp.asarray(full(cos, cos)), jnp.asarray(full(-sin, zero)), jnp.asarray(full(zero, sin))


DENSE = -1


def _qkv_kernel(*refs, tm, tn, norm, rope_dim, cache_split, out_scale):
    refs = list(refs)
    take = lambda n: [refs.pop(0) for _ in range(n)]
    h_ref, w_ref, gain_ref = take(3)
    cos_ref, slo_ref, shi_ref = take(3) if rope_dim else (None, None, None)
    (o_ref,) = take(1)
    (cache_ref,) = take(1) if cache_split else (None,)
    (wbf_ref,) = take(1)

    @pl.when(pl.program_id(1) == 0)
    def _():
        wbf_ref[...] = w_ref[...].astype(BF16)

    def finish(rope, write_cache):
        for r in range(tm // SEQ):
            rs = slice(r * SEQ, (r + 1) * SEQ)
            y = jnp.dot(h_ref[rs, :], wbf_ref[...], preferred_element_type=F32)
            pieces = []
            for c in range(tn // LANES):
                yc = y[:, c * LANES:(c + 1) * LANES]
                if norm:
                    yc = yc * lax.rsqrt(jnp.mean(yc * yc, axis=-1, keepdims=True) + EPS) * gain_ref[...]
                if rope:
                    half = rope_dim // 2
                    yc = (yc * cos_ref[rs, :]
                          + pltpu.roll(yc, LANES - half, 1) * slo_ref[rs, :]
                          + pltpu.roll(yc, half, 1) * shi_ref[rs, :])
                if out_scale != 1.0:
                    yc = yc * out_scale
                o_ref[rs, c * LANES:(c + 1) * LANES] = yc.astype(o_ref.dtype)
                pieces.append(yc)
                if write_cache and cache_split == DENSE:
                    cache_ref[rs, c * LANES:(c + 1) * LANES] = yc
                elif write_cache and cache_split == 1:
                    cache_ref[r, :, c, :] = yc
            if write_cache and cache_split > 1:
                full = jnp.concatenate(pieces, axis=1)
                cache_ref[r] = full.reshape(SEQ, tn // LANES, cache_split, LANES // cache_split)

    if rope_dim or cache_split:
        is_latent = pl.program_id(1) * tm >= N_CTX
        pl.when(jnp.logical_not(is_latent))(lambda: finish(False, bool(cache_split)))
        pl.when(is_latent)(lambda: finish(bool(rope_dim), False))
    else:
        finish(False, False)


def _qkv_proj(h, w, layer_j, col0, width, gain, tables, norm, rope_dim, out_scale=1.0, cache_split=0):
    tm = 1024
    tn = 1024 if width % 1024 == 0 else 512
    n0 = col0 // tn
    in_specs = [
        pl.BlockSpec((tm, D_MODEL), lambda j, i: (i, 0)),
        pl.BlockSpec((None, D_MODEL, tn), lambda j, i: (layer_j, 0, n0 + j)),
        pl.BlockSpec((1, LANES), lambda j, i: (0, 0)),
    ]
    args = [h, w, gain]
    if rope_dim:
        def tab_map(j, i):
            r = i * tm
            return (jnp.where(r >= N_CTX, ((r - N_CTX) % DEC_SEQ) // tm, 0), 0)
        in_specs += [pl.BlockSpec((tm, LANES), tab_map)] * 3
        args += list(tables)
    out_specs = [pl.BlockSpec((tm, tn), lambda j, i: (i, j))]
    out_shape = [jax.ShapeDtypeStruct((N_ROWS, width), BF16)]
    n_ctx = N_CTX // tm
    if cache_split == DENSE:
        out_specs.append(pl.BlockSpec((tm, tn), lambda j, i: (jnp.minimum(i, n_ctx - 1), j)))
        out_shape.append(jax.ShapeDtypeStruct((N_CTX, width), F32))
    elif cache_split:
        tail = (LANES,) if cache_split == 1 else (cache_split, LANES // cache_split)
        zeros = (0,) * len(tail)
        out_specs.append(pl.BlockSpec((tm // SEQ, None, SEQ, tn // LANES) + tail,
                                      lambda j, i: (jnp.minimum(i, n_ctx - 1), 0, 0, j) + zeros))
        out_shape.append(jax.ShapeDtypeStruct((BATCH, 1, SEQ, width // LANES) + tail, F32))
    outs = pl.pallas_call(
        functools.partial(_qkv_kernel, tm=tm, tn=tn, norm=norm, rope_dim=rope_dim,
                          cache_split=cache_split, out_scale=out_scale),
        grid=(width // tn, N_ROWS // tm),
        in_specs=in_specs,
        out_specs=out_specs,
        out_shape=out_shape,
        scratch_shapes=[pltpu.VMEM((D_MODEL, tn), BF16)],
        compiler_params=pltpu.CompilerParams(
            dimension_semantics=("arbitrary", "arbitrary"),
            vmem_limit_bytes=_vmem_limit(2 * tm * D_MODEL * 2 + 2 * D_MODEL * tn * 4 + D_MODEL * tn * 2
                                         + 2 * tm * tn * 2 + 2 * abs(cache_split) * tm * tn * 4
                                         + 6 * tm * LANES * 4 + 3 * tm * tn * 4)),
        name="qkv_proj",
    )(*args)
    return (outs[0], outs[1]) if cache_split else outs[0]


_NT = (((1,), (1,)), ((), ()))


LOG2_E = math.log2(math.e)


def _with_ones(v):
    v = v.astype(BF16)
    return jnp.concatenate([v, jnp.ones_like(v)], axis=1)


def _softmax_pv(q, keys, vals1):
    logits = [lax.dot_general(q, k, _NT, preferred_element_type=F32) for k in keys]
    m = functools.reduce(jnp.maximum, [jnp.max(s, axis=-1, keepdims=True) for s in logits])
    acc = None
    for s, v1 in zip(logits, vals1):
        pv = jnp.dot(jnp.exp2(s - m).astype(BF16), v1, preferred_element_type=F32)
        acc = pv if acc is None else acc + pv
    return acc[:, :LANES] / acc[:, LANES:]


def _attn_a_kernel(*refs, kv_heads, has_cache):
    if has_cache:
        q_ref, kc_ref, vc_ref, kn_ref, vn_ref, o_ref = refs
    else:
        q_ref, kn_ref, vn_ref, o_ref = refs
    for kv in range(kv_heads):
        ks = slice(kv * LANES, (kv + 1) * LANES)
        keys = [kn_ref[:, ks].astype(BF16)]
        vals1 = [_with_ones(vn_ref[:, ks])]
        if has_cache:
            keys.append(kc_ref[:, ks].astype(BF16))
            vals1.append(_with_ones(vc_ref[:, ks]))
        for g in range(A_REP):
            sl = slice((kv * A_REP + g) * LANES, (kv * A_REP + g + 1) * LANES)
            o_ref[:, sl] = _softmax_pv(q_ref[:, sl], keys, vals1).astype(o_ref.dtype)


def _attn_a(q, k, v, cache_k, cache_v):
    tq = SEQ
    kvs = A_KV_HEADS
    kw, gw = kvs * LANES, kvs * A_REP * LANES
    o_ctx = pl.pallas_call(
        functools.partial(_attn_a_kernel, kv_heads=kvs, has_cache=False),
        grid=(BATCH, A_KV_HEADS // kvs),
        in_specs=[
            pl.BlockSpec((tq, gw), lambda b, h: (b, h)),
            pl.BlockSpec((SEQ, kw), lambda b, h: (b, h)),
            pl.BlockSpec((SEQ, kw), lambda b, h: (b, h)),
        ],
        out_specs=pl.BlockSpec((tq, gw), lambda b, h: (b, h)),
        out_shape=jax.ShapeDtypeStruct((N_CTX, D_MODEL), BF16),
        compiler_params=pltpu.CompilerParams(
            dimension_semantics=("arbitrary", "arbitrary"),
            vmem_limit_bytes=_vmem_limit(4 * tq * gw * 2 + 4 * SEQ * kw * 2 + 4 * kvs * A_REP * tq * SEQ * 4)),
        name="attn_a_ctx",
    )(q, k, v)
    tq = 512
    kvs = A_KV_HEADS
    kw, gw = kvs * LANES, kvs * A_REP * LANES
    nq = DEC_SEQ // tq
    row0 = N_CTX // tq
    t_all = PAST_LEN + DEC_SEQ
    o_lat = pl.pallas_call(
        functools.partial(_attn_a_kernel, kv_heads=kvs, has_cache=True),
        grid=(DEC_BATCH, A_KV_HEADS // kvs, nq),
        in_specs=[
            pl.BlockSpec((tq, gw), lambda b, h, i: (row0 + b * nq + i, h)),
            pl.BlockSpec((None, PAST_LEN, kw), lambda b, h, i: (b, 0, h)),
            pl.BlockSpec((None, PAST_LEN, kw), lambda b, h, i: (b, 0, h)),
            pl.BlockSpec((DEC_SEQ, kw), lambda b, h, i: (N_CTX // DEC_SEQ + b, h)),
            pl.BlockSpec((DEC_SEQ, kw), lambda b, h, i: (N_CTX // DEC_SEQ + b, h)),
        ],
        out_specs=pl.BlockSpec((tq, gw), lambda b, h, i: (b * nq + i, h)),
        out_shape=jax.ShapeDtypeStruct((N_LAT, D_MODEL), BF16),
        compiler_params=pltpu.CompilerParams(
            dimension_semantics=("arbitrary", "arbitrary", "arbitrary"),
            vmem_limit_bytes=_vmem_limit(4 * tq * gw * 2 + 4 * t_all * kw * 4
                                         + 2 * kvs * A_REP * tq * t_all * 4)),
        name="attn_a_lat",
    )(q, cache_k, cache_v, k, v)
    return o_ctx, o_lat


def _attn_b_kernel(*refs, heads, has_cache, lam_init):
    if has_cache:
        q_ref, kc_ref, vc_ref, kn_ref, vn_ref, lam_ref, sub_ref, o_ref = refs
    else:
        q_ref, kn_ref, vn_ref, lam_ref, sub_ref, o_ref = refs
    lp = lam_ref[...]
    lam = (jnp.exp(jnp.sum(lp[0:1] * lp[1:2], axis=-1, keepdims=True))
           - jnp.exp(jnp.sum(lp[2:3] * lp[3:4], axis=-1, keepdims=True)) + lam_init)
    first = lax.broadcasted_iota(jnp.int32, (1, LANES), 1) < B_QK_DIM
    for h in range(heads):
        sl = slice(h * LANES, (h + 1) * LANES)
        q = q_ref[:, sl]
        zero = jnp.zeros_like(q)
        keys = [kn_ref[:, sl].astype(BF16)]
        vals1 = [_with_ones(vn_ref[:, sl])]
        if has_cache:
            keys.append(kc_ref[:, sl].astype(BF16))
            vals1.append(_with_ones(vc_ref[:, sl]))
        o1 = _softmax_pv(jnp.where(first, q, zero), keys, vals1)
        o2 = _softmax_pv(jnp.where(first, zero, q), keys, vals1)
        o = o1 - lam * o2
        o = o * lax.rsqrt(jnp.mean(o * o, axis=-1, keepdims=True) + EPS) * sub_ref[...] * (1.0 - lam_init)
        o_ref[:, sl] = o.astype(o_ref.dtype)


def _attn_b(q, k, v, cache_k, cache_v, lam_p, subln, lam_init):
    heads = 8
    gw = heads * LANES
    tq = SEQ
    o_ctx = pl.pallas_call(
        functools.partial(_attn_b_kernel, heads=heads, has_cache=False, lam_init=lam_init),
        grid=(BATCH, B_HEADS // heads),
        in_specs=[
            pl.BlockSpec((tq, gw), lambda b, h: (b, h)),
            pl.BlockSpec((SEQ, gw), lambda b, h: (b, h)),
            pl.BlockSpec((SEQ, gw), lambda b, h: (b, h)),
            pl.BlockSpec((4, B_QK_DIM), lambda b, h: (0, 0)),
            pl.BlockSpec((1, LANES), lambda b, h: (0, 0)),
        ],
        out_specs=pl.BlockSpec((tq, gw), lambda b, h: (b, h)),
        out_shape=jax.ShapeDtypeStruct((N_CTX, D_MODEL), BF16),
        compiler_params=pltpu.CompilerParams(
            dimension_semantics=("arbitrary", "arbitrary"),
            vmem_limit_bytes=_vmem_limit(4 * tq * gw * 2 + 4 * SEQ * gw * 4 + 8 * heads * tq * SEQ * 4)),
        name="attn_b_ctx",
    )(q, k, v, lam_p, subln)
    tq = 512
    nq = DEC_SEQ // tq
    row0 = N_CTX // tq
    t_all = PAST_LEN + DEC_SEQ
    heads = 8
    gw = heads * LANES
    o_lat = pl.pallas_call(
        functools.partial(_attn_b_kernel, heads=heads, has_cache=True, lam_init=lam_init),
        grid=(DEC_BATCH, B_HEADS // heads, nq),
        in_specs=[
            pl.BlockSpec((tq, gw), lambda b, h, i: (row0 + b * nq + i, h)),
            pl.BlockSpec((None, PAST_LEN, gw), lambda b, h, i: (b, 0, h)),
            pl.BlockSpec((None, PAST_LEN, gw), lambda b, h, i: (b, 0, h)),
            pl.BlockSpec((DEC_SEQ, gw), lambda b, h, i: (N_CTX // DEC_SEQ + b, h)),
            pl.BlockSpec((DEC_SEQ, gw), lambda b, h, i: (N_CTX // DEC_SEQ + b, h)),
            pl.BlockSpec((4, B_QK_DIM), lambda b, h, i: (0, 0)),
            pl.BlockSpec((1, LANES), lambda b, h, i: (0, 0)),
        ],
        out_specs=pl.BlockSpec((tq, gw), lambda b, h, i: (b * nq + i, h)),
        out_shape=jax.ShapeDtypeStruct((N_LAT, D_MODEL), BF16),
        compiler_params=pltpu.CompilerParams(
            dimension_semantics=("arbitrary", "arbitrary", "arbitrary"),
            vmem_limit_bytes=_vmem_limit(4 * tq * gw * 2 + 4 * t_all * gw * 4
                                         + 3 * 2 * heads * tq * t_all * 4)),
        name="attn_b_lat",
    )(q, cache_k, cache_v, k, v, lam_p, subln)
    return o_ctx, o_lat


def _diff_lambda_init(layer_idx):
    return 0.8 - 0.6 * math.exp(-0.3 * layer_idx)


def kernel(x_prompt, x_sample, cache_a_k, cache_a_v, cache_b_k, cache_b_v, c, c_ctx, ada_w, ada_b, ln_g, ln_b, ffn_w_in, ffn_w_out, a_w_qkv, a_q_norm, a_k_norm, a_w_o, b_w_qkv, b_lambda, b_subln, b_w_o):
    ln_g3 = ln_g.reshape(DEPTH * N_SUB, 1, D_MODEL)
    ln_b3 = ln_b.reshape(DEPTH * N_SUB, 1, D_MODEL)
    tables_a = _rope_tables(A_HEAD_DIM)
    tables_b = _rope_tables(B_QK_DIM)
    ones = jnp.ones((1, LANES), F32)

    cond = jnp.concatenate([c_ctx[None, :], c, jnp.zeros((COND_ROWS - N_GROUPS, D_MODEL), F32)], axis=0)
    mod = [_modulation(cond, ada_w, ada_b, 1)[0]]
    cond_t = cond.T

    x = (x_prompt.reshape(N_CTX, D_MODEL), x_sample.reshape(N_LAT, D_MODEL))
    h = _premod(x[0], x[1], mod, 0, 0)

    new_kv = {"ak": [], "av": [], "bk": [], "bv": []}
    for i in range(DEPTH):
        j = i // N_MIXERS
        last = i + 1 == DEPTH
        w_o_f32 = a_w_o if i % N_MIXERS == 0 else b_w_o
        a, w_down, w_o, _ = _ffn_up(h, ffn_w_in, ffn_w_out, i, 0, extra=(w_o_f32, (j,)))
        x, h = _proj_deepnorm(a, w_down, (), x, mod, ln_g3, ln_b3, i, 0, MACARON_WEIGHT, (i, 1))
        if i % N_MIXERS == 0:
            qw = A_HEADS * A_HEAD_DIM
            kw = A_KV_HEADS * A_HEAD_DIM
            q = _qkv_proj(h, a_w_qkv, j, 0, qw, a_q_norm[j][None, :], tables_a, True, A_HEAD_DIM,
                          out_scale=A_HEAD_DIM ** -0.5 * LOG2_E)
            k, new_k = _qkv_proj(h, a_w_qkv, j, qw, kw, a_k_norm[j][None, :], tables_a, True, A_HEAD_DIM,
                                 cache_split=1)
            v, new_v = _qkv_proj(h, a_w_qkv, j, qw + kw, kw, ones, None, False, 0, cache_split=1)
            o = _attn_a(q, k, v, cache_a_k[:, j].reshape(DEC_BATCH, PAST_LEN, kw),
                        cache_a_v[:, j].reshape(DEC_BATCH, PAST_LEN, kw))
            new_kv["ak"].append(new_k)
            new_kv["av"].append(new_v)
        else:
            lam_init = _diff_lambda_init(i)
            qw = B_HEADS * 2 * B_QK_DIM
            q = _qkv_proj(h, b_w_qkv, j, 0, qw, ones, tables_b, False, B_QK_DIM,
                          out_scale=B_QK_DIM ** -0.5 * LOG2_E)
            k, new_k = _qkv_proj(h, b_w_qkv, j, qw, qw, ones, tables_b, False, B_QK_DIM, cache_split=2)
            v, new_v = _qkv_proj(h, b_w_qkv, j, 2 * qw, B_HEADS * B_V_DIM, ones, None, False, 0,
                                 cache_split=DENSE)
            new_v = new_v.reshape(BATCH, 1, SEQ, B_HEADS, B_V_DIM)
            o = _attn_b(q, k, v, cache_b_k[:, j].reshape(DEC_BATCH, PAST_LEN, qw),
                        cache_b_v[:, j].reshape(DEC_BATCH, PAST_LEN, B_HEADS * B_V_DIM),
                        b_lambda[j], b_subln[j][None, :], lam_init)
            new_kv["bk"].append(new_k)
            new_kv["bv"].append(new_v)
        x, h = _proj_deepnorm(o, w_o, (), x, mod, ln_g3, ln_b3, i, 1, 1.0, (i, 2))
        a, w_down, _, next_mod = _ffn_up(h, ffn_w_in, ffn_w_out, i, 1,
                                         side_mod=None if last else (cond_t, ada_w, ada_b, i + 1))
        if not last:
            mod.append(next_mod)
        x, h = _proj_deepnorm(a, w_down, (), x, mod, ln_g3, ln_b3, i, 2, MACARON_WEIGHT,
                              None if last else (i + 1, 0), split_out=last)

    y_prompt = x[0].reshape(BATCH, SEQ, D_MODEL)
    y_sample = x[1].reshape(DEC_BATCH, DEC_SEQ, D_MODEL)
    return (y_prompt, y_sample,
            jnp.concatenate(new_kv["ak"], axis=1), jnp.concatenate(new_kv["av"], axis=1),
            jnp.concatenate(new_kv["bk"], axis=1), jnp.concatenate(new_kv["bv"], axis=1))
```

```python
import functools
import math

import jax
import jax.numpy as jnp
import numpy as np
from jax import lax
from jax.experimental import pallas as pl
from jax.experimental.pallas import tpu as pltpu

D_MODEL = 2048
BATCH = 16
SEQ = 256
DEPTH = 2
DEC_BATCH = 2
DEC_SEQ = 2048
PAST_LEN = 256
GRID_W = 64
N_MIXERS = 2
N_SUB = 3
A_HEAD_DIM = 128
A_HEADS = D_MODEL // A_HEAD_DIM
A_KV_HEADS = 4
A_REP = A_HEADS // A_KV_HEADS
B_QK_DIM = 64
B_V_DIM = 2 * B_QK_DIM
B_HEADS = D_MODEL // B_V_DIM
D_FF = ((8 * D_MODEL // 3 + 127) // 128) * 128
ROPE_THETA = 10000.0
EPS = 1e-6
MACARON_WEIGHT = 0.5
DEEPNORM_ALPHA = (2 * DEPTH) ** 0.25

N_CTX = BATCH * SEQ
N_LAT = DEC_BATCH * DEC_SEQ
N_ROWS = N_CTX + N_LAT
N_GROUPS = 1 + DEC_BATCH
COND_ROWS = 8
LANES = 128
V7X_VMEM_BYTES = 64 * 1024 * 1024

F32 = jnp.float32
BF16 = jnp.bfloat16


def _vmem_limit(nbytes):
    return int(min(nbytes * 3 // 2 + (2 << 20), V7X_VMEM_BYTES - (4 << 20)))


def _group_of_row(r):
    return jnp.where(r < N_CTX, 0, 1 + (r - N_CTX) // DEC_SEQ)


def _mod_spec(sub, kind):
    col = sub * 3 + kind
    return pl.BlockSpec((COND_ROWS, D_MODEL), lambda *idx: (0, col))


def _mod_row(ref, row0):
    return ref[pl.ds(_group_of_row(row0), 1), :]


def _mod_kernel(c_ref, w_ref, b_ref, o_ref):
    s = jax.nn.silu(c_ref[...]).astype(BF16)
    o_ref[...] = jnp.dot(s, w_ref[...].astype(BF16), preferred_element_type=F32) + b_ref[...]


def _modulation(cond, ada_w, ada_b, layers):
    tn = 1024
    n_out = N_SUB * 3 * D_MODEL
    return pl.pallas_call(
        _mod_kernel,
        grid=(layers, n_out // tn),
        in_specs=[
            pl.BlockSpec((COND_ROWS, D_MODEL), lambda l, j: (0, 0)),
            pl.BlockSpec((None, D_MODEL, tn), lambda l, j: (l, 0, j)),
            pl.BlockSpec((None, 1, tn), lambda l, j: (l, 0, j)),
        ],
        out_specs=pl.BlockSpec((None, COND_ROWS, tn), lambda l, j: (l, 0, j)),
        out_shape=jax.ShapeDtypeStruct((layers, COND_ROWS, n_out), F32),
        compiler_params=pltpu.CompilerParams(
            dimension_semantics=("arbitrary", "arbitrary"),
            vmem_limit_bytes=_vmem_limit(2 * D_MODEL * tn * 4 + D_MODEL * tn * 2)),
        name="modulation",
    )(cond, ada_w, ada_b.reshape(DEPTH, 1, n_out))


def _split_specs(tm, width, row_axis, shift=0):
    n_ctx, n_lat = N_CTX // tm, N_LAT // tm

    def ctx_map(*idx):
        return (jnp.clip(idx[row_axis] - shift, 0, n_ctx - 1), 0)

    def lat_map(*idx):
        return (jnp.clip(idx[row_axis] - shift - n_ctx, 0, n_lat - 1), 0)

    return pl.BlockSpec((tm, width), ctx_map), pl.BlockSpec((tm, width), lat_map)


def _premod_kernel(xc_ref, xl_ref, sc_ref, sh_ref, h_ref, *, tm):
    row0 = pl.program_id(0) * tm
    sc = 1.0 + _mod_row(sc_ref, row0)
    sh = _mod_row(sh_ref, row0)

    def emit(x_ref):
        h_ref[...] = (x_ref[...] * sc + sh).astype(BF16)

    pl.when(row0 < N_CTX)(lambda: emit(xc_ref))
    pl.when(row0 >= N_CTX)(lambda: emit(xl_ref))


def _premod(x_ctx, x_lat, mod, layer, sub):
    tm = 512
    return pl.pallas_call(
        functools.partial(_premod_kernel, tm=tm),
        grid=(N_ROWS // tm,),
        in_specs=[*_split_specs(tm, D_MODEL, 0),
                  _mod_spec(sub, 1), _mod_spec(sub, 0)],
        out_specs=pl.BlockSpec((tm, D_MODEL), lambda i: (i, 0)),
        out_shape=jax.ShapeDtypeStruct((N_ROWS, D_MODEL), BF16),
        compiler_params=pltpu.CompilerParams(
            dimension_semantics=("arbitrary",),
            vmem_limit_bytes=_vmem_limit(4 * tm * D_MODEL * 4 + 2 * tm * D_MODEL * 2)),
        name="premod",
    )(x_ctx, x_lat, mod[layer], mod[layer])


def _up_kernel(*refs, tf, ts, nf, kc, tail, extra, side_mod):
    refs = list(refs)
    take = lambda n: [refs.pop(0) for _ in range(n)]
    h_ref, wg_ref, wu_ref, wo_ref = take(4)
    (xw_ref,) = take(1) if extra else (None,)
    ct_ref, aw_ref, ab_ref = take(3) if side_mod else (None, None, None)
    a_ref, wob_ref = take(2)
    (xwb_ref,) = take(1) if extra else (None,)
    (mo_ref,) = take(1) if side_mod else (None,)
    (wbf_ref,) = take(1)
    j, i = pl.program_id(0), pl.program_id(1)
    n_sub = tf // ts
    lead = ts - tail

    def stage():
        rows = pl.ds(pl.multiple_of(i * kc, kc), kc)
        for s in range(n_sub):
            p0 = 2 * s * ts
            wbf_ref[j % 2, rows, p0:p0 + ts] = wg_ref[:, s * ts:(s + 1) * ts].astype(BF16)
            if s < n_sub - 1:
                wbf_ref[j % 2, rows, p0 + ts:p0 + 2 * ts] = wu_ref[:, s * ts:(s + 1) * ts].astype(BF16)
            else:
                wbf_ref[j % 2, rows, p0 + ts:p0 + ts + tail] = wu_ref[:, s * ts + lead:(s + 1) * ts].astype(BF16)
                wbf_ref[j % 2, rows, p0 + ts + tail:p0 + 2 * ts] = wu_ref[:, s * ts:s * ts + lead].astype(BF16)
        wob_ref[...] = wo_ref[...].astype(BF16)
        if extra:
            xwb_ref[...] = xw_ref[...].astype(BF16)

    def modulate():
        if side_mod:
            w = aw_ref[...]
            s_t = jax.nn.silu(ct_ref[...])
            rows = [jnp.sum(w * s_t[:, r:r + 1], axis=0, keepdims=True) for r in range(N_GROUPS)]
            rows += [jnp.zeros_like(rows[0])] * (COND_ROWS - N_GROUPS)
            mo_ref[...] = jnp.concatenate(rows, axis=0) + ab_ref[...]

    def swiglu(g, u):
        return (jax.nn.silu(g) * u).astype(BF16)

    def multiply():
        for s in range(n_sub):
            p0, c0 = 2 * s * ts, s * ts
            gu = jnp.dot(h_ref[...], wbf_ref[(j + 1) % 2, :, p0:p0 + 2 * ts], preferred_element_type=F32)
            if s < n_sub - 1:
                a_ref[:, c0:c0 + ts] = swiglu(gu[:, :ts], gu[:, ts:])
            else:
                a_ref[:, c0:c0 + lead] = swiglu(gu[:, :lead], gu[:, ts + tail:])
                a_ref[:, c0 + lead:c0 + ts] = swiglu(gu[:, lead:ts], gu[:, ts:ts + tail])

    @pl.when(j == 0)
    def _():
        stage()

    @pl.when(jnp.logical_and(j > 0, j < nf))
    def _():
        stage()
        modulate()
        multiply()

    @pl.when(j == nf)
    def _():
        modulate()
        p0 = 2 * (n_sub - 1) * ts + lead
        gu = jnp.dot(h_ref[...], wbf_ref[(j + 1) % 2, :, p0:p0 + 2 * tail], preferred_element_type=F32)
        a_ref[:, :tail] = swiglu(gu[:, :tail], gu[:, tail:])


def _ffn_up(h, w_in, w_out, layer, half, extra=None, side_mod=None):
    tm, tf, ts = 1024, 1024, 512
    nm = N_ROWS // tm
    nf = pl.cdiv(D_FF, tf)
    tail = D_FF - (nf - 1) * tf
    kc = D_MODEL // nm
    dc = LANES
    n_down = D_FF // dc
    assert 0 < tail <= ts
    assert D_FF % dc == 0 and n_down <= nf * nm
    staged = lambda j: jnp.minimum(j, nf - 1)
    krow = lambda j, i: pl.multiple_of(jnp.where(j < nf, i, nm - 1) * kc, kc)
    start = lambda j: pl.multiple_of(jnp.minimum(staged(j) * tf, D_FF - tf), LANES)
    ustart = lambda j: pl.multiple_of(D_FF + jnp.minimum(staged(j) * tf, D_FF - tf), LANES)
    down = lambda j, i: jnp.minimum(j * nm + i, n_down - 1)
    rows = lambda j, i: jnp.where(j > 0, i, 0)
    elem = pl.Element
    in_specs = [
        pl.BlockSpec((tm, D_MODEL), lambda j, i: (rows(j, i), 0)),
        pl.BlockSpec((None, None, elem(kc), elem(tf)), lambda j, i: (layer, half, krow(j, i), start(j))),
        pl.BlockSpec((None, None, elem(kc), elem(tf)), lambda j, i: (layer, half, krow(j, i), ustart(j))),
        pl.BlockSpec((None, None, dc, D_MODEL), lambda j, i: (layer, half, down(j, i), 0)),
    ]
    args = [h, w_in, w_in, w_out]
    out_specs = [pl.BlockSpec((tm, tf), lambda j, i: (rows(j, i), jnp.maximum(j - 1, 0))),
                 pl.BlockSpec((dc, D_MODEL), lambda j, i: (down(j, i), 0))]
    out_shape = [jax.ShapeDtypeStruct((N_ROWS, D_FF), BF16), jax.ShapeDtypeStruct((D_FF, D_MODEL), BF16)]
    ec = 0
    if extra:
        xw, xindex = extra
        xrows = xw.shape[len(xindex)]
        ec = 64
        n_extra = xrows // ec
        assert xrows % ec == 0 and n_extra <= nf * nm and xw.shape[-1] == D_MODEL
        xchunk = lambda j, i: jnp.minimum(j * nm + i, n_extra - 1)
        in_specs.append(pl.BlockSpec((None,) * len(xindex) + (ec, D_MODEL), lambda j, i: xindex + (xchunk(j, i), 0)))
        args.append(xw)
        out_specs.append(pl.BlockSpec((ec, D_MODEL), lambda j, i: (xchunk(j, i), 0)))
        out_shape.append(jax.ShapeDtypeStruct((xrows, D_MODEL), BF16))
    mc = 0
    if side_mod:
        cond_t, ada_w, ada_b, mod_layer = side_mod
        n_out = ada_w.shape[-1]
        mc = n_out // (nf * nm)
        assert n_out % (nf * nm) == 0 and mc % LANES == 0
        mchunk = lambda j, i: jnp.where(j > 0, (j - 1) * nm + i, 0)
        in_specs += [pl.BlockSpec((D_MODEL, COND_ROWS), lambda j, i: (0, 0)),
                     pl.BlockSpec((None, D_MODEL, mc), lambda j, i: (mod_layer, 0, mchunk(j, i))),
                     pl.BlockSpec((None, 1, mc), lambda j, i: (mod_layer, 0, mchunk(j, i)))]
        args += [cond_t, ada_w, ada_b.reshape(ada_b.shape[0], 1, n_out)]
        out_specs.append(pl.BlockSpec((COND_ROWS, mc), lambda j, i: (0, mchunk(j, i))))
        out_shape.append(jax.ShapeDtypeStruct((COND_ROWS, n_out), F32))
    outs = pl.pallas_call(
        functools.partial(_up_kernel, tf=tf, ts=ts, nf=nf, kc=kc, tail=tail, extra=bool(extra),
                          side_mod=bool(side_mod)),
        grid=(nf + 1, nm),
        in_specs=in_specs,
        out_specs=out_specs,
        out_shape=out_shape,
        scratch_shapes=[pltpu.VMEM((2, D_MODEL, 2 * tf), BF16)],
        compiler_params=pltpu.CompilerParams(
            dimension_semantics=("arbitrary", "arbitrary"),
            vmem_limit_bytes=_vmem_limit(2 * tm * D_MODEL * 2 + 4 * kc * tf * 4 + 2 * (dc + ec) * D_MODEL * 6 + 3 * D_MODEL * mc * 4
                                         + 4 * D_MODEL * tf * 2 + 2 * tm * tf * 2 + 3 * tm * tf * 4)),
        name="ffn_up",
    )(*args)
    return outs[0], outs[1], (outs[2] if extra else None), (outs[-1] if side_mod else None)


def _deepnorm_kernel(*refs, tm, coef, cast_w, split_lhs, split_x, split_out, emit_h):
    refs = list(refs)
    take = lambda n: [refs.pop(0) for _ in range(n)]
    lhs_refs = take(2 if split_lhs else 1)
    (w_ref,) = take(1)
    x_refs = take(2 if split_x else 1)
    gate_ref, lng_ref, lnb_ref = take(3)
    sc_ref, sh_ref = take(2) if emit_h else (None, None)
    xo_refs = take(2 if split_out else 1)
    (ho_ref,) = take(1) if emit_h else (None,)
    (wbf_ref,) = take(1) if cast_w else (w_ref,)

    if cast_w:
        @pl.when(pl.program_id(0) == 0)
        def _():
            wbf_ref[...] = w_ref[...].astype(BF16)

    row0 = pl.program_id(0) * tm
    gate = coef * _mod_row(gate_ref, row0)
    if emit_h:
        sc = 1.0 + _mod_row(sc_ref, row0)
        sh = _mod_row(sh_ref, row0)

    def body(lhs_ref, x_ref, xo_ref):
        y = jnp.dot(lhs_ref[...], wbf_ref[...], preferred_element_type=F32)
        z = DEEPNORM_ALPHA * x_ref[...] + gate * y
        mu = jnp.mean(z, axis=-1, keepdims=True)
        zc = z - mu
        var = jnp.mean(zc * zc, axis=-1, keepdims=True)
        o = zc * lax.rsqrt(var + EPS) * lng_ref[...] + lnb_ref[...]
        xo_ref[...] = o
        if emit_h:
            ho_ref[...] = (o * sc + sh).astype(BF16)

    if split_lhs or split_x or split_out:
        pl.when(row0 < N_CTX)(lambda: body(lhs_refs[0], x_refs[0], xo_refs[0]))
        pl.when(row0 >= N_CTX)(lambda: body(lhs_refs[-1], x_refs[-1], xo_refs[-1]))
    else:
        body(lhs_refs[0], x_refs[0], xo_refs[0])


def _proj_deepnorm(lhs, w, w_index, x, mod, ln_g, ln_b, layer, sub, coef, next_mod, split_out=False):
    split_lhs, split_x = isinstance(lhs, tuple), isinstance(x, tuple)
    k = (lhs[0] if split_lhs else lhs).shape[1]
    tm = 256
    cast_w = w.dtype != BF16
    emit_h = next_mod is not None
    lead = (None,) * len(w_index)
    row_spec = lambda width: pl.BlockSpec((tm, width), lambda i: (i, 0))
    in_specs, args = [], []
    if split_lhs:
        in_specs += list(_split_specs(tm, k, 0)); args += list(lhs)
    else:
        in_specs.append(row_spec(k)); args.append(lhs)
    in_specs.append(pl.BlockSpec(lead + (k, D_MODEL), lambda i: w_index + (0, 0), pipeline_mode=pl.Buffered(1)))
    args.append(w)
    if split_x:
        in_specs += list(_split_specs(tm, D_MODEL, 0)); args += list(x)
    else:
        in_specs.append(row_spec(D_MODEL)); args.append(x)
    ln_spec = pl.BlockSpec((None, 1, D_MODEL), lambda i: (layer * N_SUB + sub, 0, 0))
    in_specs += [_mod_spec(sub, 2), ln_spec, ln_spec]
    args += [mod[layer], ln_g, ln_b]
    if emit_h:
        in_specs += [_mod_spec(next_mod[1], 1), _mod_spec(next_mod[1], 0)]
        args += [mod[next_mod[0]], mod[next_mod[0]]]
    if split_out:
        out_specs = list(_split_specs(tm, D_MODEL, 0))
        out_shape = [jax.ShapeDtypeStruct((N_CTX, D_MODEL), F32), jax.ShapeDtypeStruct((N_LAT, D_MODEL), F32)]
    else:
        out_specs = [row_spec(D_MODEL)]
        out_shape = [jax.ShapeDtypeStruct((N_ROWS, D_MODEL), F32)]
    if emit_h:
        out_specs.append(row_spec(D_MODEL))
        out_shape.append(jax.ShapeDtypeStruct((N_ROWS, D_MODEL), BF16))
    w_bytes = k * D_MODEL * (6 if cast_w else 2)
    est = (2 * (1 + split_lhs) * tm * k * 2 + w_bytes + 2 * (2 + split_x + split_out) * tm * D_MODEL * 4
           + 2 * tm * D_MODEL * 2 + 4 * tm * D_MODEL * 4)
    outs = pl.pallas_call(
        functools.partial(_deepnorm_kernel, tm=tm, coef=coef, cast_w=cast_w, split_lhs=split_lhs,
                          split_x=split_x, split_out=split_out, emit_h=emit_h),
        grid=(N_ROWS // tm,),
        in_specs=in_specs,
        out_specs=out_specs,
        out_shape=out_shape,
        scratch_shapes=[pltpu.VMEM((k, D_MODEL), BF16)] if cast_w else [],
        compiler_params=pltpu.CompilerParams(
            dimension_semantics=("arbitrary",), vmem_limit_bytes=_vmem_limit(est)),
        name="proj_deepnorm",
    )(*args)
    x_new = (outs[0], outs[1]) if split_out else outs[0]
    return x_new, (outs[-1] if emit_h else None)


def _rope_tables(dim):
    rows = DEC_SEQ // GRID_W
    row = np.repeat(np.arange(rows, dtype=np.float64), GRID_W)
    col = np.tile(np.arange(GRID_W, dtype=np.float64), rows)
    quarter = dim // 4
    inv = ROPE_THETA ** (-np.arange(quarter, dtype=np.float64) / quarter)
    ang = np.concatenate([row[:, None] * inv, col[:, None] * inv], axis=-1)
    cos, sin, zero = np.cos(ang), np.sin(ang), np.zeros_like(ang)
    reps = LANES // dim
    full = lambda a, b: np.tile(np.concatenate([a, b], axis=-1), (1, reps)).astype(np.float32)
    return jnp.asarray(full(cos, cos)), jnp.asarray(full(-sin, zero)), jnp.asarray(full(zero, sin))


DENSE = -1


def _qkv_kernel(*refs, tm, tn, norm, rope_dim, cache_split, out_scale):
    refs = list(refs)
    take = lambda n: [refs.pop(0) for _ in range(n)]
    h_ref, w_ref, gain_ref = take(3)
    cos_ref, slo_ref, shi_ref = take(3) if rope_dim else (None, None, None)
    (o_ref,) = take(1)
    (cache_ref,) = take(1) if cache_split else (None,)
    (wbf_ref,) = take(1)

    @pl.when(pl.program_id(1) == 0)
    def _():
        wbf_ref[...] = w_ref[...].astype(BF16)

    def finish(rope, write_cache):
        for r in range(tm // SEQ):
            rs = slice(r * SEQ, (r + 1) * SEQ)
            y = jnp.dot(h_ref[rs, :], wbf_ref[...], preferred_element_type=F32)
            pieces = []
            for c in range(tn // LANES):
                yc = y[:, c * LANES:(c + 1) * LANES]
                if norm:
                    yc = yc * lax.rsqrt(jnp.mean(yc * yc, axis=-1, keepdims=True) + EPS) * gain_ref[...]
                if rope:
                    half = rope_dim // 2
                    yc = (yc * cos_ref[rs, :]
                          + pltpu.roll(yc, LANES - half, 1) * slo_ref[rs, :]
                          + pltpu.roll(yc, half, 1) * shi_ref[rs, :])
                if out_scale != 1.0:
                    yc = yc * out_scale
                o_ref[rs, c * LANES:(c + 1) * LANES] = yc.astype(o_ref.dtype)
                pieces.append(yc)
                if write_cache and cache_split == DENSE:
                    cache_ref[rs, c * LANES:(c + 1) * LANES] = yc
                elif write_cache and cache_split == 1:
                    cache_ref[r, :, c, :] = yc
            if write_cache and cache_split > 1:
                full = jnp.concatenate(pieces, axis=1)
                cache_ref[r] = full.reshape(SEQ, tn // LANES, cache_split, LANES // cache_split)

    if rope_dim or cache_split:
        is_latent = pl.program_id(1) * tm >= N_CTX
        pl.when(jnp.logical_not(is_latent))(lambda: finish(False, bool(cache_split)))
        pl.when(is_latent)(lambda: finish(bool(rope_dim), False))
    else:
        finish(False, False)


def _qkv_proj(h, w, layer_j, col0, width, gain, tables, norm, rope_dim, out_scale=1.0, cache_split=0):
    tm = 1024 if cache_split else 2048
    tn = 1024 if width % 1024 == 0 else 512
    n0 = col0 // tn
    in_specs = [
        pl.BlockSpec((tm, D_MODEL), lambda j, i: (i, 0)),
        pl.BlockSpec((None, D_MODEL, tn), lambda j, i: (layer_j, 0, n0 + j)),
        pl.BlockSpec((1, LANES), lambda j, i: (0, 0)),
    ]
    args = [h, w, gain]
    if rope_dim:
        def tab_map(j, i):
            r = i * tm
            return (jnp.where(r >= N_CTX, ((r - N_CTX) % DEC_SEQ) // tm, 0), 0)
        in_specs += [pl.BlockSpec((tm, LANES), tab_map)] * 3
        args += list(tables)
    out_specs = [pl.BlockSpec((tm, tn), lambda j, i: (i, j))]
    out_shape = [jax.ShapeDtypeStruct((N_ROWS, width), BF16)]
    n_ctx = N_CTX // tm
    if cache_split == DENSE:
        out_specs.append(pl.BlockSpec((tm, tn), lambda j, i: (jnp.minimum(i, n_ctx - 1), j)))
        out_shape.append(jax.ShapeDtypeStruct((N_CTX, width), F32))
    elif cache_split:
        tail = (LANES,) if cache_split == 1 else (cache_split, LANES // cache_split)
        zeros = (0,) * len(tail)
        out_specs.append(pl.BlockSpec((tm // SEQ, None, SEQ, tn // LANES) + tail,
                                      lambda j, i: (jnp.minimum(i, n_ctx - 1), 0, 0, j) + zeros))
        out_shape.append(jax.ShapeDtypeStruct((BATCH, 1, SEQ, width // LANES) + tail, F32))
    outs = pl.pallas_call(
        functools.partial(_qkv_kernel, tm=tm, tn=tn, norm=norm, rope_dim=rope_dim,
                          cache_split=cache_split, out_scale=out_scale),
        grid=(width // tn, N_ROWS // tm),
        in_specs=in_specs,
        out_specs=out_specs,
        out_shape=out_shape,
        scratch_shapes=[pltpu.VMEM((D_MODEL, tn), BF16)],
        compiler_params=pltpu.CompilerParams(
            dimension_semantics=("arbitrary", "arbitrary"),
            vmem_limit_bytes=_vmem_limit(2 * tm * D_MODEL * 2 + 2 * D_MODEL * tn * 4 + D_MODEL * tn * 2
                                         + 2 * tm * tn * 2 + 2 * abs(cache_split) * tm * tn * 4
                                         + 6 * tm * LANES * 4 + 3 * tm * tn * 4)),
        name="qkv_proj",
    )(*args)
    return (outs[0], outs[1]) if cache_split else outs[0]


_NT = (((1,), (1,)), ((), ()))


LOG2_E = math.log2(math.e)


def _with_ones(v):
    v = v.astype(BF16)
    return jnp.concatenate([v, jnp.ones_like(v)], axis=1)


def _softmax_pv(q, keys, vals1):
    logits = [lax.dot_general(q, k, _NT, preferred_element_type=F32) for k in keys]
    m = functools.reduce(jnp.maximum, [jnp.max(s, axis=-1, keepdims=True) for s in logits])
    acc = None
    for s, v1 in zip(logits, vals1):
        pv = jnp.dot(jnp.exp2(s - m).astype(BF16), v1, preferred_element_type=F32)
        acc = pv if acc is None else acc + pv
    return acc[:, :LANES] / acc[:, LANES:]


def _attn_a_kernel(*refs, kv_heads, has_cache):
    if has_cache:
        q_ref, kc_ref, vc_ref, kn_ref, vn_ref, o_ref = refs
    else:
        q_ref, kn_ref, vn_ref, o_ref = refs
    for kv in range(kv_heads):
        ks = slice(kv * LANES, (kv + 1) * LANES)
        keys = [kn_ref[:, ks].astype(BF16)]
        vals1 = [_with_ones(vn_ref[:, ks])]
        if has_cache:
            keys.append(kc_ref[:, ks].astype(BF16))
            vals1.append(_with_ones(vc_ref[:, ks]))
        for g in range(A_REP):
            sl = slice((kv * A_REP + g) * LANES, (kv * A_REP + g + 1) * LANES)
            o_ref[:, sl] = _softmax_pv(q_ref[:, sl], keys, vals1).astype(o_ref.dtype)


def _attn_a(q, k, v, cache_k, cache_v):
    tq = SEQ
    kvs = A_KV_HEADS
    kw, gw = kvs * LANES, kvs * A_REP * LANES
    o_ctx = pl.pallas_call(
        functools.partial(_attn_a_kernel, kv_heads=kvs, has_cache=False),
        grid=(BATCH, A_KV_HEADS // kvs),
        in_specs=[
            pl.BlockSpec((tq, gw), lambda b, h: (b, h)),
            pl.BlockSpec((SEQ, kw), lambda b, h: (b, h)),
            pl.BlockSpec((SEQ, kw), lambda b, h: (b, h)),
        ],
        out_specs=pl.BlockSpec((tq, gw), lambda b, h: (b, h)),
        out_shape=jax.ShapeDtypeStruct((N_CTX, D_MODEL), BF16),
        compiler_params=pltpu.CompilerParams(
            dimension_semantics=("arbitrary", "arbitrary"),
            vmem_limit_bytes=_vmem_limit(4 * tq * gw * 2 + 4 * SEQ * kw * 2 + 4 * kvs * A_REP * tq * SEQ * 4)),
        name="attn_a_ctx",
    )(q, k, v)
    tq = 512
    kvs = A_KV_HEADS
    kw, gw = kvs * LANES, kvs * A_REP * LANES
    nq = DEC_SEQ // tq
    row0 = N_CTX // tq
    t_all = PAST_LEN + DEC_SEQ
    o_lat = pl.pallas_call(
        functools.partial(_attn_a_kernel, kv_heads=kvs, has_cache=True),
        grid=(DEC_BATCH, A_KV_HEADS // kvs, nq),
        in_specs=[
            pl.BlockSpec((tq, gw), lambda b, h, i: (row0 + b * nq + i, h)),
            pl.BlockSpec((None, PAST_LEN, kw), lambda b, h, i: (b, 0, h)),
            pl.BlockSpec((None, PAST_LEN, kw), lambda b, h, i: (b, 0, h)),
            pl.BlockSpec((DEC_SEQ, kw), lambda b, h, i: (N_CTX // DEC_SEQ + b, h)),
            pl.BlockSpec((DEC_SEQ, kw), lambda b, h, i: (N_CTX // DEC_SEQ + b, h)),
        ],
        out_specs=pl.BlockSpec((tq, gw), lambda b, h, i: (b * nq + i, h)),
        out_shape=jax.ShapeDtypeStruct((N_LAT, D_MODEL), BF16),
        compiler_params=pltpu.CompilerParams(
            dimension_semantics=("arbitrary", "arbitrary", "arbitrary"),
            vmem_limit_bytes=_vmem_limit(4 * tq * gw * 2 + 4 * t_all * kw * 4
                                         + 2 * kvs * A_REP * tq * t_all * 4)),
        name="attn_a_lat",
    )(q, cache_k, cache_v, k, v)
    return o_ctx, o_lat


def _attn_b_kernel(*refs, heads, has_cache, lam_init):
    if has_cache:
        q_ref, kc_ref, vc_ref, kn_ref, vn_ref, lam_ref, sub_ref, o_ref = refs
    else:
        q_ref, kn_ref, vn_ref, lam_ref, sub_ref, o_ref = refs
    lp = lam_ref[...]
    lam = (jnp.exp(jnp.sum(lp[0:1] * lp[1:2], axis=-1, keepdims=True))
           - jnp.exp(jnp.sum(lp[2:3] * lp[3:4], axis=-1, keepdims=True)) + lam_init)
    first = lax.broadcasted_iota(jnp.int32, (1, LANES), 1) < B_QK_DIM
    for h in range(heads):
        sl = slice(h * LANES, (h + 1) * LANES)
        q = q_ref[:, sl]
        zero = jnp.zeros_like(q)
        keys = [kn_ref[:, sl].astype(BF16)]
        vals1 = [_with_ones(vn_ref[:, sl])]
        if has_cache:
            keys.append(kc_ref[:, sl].astype(BF16))
            vals1.append(_with_ones(vc_ref[:, sl]))
        o1 = _softmax_pv(jnp.where(first, q, zero), keys, vals1)
        o2 = _softmax_pv(jnp.where(first, zero, q), keys, vals1)
        o = o1 - lam * o2
        o = o * lax.rsqrt(jnp.mean(o * o, axis=-1, keepdims=True) + EPS) * sub_ref[...] * (1.0 - lam_init)
        o_ref[:, sl] = o.astype(o_ref.dtype)


def _attn_b(q, k, v, cache_k, cache_v, lam_p, subln, lam_init):
    heads = 8
    gw = heads * LANES
    tq = SEQ
    o_ctx = pl.pallas_call(
        functools.partial(_attn_b_kernel, heads=heads, has_cache=False, lam_init=lam_init),
        grid=(BATCH, B_HEADS // heads),
        in_specs=[
            pl.BlockSpec((tq, gw), lambda b, h: (b, h)),
            pl.BlockSpec((SEQ, gw), lambda b, h: (b, h)),
            pl.BlockSpec((SEQ, gw), lambda b, h: (b, h)),
            pl.BlockSpec((4, B_QK_DIM), lambda b, h: (0, 0)),
            pl.BlockSpec((1, LANES), lambda b, h: (0, 0)),
        ],
        out_specs=pl.BlockSpec((tq, gw), lambda b, h: (b, h)),
        out_shape=jax.ShapeDtypeStruct((N_CTX, D_MODEL), BF16),
        compiler_params=pltpu.CompilerParams(
            dimension_semantics=("arbitrary", "arbitrary"),
            vmem_limit_bytes=_vmem_limit(4 * tq * gw * 2 + 4 * SEQ * gw * 4 + 8 * heads * tq * SEQ * 4)),
        name="attn_b_ctx",
    )(q, k, v, lam_p, subln)
    tq = 512
    nq = DEC_SEQ // tq
    row0 = N_CTX // tq
    t_all = PAST_LEN + DEC_SEQ
    heads = 8
    gw = heads * LANES
    o_lat = pl.pallas_call(
        functools.partial(_attn_b_kernel, heads=heads, has_cache=True, lam_init=lam_init),
        grid=(DEC_BATCH, B_HEADS // heads, nq),
        in_specs=[
            pl.BlockSpec((tq, gw), lambda b, h, i: (row0 + b * nq + i, h)),
            pl.BlockSpec((None, PAST_LEN, gw), lambda b, h, i: (b, 0, h)),
            pl.BlockSpec((None, PAST_LEN, gw), lambda b, h, i: (b, 0, h)),
            pl.BlockSpec((DEC_SEQ, gw), lambda b, h, i: (N_CTX // DEC_SEQ + b, h)),
            pl.BlockSpec((DEC_SEQ, gw), lambda b, h, i: (N_CTX // DEC_SEQ + b, h)),
            pl.BlockSpec((4, B_QK_DIM), lambda b, h, i: (0, 0)),
            pl.BlockSpec((1, LANES), lambda b, h, i: (0, 0)),
        ],
        out_specs=pl.BlockSpec((tq, gw), lambda b, h, i: (b * nq + i, h)),
        out_shape=jax.ShapeDtypeStruct((N_LAT, D_MODEL), BF16),
        compiler_params=pltpu.CompilerParams(
            dimension_semantics=("arbitrary", "arbitrary", "arbitrary"),
            vmem_limit_bytes=_vmem_limit(4 * tq * gw * 2 + 4 * t_all * gw * 4
                                         + 3 * 2 * heads * tq * t_all * 4)),
        name="attn_b_lat",
    )(q, cache_k, cache_v, k, v, lam_p, subln)
    return o_ctx, o_lat


def _diff_lambda_init(layer_idx):
    return 0.8 - 0.6 * math.exp(-0.3 * layer_idx)


def kernel(x_prompt, x_sample, cache_a_k, cache_a_v, cache_b_k, cache_b_v, c, c_ctx, ada_w, ada_b, ln_g, ln_b, ffn_w_in, ffn_w_out, a_w_qkv, a_q_norm, a_k_norm, a_w_o, b_w_qkv, b_lambda, b_subln, b_w_o):
    ln_g3 = ln_g.reshape(DEPTH * N_SUB, 1, D_MODEL)
    ln_b3 = ln_b.reshape(DEPTH * N_SUB, 1, D_MODEL)
    tables_a = _rope_tables(A_HEAD_DIM)
    tables_b = _rope_tables(B_QK_DIM)
    ones = jnp.ones((1, LANES), F32)

    cond = jnp.concatenate([c_ctx[None, :], c, jnp.zeros((COND_ROWS - N_GROUPS, D_MODEL), F32)], axis=0)
    mod = [_modulation(cond, ada_w, ada_b, 1)[0]]
    cond_t = cond.T

    x = (x_prompt.reshape(N_CTX, D_MODEL), x_sample.reshape(N_LAT, D_MODEL))
    h = _premod(x[0], x[1], mod, 0, 0)

    new_kv = {"ak": [], "av": [], "bk": [], "bv": []}
    for i in range(DEPTH):
        j = i // N_MIXERS
        last = i + 1 == DEPTH
        w_o_f32 = a_w_o if i % N_MIXERS == 0 else b_w_o
        a, w_down, w_o, _ = _ffn_up(h, ffn_w_in, ffn_w_out, i, 0, extra=(w_o_f32, (j,)))
        x, h = _proj_deepnorm(a, w_down, (), x, mod, ln_g3, ln_b3, i, 0, MACARON_WEIGHT, (i, 1))
        if i % N_MIXERS == 0:
            qw = A_HEADS * A_HEAD_DIM
            kw = A_KV_HEADS * A_HEAD_DIM
            q = _qkv_proj(h, a_w_qkv, j, 0, qw, a_q_norm[j][None, :], tables_a, True, A_HEAD_DIM,
                          out_scale=A_HEAD_DIM ** -0.5 * LOG2_E)
            k, new_k = _qkv_proj(h, a_w_qkv, j, qw, kw, a_k_norm[j][None, :], tables_a, True, A_HEAD_DIM,
                                 cache_split=1)
            v, new_v = _qkv_proj(h, a_w_qkv, j, qw + kw, kw, ones, None, False, 0, cache_split=1)
            o = _attn_a(q, k, v, cache_a_k[:, j].reshape(DEC_BATCH, PAST_LEN, kw),
                        cache_a_v[:, j].reshape(DEC_BATCH, PAST_LEN, kw))
            new_kv["ak"].append(new_k)
            new_kv["av"].append(new_v)
        else:
            lam_init = _diff_lambda_init(i)
            qw = B_HEADS * 2 * B_QK_DIM
            q = _qkv_proj(h, b_w_qkv, j, 0, qw, ones, tables_b, False, B_QK_DIM,
                          out_scale=B_QK_DIM ** -0.5 * LOG2_E)
            k, new_k = _qkv_proj(h, b_w_qkv, j, qw, qw, ones, tables_b, False, B_QK_DIM, cache_split=2)
            v, new_v = _qkv_proj(h, b_w_qkv, j, 2 * qw, B_HEADS * B_V_DIM, ones, None, False, 0,
                                 cache_split=DENSE)
            new_v = new_v.reshape(BATCH, 1, SEQ, B_HEADS, B_V_DIM)
            o = _attn_b(q, k, v, cache_b_k[:, j].reshape(DEC_BATCH, PAST_LEN, qw),
                        cache_b_v[:, j].reshape(DEC_BATCH, PAST_LEN, B_HEADS * B_V_DIM),
                        b_lambda[j], b_subln[j][None, :], lam_init)
            new_kv["bk"].append(new_k)
            new_kv["bv"].append(new_v)
        x, h = _proj_deepnorm(o, w_o, (), x, mod, ln_g3, ln_b3, i, 1, 1.0, (i, 2))
        a, w_down, _, next_mod = _ffn_up(h, ffn_w_in, ffn_w_out, i, 1,
                                         side_mod=None if last else (cond_t, ada_w, ada_b, i + 1))
        if not last:
            mod.append(next_mod)
        x, h = _proj_deepnorm(a, w_down, (), x, mod, ln_g3, ln_b3, i, 2, MACARON_WEIGHT,
                              None if last else (i + 1, 0), split_out=last)

    y_prompt = x[0].reshape(BATCH, SEQ, D_MODEL)
    y_sample = x[1].reshape(DEC_BATCH, DEC_SEQ, D_MODEL)
    return (y_prompt, y_sample,
            jnp.concatenate(new_kv["ak"], axis=1), jnp.concatenate(new_kv["av"], axis=1),
            jnp.concatenate(new_kv["bk"], axis=1), jnp.concatenate(new_kv["bv"], axis=1))
```

```python
import functools
import math

import jax
import jax.numpy as jnp
import numpy as np
from jax import lax
from jax.experimental import pallas as pl
from jax.experimental.pallas import tpu as pltpu

D_MODEL = 2048
BATCH = 16
SEQ = 256
DEPTH = 2
DEC_BATCH = 2
DEC_SEQ = 2048
PAST_LEN = 256
GRID_W = 64
N_MIXERS = 2
N_SUB = 3
A_HEAD_DIM = 128
A_HEADS = D_MODEL // A_HEAD_DIM
A_KV_HEADS = 4
A_REP = A_HEADS // A_KV_HEADS
B_QK_DIM = 64
B_V_DIM = 2 * B_QK_DIM
B_HEADS = D_MODEL // B_V_DIM
D_FF = ((8 * D_MODEL // 3 + 127) // 128) * 128
ROPE_THETA = 10000.0
EPS = 1e-6
MACARON_WEIGHT = 0.5
DEEPNORM_ALPHA = (2 * DEPTH) ** 0.25

N_CTX = BATCH * SEQ
N_LAT = DEC_BATCH * DEC_SEQ
N_ROWS = N_CTX + N_LAT
N_GROUPS = 1 + DEC_BATCH
COND_ROWS = 8
LANES = 128
V7X_VMEM_BYTES = 64 * 1024 * 1024

F32 = jnp.float32
BF16 = jnp.bfloat16


def _vmem_limit(nbytes):
    return int(min(nbytes * 3 // 2 + (2 << 20), V7X_VMEM_BYTES - (4 << 20)))


def _group_of_row(r):
    return jnp.where(r < N_CTX, 0, 1 + (r - N_CTX) // DEC_SEQ)


def _mod_spec(sub, kind):
    col = sub * 3 + kind
    return pl.BlockSpec((COND_ROWS, D_MODEL), lambda *idx: (0, col))


def _mod_row(ref, row0):
    return ref[pl.ds(_group_of_row(row0), 1), :]


def _mod_kernel(c_ref, w_ref, b_ref, o_ref):
    s = jax.nn.silu(c_ref[...]).astype(BF16)
    o_ref[...] = jnp.dot(s, w_ref[...].astype(BF16), preferred_element_type=F32) + b_ref[...]


def _modulation(cond, ada_w, ada_b, layers, n_cols):
    tn = 1024
    n_out = N_SUB * 3 * D_MODEL
    return pl.pallas_call(
        _mod_kernel,
        grid=(layers, n_cols // tn),
        in_specs=[
            pl.BlockSpec((COND_ROWS, D_MODEL), lambda l, j: (0, 0)),
            pl.BlockSpec((None, D_MODEL, tn), lambda l, j: (l, 0, j)),
            pl.BlockSpec((None, 1, tn), lambda l, j: (l, 0, j)),
        ],
        out_specs=pl.BlockSpec((None, COND_ROWS, tn), lambda l, j: (l, 0, j)),
        out_shape=jax.ShapeDtypeStruct((layers, COND_ROWS, n_cols), F32),
        compiler_params=pltpu.CompilerParams(
            dimension_semantics=("arbitrary", "arbitrary"),
            vmem_limit_bytes=_vmem_limit(2 * D_MODEL * tn * 4 + D_MODEL * tn * 2)),
        name="modulation",
    )(cond, ada_w, ada_b.reshape(DEPTH, 1, n_out))


def _split_specs(tm, width, row_axis, shift=0):
    n_ctx, n_lat = N_CTX // tm, N_LAT // tm

    def ctx_map(*idx):
        return (jnp.clip(idx[row_axis] - shift, 0, n_ctx - 1), 0)

    def lat_map(*idx):
        return (jnp.clip(idx[row_axis] - shift - n_ctx, 0, n_lat - 1), 0)

    return pl.BlockSpec((tm, width), ctx_map), pl.BlockSpec((tm, width), lat_map)


def _premod_kernel(xc_ref, xl_ref, sc_ref, sh_ref, h_ref, *, tm):
    row0 = pl.program_id(0) * tm
    sc = 1.0 + _mod_row(sc_ref, row0)
    sh = _mod_row(sh_ref, row0)

    def emit(x_ref):
        h_ref[...] = (x_ref[...] * sc + sh).astype(BF16)

    pl.when(row0 < N_CTX)(lambda: emit(xc_ref))
    pl.when(row0 >= N_CTX)(lambda: emit(xl_ref))


def _premod(x_ctx, x_lat, mod, layer, sub):
    tm = 512
    return pl.pallas_call(
        functools.partial(_premod_kernel, tm=tm),
        grid=(N_ROWS // tm,),
        in_specs=[*_split_specs(tm, D_MODEL, 0),
                  _mod_spec(sub, 1), _mod_spec(sub, 0)],
        out_specs=pl.BlockSpec((tm, D_MODEL), lambda i: (i, 0)),
        out_shape=jax.ShapeDtypeStruct((N_ROWS, D_MODEL), BF16),
        compiler_params=pltpu.CompilerParams(
            dimension_semantics=("arbitrary",),
            vmem_limit_bytes=_vmem_limit(4 * tm * D_MODEL * 4 + 2 * tm * D_MODEL * 2)),
        name="premod",
    )(x_ctx, x_lat, mod[layer], mod[layer])


def _up_kernel(*refs, tf, ts, nf, kc, tail, extra, side_mod):
    refs = list(refs)
    take = lambda n: [refs.pop(0) for _ in range(n)]
    h_ref, wg_ref, wu_ref, wo_ref = take(4)
    (xw_ref,) = take(1) if extra else (None,)
    ct_ref, aw_ref, ab_ref = take(3) if side_mod else (None, None, None)
    a_ref, wob_ref = take(2)
    (xwb_ref,) = take(1) if extra else (None,)
    (mo_ref,) = take(1) if side_mod else (None,)
    (wbf_ref,) = take(1)
    j, i = pl.program_id(0), pl.program_id(1)
    n_sub = tf // ts
    lead = ts - tail

    def stage():
        rows = pl.ds(pl.multiple_of(i * kc, kc), kc)
        for s in range(n_sub):
            p0 = 2 * s * ts
            wbf_ref[j % 2, rows, p0:p0 + ts] = wg_ref[:, s * ts:(s + 1) * ts].astype(BF16)
            if s < n_sub - 1:
                wbf_ref[j % 2, rows, p0 + ts:p0 + 2 * ts] = wu_ref[:, s * ts:(s + 1) * ts].astype(BF16)
            else:
                wbf_ref[j % 2, rows, p0 + ts:p0 + ts + tail] = wu_ref[:, s * ts + lead:(s + 1) * ts].astype(BF16)
                wbf_ref[j % 2, rows, p0 + ts + tail:p0 + 2 * ts] = wu_ref[:, s * ts:s * ts + lead].astype(BF16)
        wob_ref[...] = wo_ref[...].astype(BF16)
        if extra:
            xwb_ref[...] = xw_ref[...].astype(BF16)

    def modulate():
        if side_mod:
            w = aw_ref[...]
            s_t = jax.nn.silu(ct_ref[...])
            rows = [jnp.sum(w * s_t[:, r:r + 1], axis=0, keepdims=True) for r in range(N_GROUPS)]
            rows += [jnp.zeros_like(rows[0])] * (COND_ROWS - N_GROUPS)
            mo_ref[...] = jnp.concatenate(rows, axis=0) + ab_ref[...]

    def swiglu(g, u):
        return (jax.nn.silu(g) * u).astype(BF16)

    def multiply():
        for s in range(n_sub):
            p0, c0 = 2 * s * ts, s * ts
            gu = jnp.dot(h_ref[...], wbf_ref[(j + 1) % 2, :, p0:p0 + 2 * ts], preferred_element_type=F32)
            if s < n_sub - 1:
                a_ref[:, c0:c0 + ts] = swiglu(gu[:, :ts], gu[:, ts:])
            else:
                a_ref[:, c0:c0 + lead] = swiglu(gu[:, :lead], gu[:, ts + tail:])
                a_ref[:, c0 + lead:c0 + ts] = swiglu(gu[:, lead:ts], gu[:, ts:ts + tail])

    @pl.when(j == 0)
    def _():
        stage()

    @pl.when(jnp.logical_and(j > 0, j < nf))
    def _():
        stage()
        modulate()
        multiply()

    @pl.when(j == nf)
    def _():
        modulate()
        p0 = 2 * (n_sub - 1) * ts + lead
        gu = jnp.dot(h_ref[...], wbf_ref[(j + 1) % 2, :, p0:p0 + 2 * tail], preferred_element_type=F32)
        a_ref[:, :tail] = swiglu(gu[:, :tail], gu[:, tail:])


def _ffn_up(h, w_in, w_out, layer, half, extra=None, side_mod=None):
    tm, tf, ts = 1024, 1024, 512
    nm = N_ROWS // tm
    nf = pl.cdiv(D_FF, tf)
    tail = D_FF - (nf - 1) * tf
    kc = D_MODEL // nm
    dc = LANES
    n_down = D_FF // dc
    assert 0 < tail <= ts
    assert D_FF % dc == 0 and n_down <= nf * nm
    staged = lambda j: jnp.minimum(j, nf - 1)
    krow = lambda j, i: pl.multiple_of(jnp.where(j < nf, i, nm - 1) * kc, kc)
    start = lambda j: pl.multiple_of(jnp.minimum(staged(j) * tf, D_FF - tf), LANES)
    ustart = lambda j: pl.multiple_of(D_FF + jnp.minimum(staged(j) * tf, D_FF - tf), LANES)
    down = lambda j, i: jnp.minimum(j * nm + i, n_down - 1)
    rows = lambda j, i: jnp.where(j > 0, i, 0)
    elem = pl.Element
    in_specs = [
        pl.BlockSpec((tm, D_MODEL), lambda j, i: (rows(j, i), 0)),
        pl.BlockSpec((None, None, elem(kc), elem(tf)), lambda j, i: (layer, half, krow(j, i), start(j))),
        pl.BlockSpec((None, None, elem(kc), elem(tf)), lambda j, i: (layer, half, krow(j, i), ustart(j))),
        pl.BlockSpec((None, None, dc, D_MODEL), lambda j, i: (layer, half, down(j, i), 0)),
    ]
    args = [h, w_in, w_in, w_out]
    out_specs = [pl.BlockSpec((tm, tf), lambda j, i: (rows(j, i), jnp.maximum(j - 1, 0))),
                 pl.BlockSpec((dc, D_MODEL), lambda j, i: (down(j, i), 0))]
    out_shape = [jax.ShapeDtypeStruct((N_ROWS, D_FF), BF16), jax.ShapeDtypeStruct((D_FF, D_MODEL), BF16)]
    ec = 0
    if extra:
        xw, xindex = extra
        xrows = xw.shape[len(xindex)]
        ec = 64
        n_extra = xrows // ec
        assert xrows % ec == 0 and n_extra <= nf * nm and xw.shape[-1] == D_MODEL
        xchunk = lambda j, i: jnp.minimum(j * nm + i, n_extra - 1)
        in_specs.append(pl.BlockSpec((None,) * len(xindex) + (ec, D_MODEL), lambda j, i: xindex + (xchunk(j, i), 0)))
        args.append(xw)
        out_specs.append(pl.BlockSpec((ec, D_MODEL), lambda j, i: (xchunk(j, i), 0)))
        out_shape.append(jax.ShapeDtypeStruct((xrows, D_MODEL), BF16))
    mc = 0
    if side_mod:
        cond_t, ada_w, ada_b, mod_layer = side_mod
        n_out = ada_w.shape[-1]
        mc = n_out // (nf * nm)
        assert n_out % (nf * nm) == 0 and mc % LANES == 0
        mchunk = lambda j, i: jnp.where(j > 0, (j - 1) * nm + i, 0)
        in_specs += [pl.BlockSpec((D_MODEL, COND_ROWS), lambda j, i: (0, 0)),
                     pl.BlockSpec((None, D_MODEL, mc), lambda j, i: (mod_layer, 0, mchunk(j, i))),
                     pl.BlockSpec((None, 1, mc), lambda j, i: (mod_layer, 0, mchunk(j, i)))]
        args += [cond_t, ada_w, ada_b.reshape(ada_b.shape[0], 1, n_out)]
        out_specs.append(pl.BlockSpec((COND_ROWS, mc), lambda j, i: (0, mchunk(j, i))))
        out_shape.append(jax.ShapeDtypeStruct((COND_ROWS, n_out), F32))
    outs = pl.pallas_call(
        functools.partial(_up_kernel, tf=tf, ts=ts, nf=nf, kc=kc, tail=tail, extra=bool(extra),
                          side_mod=bool(side_mod)),
        grid=(nf + 1, nm),
        in_specs=in_specs,
        out_specs=out_specs,
        out_shape=out_shape,
        scratch_shapes=[pltpu.VMEM((2, D_MODEL, 2 * tf), BF16)],
        compiler_params=pltpu.CompilerParams(
            dimension_semantics=("arbitrary", "arbitrary"),
            vmem_limit_bytes=_vmem_limit(2 * tm * D_MODEL * 2 + 4 * kc * tf * 4 + 2 * (dc + ec) * D_MODEL * 6 + 3 * D_MODEL * mc * 4
                                         + 4 * D_MODEL * tf * 2 + 2 * tm * tf * 2 + 3 * tm * tf * 4)),
        name="ffn_up",
    )(*args)
    return outs[0], outs[1], (outs[2] if extra else None), (outs[-1] if side_mod else None)


def _deepnorm_kernel(*refs, tm, coef, cast_w, split_lhs, split_x, split_out, emit_h):
    refs = list(refs)
    take = lambda n: [refs.pop(0) for _ in range(n)]
    lhs_refs = take(2 if split_lhs else 1)
    (w_ref,) = take(1)
    x_refs = take(2 if split_x else 1)
    gate_ref, lng_ref, lnb_ref = take(3)
    sc_ref, sh_ref = take(2) if emit_h else (None, None)
    xo_refs = take(2 if split_out else 1)
    (ho_ref,) = take(1) if emit_h else (None,)
    (wbf_ref,) = take(1) if cast_w else (w_ref,)

    if cast_w:
        @pl.when(pl.program_id(0) == 0)
        def _():
            wbf_ref[...] = w_ref[...].astype(BF16)

    row0 = pl.program_id(0) * tm
    gate = coef * _mod_row(gate_ref, row0)
    if emit_h:
        sc = 1.0 + _mod_row(sc_ref, row0)
        sh = _mod_row(sh_ref, row0)

    def body(lhs_ref, x_ref, xo_ref):
        y = jnp.dot(lhs_ref[...], wbf_ref[...], preferred_element_type=F32)
        z = DEEPNORM_ALPHA * x_ref[...] + gate * y
        mu = jnp.mean(z, axis=-1, keepdims=True)
        zc = z - mu
        var = jnp.mean(zc * zc, axis=-1, keepdims=True)
        o = zc * lax.rsqrt(var + EPS) * lng_ref[...] + lnb_ref[...]
        xo_ref[...] = o
        if emit_h:
            ho_ref[...] = (o * sc + sh).astype(BF16)

    if split_lhs or split_x or split_out:
        pl.when(row0 < N_CTX)(lambda: body(lhs_refs[0], x_refs[0], xo_refs[0]))
        pl.when(row0 >= N_CTX)(lambda: body(lhs_refs[-1], x_refs[-1], xo_refs[-1]))
    else:
        body(lhs_refs[0], x_refs[0], xo_refs[0])


def _proj_deepnorm(lhs, w, w_index, x, mod, ln_g, ln_b, layer, sub, coef, next_mod, split_out=False):
    split_lhs, split_x = isinstance(lhs, tuple), isinstance(x, tuple)
    k = (lhs[0] if split_lhs else lhs).shape[1]
    tm = 256
    cast_w = w.dtype != BF16
    emit_h = next_mod is not None
    lead = (None,) * len(w_index)
    row_spec = lambda width: pl.BlockSpec((tm, width), lambda i: (i, 0))
    in_specs, args = [], []
    if split_lhs:
        in_specs += list(_split_specs(tm, k, 0)); args += list(lhs)
    else:
        in_specs.append(row_spec(k)); args.append(lhs)
    in_specs.append(pl.BlockSpec(lead + (k, D_MODEL), lambda i: w_index + (0, 0), pipeline_mode=pl.Buffered(1)))
    args.append(w)
    if split_x:
        in_specs += list(_split_specs(tm, D_MODEL, 0)); args += list(x)
    else:
        in_specs.append(row_spec(D_MODEL)); args.append(x)
    ln_spec = pl.BlockSpec((None, 1, D_MODEL), lambda i: (layer * N_SUB + sub, 0, 0))
    in_specs += [_mod_spec(sub, 2), ln_spec, ln_spec]
    args += [mod[layer], ln_g, ln_b]
    if emit_h:
        in_specs += [_mod_spec(next_mod[1], 1), _mod_spec(next_mod[1], 0)]
        args += [mod[next_mod[0]], mod[next_mod[0]]]
    if split_out:
        out_specs = list(_split_specs(tm, D_MODEL, 0))
        out_shape = [jax.ShapeDtypeStruct((N_CTX, D_MODEL), F32), jax.ShapeDtypeStruct((N_LAT, D_MODEL), F32)]
    else:
        out_specs = [row_spec(D_MODEL)]
        out_shape = [jax.ShapeDtypeStruct((N_ROWS, D_MODEL), F32)]
    if emit_h:
        out_specs.append(row_spec(D_MODEL))
        out_shape.append(jax.ShapeDtypeStruct((N_ROWS, D_MODEL), BF16))
    w_bytes = k * D_MODEL * (6 if cast_w else 2)
    est = (2 * (1 + split_lhs) * tm * k * 2 + w_bytes + 2 * (2 + split_x + split_out) * tm * D_MODEL * 4
           + 2 * tm * D_MODEL * 2 + 4 * tm * D_MODEL * 4)
    outs = pl.pallas_call(
        functools.partial(_deepnorm_kernel, tm=tm, coef=coef, cast_w=cast_w, split_lhs=split_lhs,
                          split_x=split_x, split_out=split_out, emit_h=emit_h),
        grid=(N_ROWS // tm,),
        in_specs=in_specs,
        out_specs=out_specs,
        out_shape=out_shape,
        scratch_shapes=[pltpu.VMEM((k, D_MODEL), BF16)] if cast_w else [],
        compiler_params=pltpu.CompilerParams(
            dimension_semantics=("arbitrary",), vmem_limit_bytes=_vmem_limit(est)),
        name="proj_deepnorm",
    )(*args)
    x_new = (outs[0], outs[1]) if split_out else outs[0]
    return x_new, (outs[-1] if emit_h else None)


def _rope_tables(dim):
    rows = DEC_SEQ // GRID_W
    row = np.repeat(np.arange(rows, dtype=np.float64), GRID_W)
    col = np.tile(np.arange(GRID_W, dtype=np.float64), rows)
    quarter = dim // 4
    inv = ROPE_THETA ** (-np.arange(quarter, dtype=np.float64) / quarter)
    ang = np.concatenate([row[:, None] * inv, col[:, None] * inv], axis=-1)
    cos, sin, zero = np.cos(ang), np.sin(ang), np.zeros_like(ang)
    reps = LANES // dim
    full = lambda a, b: np.tile(np.concatenate([a, b], axis=-1), (1, reps)).astype(np.float32)
    return jnp.asarray(full(cos, cos)), jnp.asarray(full(-sin, zero)), jnp.asarray(full(zero, sin))


DENSE = -1


def _qkv_kernel(*refs, tm, tn, norm, rope_dim, cache_split, out_scale):
    refs = list(refs)
    take = lambda n: [refs.pop(0) for _ in range(n)]
    h_ref, w_ref, gain_ref = take(3)
    cos_ref, slo_ref, shi_ref = take(3) if rope_dim else (None, None, None)
    (o_ref,) = take(1)
    (cache_ref,) = take(1) if cache_split else (None,)
    (wbf_ref,) = take(1)

    @pl.when(pl.program_id(1) == 0)
    def _():
        wbf_ref[...] = w_ref[...].astype(BF16)

    def finish(rope, write_cache):
        for r in range(tm // SEQ):
            rs = slice(r * SEQ, (r + 1) * SEQ)
            y = jnp.dot(h_ref[rs, :], wbf_ref[...], preferred_element_type=F32)
            pieces = []
            for c in range(tn // LANES):
                yc = y[:, c * LANES:(c + 1) * LANES]
                if norm:
                    yc = yc * lax.rsqrt(jnp.mean(yc * yc, axis=-1, keepdims=True) + EPS) * gain_ref[...]
                if rope:
                    half = rope_dim // 2
                    yc = (yc * cos_ref[rs, :]
                          + pltpu.roll(yc, LANES - half, 1) * slo_ref[rs, :]
                          + pltpu.roll(yc, half, 1) * shi_ref[rs, :])
                if out_scale != 1.0:
                    yc = yc * out_scale
                o_ref[rs, c * LANES:(c + 1) * LANES] = yc.astype(o_ref.dtype)
                pieces.append(yc)
                if write_cache and cache_split == DENSE:
                    cache_ref[rs, c * LANES:(c + 1) * LANES] = yc
                elif write_cache and cache_split == 1:
                    cache_ref[r, :, c, :] = yc
            if write_cache and cache_split > 1:
                full = jnp.concatenate(pieces, axis=1)
                cache_ref[r] = full.reshape(SEQ, tn // LANES, cache_split, LANES // cache_split)

    if rope_dim or cache_split:
        is_latent = pl.program_id(1) * tm >= N_CTX
        pl.when(jnp.logical_not(is_latent))(lambda: finish(False, bool(cache_split)))
        pl.when(is_latent)(lambda: finish(bool(rope_dim), False))
    else:
        finish(False, False)


def _qkv_proj(h, w, layer_j, col0, width, gain, tables, norm, rope_dim, out_scale=1.0, cache_split=0):
    tm = 1024
    tn = 1024 if width % 1024 == 0 else 512
    n0 = col0 // tn
    in_specs = [
        pl.BlockSpec((tm, D_MODEL), lambda j, i: (i, 0)),
        pl.BlockSpec((None, D_MODEL, tn), lambda j, i: (layer_j, 0, n0 + j)),
        pl.BlockSpec((1, LANES), lambda j, i: (0, 0)),
    ]
    args = [h, w, gain]
    if rope_dim:
        def tab_map(j, i):
            r = i * tm
            return (jnp.where(r >= N_CTX, ((r - N_CTX) % DEC_SEQ) // tm, 0), 0)
        in_specs += [pl.BlockSpec((tm, LANES), tab_map)] * 3
        args += list(tables)
    out_specs = [pl.BlockSpec((tm, tn), lambda j, i: (i, j))]
    out_shape = [jax.ShapeDtypeStruct((N_ROWS, width), BF16)]
    n_ctx = N_CTX // tm
    if cache_split == DENSE:
        out_specs.append(pl.BlockSpec((tm, tn), lambda j, i: (jnp.minimum(i, n_ctx - 1), j)))
        out_shape.append(jax.ShapeDtypeStruct((N_CTX, width), F32))
    elif cache_split:
        tail = (LANES,) if cache_split == 1 else (cache_split, LANES // cache_split)
        zeros = (0,) * len(tail)
        out_specs.append(pl.BlockSpec((tm // SEQ, None, SEQ, tn // LANES) + tail,
                                      lambda j, i: (jnp.minimum(i, n_ctx - 1), 0, 0, j) + zeros))
        out_shape.append(jax.ShapeDtypeStruct((BATCH, 1, SEQ, width // LANES) + tail, F32))
    outs = pl.pallas_call(
        functools.partial(_qkv_kernel, tm=tm, tn=tn, norm=norm, rope_dim=rope_dim,
                          cache_split=cache_split, out_scale=out_scale),
        grid=(width // tn, N_ROWS // tm),
        in_specs=in_specs,
        out_specs=out_specs,
        out_shape=out_shape,
        scratch_shapes=[pltpu.VMEM((D_MODEL, tn), BF16)],
        compiler_params=pltpu.CompilerParams(
            dimension_semantics=("arbitrary", "arbitrary"),
            vmem_limit_bytes=_vmem_limit(2 * tm * D_MODEL * 2 + 2 * D_MODEL * tn * 4 + D_MODEL * tn * 2
                                         + 2 * tm * tn * 2 + 2 * abs(cache_split) * tm * tn * 4
                                         + 6 * tm * LANES * 4 + 3 * tm * tn * 4)),
        name="qkv_proj",
    )(*args)
    return (outs[0], outs[1]) if cache_split else outs[0]


_NT = (((1,), (1,)), ((), ()))


LOG2_E = math.log2(math.e)


def _with_ones(v):
    v = v.astype(BF16)
    return jnp.concatenate([v, jnp.ones_like(v)], axis=1)


def _softmax_pv(q, keys, vals1):
    logits = [lax.dot_general(q, k, _NT, preferred_element_type=F32) for k in keys]
    m = functools.reduce(jnp.maximum, [jnp.max(s, axis=-1, keepdims=True) for s in logits])
    acc = None
    for s, v1 in zip(logits, vals1):
        pv = jnp.dot(jnp.exp2(s - m).astype(BF16), v1, preferred_element_type=F32)
        acc = pv if acc is None else acc + pv
    return acc[:, :LANES] / acc[:, LANES:]


def _attn_a_kernel(*refs, kv_heads, has_cache):
    if has_cache:
        q_ref, kc_ref, vc_ref, kn_ref, vn_ref, o_ref = refs
    else:
        q_ref, kn_ref, vn_ref, o_ref = refs
    for kv in range(kv_heads):
        ks = slice(kv * LANES, (kv + 1) * LANES)
        keys = [kn_ref[:, ks].astype(BF16)]
        vals1 = [_with_ones(vn_ref[:, ks])]
        if has_cache:
            keys.append(kc_ref[:, ks].astype(BF16))
            vals1.append(_with_ones(vc_ref[:, ks]))
        for g in range(A_REP):
            sl = slice((kv * A_REP + g) * LANES, (kv * A_REP + g + 1) * LANES)
            o_ref[:, sl] = _softmax_pv(q_ref[:, sl], keys, vals1).astype(o_ref.dtype)


def _attn_a(q, k, v, cache_k, cache_v):
    tq = SEQ
    kvs = A_KV_HEADS
    kw, gw = kvs * LANES, kvs * A_REP * LANES
    o_ctx = pl.pallas_call(
        functools.partial(_attn_a_kernel, kv_heads=kvs, has_cache=False),
        grid=(BATCH, A_KV_HEADS // kvs),
        in_specs=[
            pl.BlockSpec((tq, gw), lambda b, h: (b, h)),
            pl.BlockSpec((SEQ, kw), lambda b, h: (b, h)),
            pl.BlockSpec((SEQ, kw), lambda b, h: (b, h)),
        ],
        out_specs=pl.BlockSpec((tq, gw), lambda b, h: (b, h)),
        out_shape=jax.ShapeDtypeStruct((N_CTX, D_MODEL), BF16),
        compiler_params=pltpu.CompilerParams(
            dimension_semantics=("arbitrary", "arbitrary"),
            vmem_limit_bytes=_vmem_limit(4 * tq * gw * 2 + 4 * SEQ * kw * 2 + 4 * kvs * A_REP * tq * SEQ * 4)),
        name="attn_a_ctx",
    )(q, k, v)
    tq = 512
    kvs = A_KV_HEADS
    kw, gw = kvs * LANES, kvs * A_REP * LANES
    nq = DEC_SEQ // tq
    row0 = N_CTX // tq
    t_all = PAST_LEN + DEC_SEQ
    o_lat = pl.pallas_call(
        functools.partial(_attn_a_kernel, kv_heads=kvs, has_cache=True),
        grid=(DEC_BATCH, A_KV_HEADS // kvs, nq),
        in_specs=[
            pl.BlockSpec((tq, gw), lambda b, h, i: (row0 + b * nq + i, h)),
            pl.BlockSpec((None, PAST_LEN, kw), lambda b, h, i: (b, 0, h)),
            pl.BlockSpec((None, PAST_LEN, kw), lambda b, h, i: (b, 0, h)),
            pl.BlockSpec((DEC_SEQ, kw), lambda b, h, i: (N_CTX // DEC_SEQ + b, h)),
            pl.BlockSpec((DEC_SEQ, kw), lambda b, h, i: (N_CTX // DEC_SEQ + b, h)),
        ],
        out_specs=pl.BlockSpec((tq, gw), lambda b, h, i: (b * nq + i, h)),
        out_shape=jax.ShapeDtypeStruct((N_LAT, D_MODEL), BF16),
        compiler_params=pltpu.CompilerParams(
            dimension_semantics=("arbitrary", "arbitrary", "arbitrary"),
            vmem_limit_bytes=_vmem_limit(4 * tq * gw * 2 + 4 * t_all * kw * 4
                                         + 2 * kvs * A_REP * tq * t_all * 4)),
        name="attn_a_lat",
    )(q, cache_k, cache_v, k, v)
    return o_ctx, o_lat


def _attn_b_kernel(*refs, heads, has_cache, lam_init):
    if has_cache:
        q_ref, kc_ref, vc_ref, kn_ref, vn_ref, lam_ref, sub_ref, o_ref = refs
    else:
        q_ref, kn_ref, vn_ref, lam_ref, sub_ref, o_ref = refs
    lp = lam_ref[...]
    lam = (jnp.exp(jnp.sum(lp[0:1] * lp[1:2], axis=-1, keepdims=True))
           - jnp.exp(jnp.sum(lp[2:3] * lp[3:4], axis=-1, keepdims=True)) + lam_init)
    first = lax.broadcasted_iota(jnp.int32, (1, LANES), 1) < B_QK_DIM
    for h in range(heads):
        sl = slice(h * LANES, (h + 1) * LANES)
        q = q_ref[:, sl]
        zero = jnp.zeros_like(q)
        keys = [kn_ref[:, sl].astype(BF16)]
        vals1 = [_with_ones(vn_ref[:, sl])]
        if has_cache:
            keys.append(kc_ref[:, sl].astype(BF16))
            vals1.append(_with_ones(vc_ref[:, sl]))
        o1 = _softmax_pv(jnp.where(first, q, zero), keys, vals1)
        o2 = _softmax_pv(jnp.where(first, zero, q), keys, vals1)
        o = o1 - lam * o2
        o = o * lax.rsqrt(jnp.mean(o * o, axis=-1, keepdims=True) + EPS) * sub_ref[...] * (1.0 - lam_init)
        o_ref[:, sl] = o.astype(o_ref.dtype)


def _attn_b(q, k, v, cache_k, cache_v, lam_p, subln, lam_init):
    heads = 8
    gw = heads * LANES
    tq = SEQ
    o_ctx = pl.pallas_call(
        functools.partial(_attn_b_kernel, heads=heads, has_cache=False, lam_init=lam_init),
        grid=(BATCH, B_HEADS // heads),
        in_specs=[
            pl.BlockSpec((tq, gw), lambda b, h: (b, h)),
            pl.BlockSpec((SEQ, gw), lambda b, h: (b, h)),
            pl.BlockSpec((SEQ, gw), lambda b, h: (b, h)),
            pl.BlockSpec((4, B_QK_DIM), lambda b, h: (0, 0)),
            pl.BlockSpec((1, LANES), lambda b, h: (0, 0)),
        ],
        out_specs=pl.BlockSpec((tq, gw), lambda b, h: (b, h)),
        out_shape=jax.ShapeDtypeStruct((N_CTX, D_MODEL), BF16),
        compiler_params=pltpu.CompilerParams(
            dimension_semantics=("arbitrary", "arbitrary"),
            vmem_limit_bytes=_vmem_limit(4 * tq * gw * 2 + 4 * SEQ * gw * 4 + 8 * heads * tq * SEQ * 4)),
        name="attn_b_ctx",
    )(q, k, v, lam_p, subln)
    tq = 512
    nq = DEC_SEQ // tq
    row0 = N_CTX // tq
    t_all = PAST_LEN + DEC_SEQ
    heads = 8
    gw = heads * LANES
    o_lat = pl.pallas_call(
        functools.partial(_attn_b_kernel, heads=heads, has_cache=True, lam_init=lam_init),
        grid=(DEC_BATCH, B_HEADS // heads, nq),
        in_specs=[
            pl.BlockSpec((tq, gw), lambda b, h, i: (row0 + b * nq + i, h)),
            pl.BlockSpec((None, PAST_LEN, gw), lambda b, h, i: (b, 0, h)),
            pl.BlockSpec((None, PAST_LEN, gw), lambda b, h, i: (b, 0, h)),
            pl.BlockSpec((DEC_SEQ, gw), lambda b, h, i: (N_CTX // DEC_SEQ + b, h)),
            pl.BlockSpec((DEC_SEQ, gw), lambda b, h, i: (N_CTX // DEC_SEQ + b, h)),
            pl.BlockSpec((4, B_QK_DIM), lambda b, h, i: (0, 0)),
            pl.BlockSpec((1, LANES), lambda b, h, i: (0, 0)),
        ],
        out_specs=pl.BlockSpec((tq, gw), lambda b, h, i: (b * nq + i, h)),
        out_shape=jax.ShapeDtypeStruct((N_LAT, D_MODEL), BF16),
        compiler_params=pltpu.CompilerParams(
            dimension_semantics=("arbitrary", "arbitrary", "arbitrary"),
            vmem_limit_bytes=_vmem_limit(4 * tq * gw * 2 + 4 * t_all * gw * 4
                                         + 3 * 2 * heads * tq * t_all * 4)),
        name="attn_b_lat",
    )(q, cache_k, cache_v, k, v, lam_p, subln)
    return o_ctx, o_lat


def _diff_lambda_init(layer_idx):
    return 0.8 - 0.6 * math.exp(-0.3 * layer_idx)


def kernel(x_prompt, x_sample, cache_a_k, cache_a_v, cache_b_k, cache_b_v, c, c_ctx, ada_w, ada_b, ln_g, ln_b, ffn_w_in, ffn_w_out, a_w_qkv, a_q_norm, a_k_norm, a_w_o, b_w_qkv, b_lambda, b_subln, b_w_o):
    ln_g3 = ln_g.reshape(DEPTH * N_SUB, 1, D_MODEL)
    ln_b3 = ln_b.reshape(DEPTH * N_SUB, 1, D_MODEL)
    tables_a = _rope_tables(A_HEAD_DIM)
    tables_b = _rope_tables(B_QK_DIM)
    ones = jnp.ones((1, LANES), F32)

    cond = jnp.concatenate([c_ctx[None, :], c, jnp.zeros((COND_ROWS - N_GROUPS, D_MODEL), F32)], axis=0)
    cond_t = cond.T
    mod_first = _modulation(cond, ada_w, ada_b, 1, 2 * D_MODEL)[0]
    mod = []

    x = (x_prompt.reshape(N_CTX, D_MODEL), x_sample.reshape(N_LAT, D_MODEL))
    h = _premod(x[0], x[1], [mod_first], 0, 0)

    new_kv = {"ak": [], "av": [], "bk": [], "bv": []}
    for i in range(DEPTH):
        j = i // N_MIXERS
        last = i + 1 == DEPTH
        w_o_f32 = a_w_o if i % N_MIXERS == 0 else b_w_o
        a, w_down, w_o, own_mod = _ffn_up(h, ffn_w_in, ffn_w_out, i, 0, extra=(w_o_f32, (j,)),
                                          side_mod=(cond_t, ada_w, ada_b, i) if i == 0 else None)
        if i == 0:
            mod.append(own_mod)
        x, h = _proj_deepnorm(a, w_down, (), x, mod, ln_g3, ln_b3, i, 0, MACARON_WEIGHT, (i, 1))
        if i % N_MIXERS == 0:
            qw = A_HEADS * A_HEAD_DIM
            kw = A_KV_HEADS * A_HEAD_DIM
            q = _qkv_proj(h, a_w_qkv, j, 0, qw, a_q_norm[j][None, :], tables_a, True, A_HEAD_DIM,
                          out_scale=A_HEAD_DIM ** -0.5 * LOG2_E)
            k, new_k = _qkv_proj(h, a_w_qkv, j, qw, kw, a_k_norm[j][None, :], tables_a, True, A_HEAD_DIM,
                                 cache_split=1)
            v, new_v = _qkv_proj(h, a_w_qkv, j, qw + kw, kw, ones, None, False, 0, cache_split=1)
            o = _attn_a(q, k, v, cache_a_k[:, j].reshape(DEC_BATCH, PAST_LEN, kw),
                        cache_a_v[:, j].reshape(DEC_BATCH, PAST_LEN, kw))
            new_kv["ak"].append(new_k)
            new_kv["av"].append(new_v)
        else:
            lam_init = _diff_lambda_init(i)
            qw = B_HEADS * 2 * B_QK_DIM
            q = _qkv_proj(h, b_w_qkv, j, 0, qw, ones, tables_b, False, B_QK_DIM,
                          out_scale=B_QK_DIM ** -0.5 * LOG2_E)
            k, new_k = _qkv_proj(h, b_w_qkv, j, qw, qw, ones, tables_b, False, B_QK_DIM, cache_split=2)
            v, new_v = _qkv_proj(h, b_w_qkv, j, 2 * qw, B_HEADS * B_V_DIM, ones, None, False, 0,
                                 cache_split=DENSE)
            new_v = new_v.reshape(BATCH, 1, SEQ, B_HEADS, B_V_DIM)
            o = _attn_b(q, k, v, cache_b_k[:, j].reshape(DEC_BATCH, PAST_LEN, qw),
                        cache_b_v[:, j].reshape(DEC_BATCH, PAST_LEN, B_HEADS * B_V_DIM),
                        b_lambda[j], b_subln[j][None, :], lam_init)
            new_kv["bk"].append(new_k)
            new_kv["bv"].append(new_v)
        x, h = _proj_deepnorm(o, w_o, (), x, mod, ln_g3, ln_b3, i, 1, 1.0, (i, 2))
        a, w_down, _, next_mod = _ffn_up(h, ffn_w_in, ffn_w_out, i, 1,
                                         side_mod=None if last else (cond_t, ada_w, ada_b, i + 1))
        if not last:
            mod.append(next_mod)
        x, h = _proj_deepnorm(a, w_down, (), x, mod, ln_g3, ln_b3, i, 2, MACARON_WEIGHT,
                              None if last else (i + 1, 0), split_out=last)

    y_prompt = x[0].reshape(BATCH, SEQ, D_MODEL)
    y_sample = x[1].reshape(DEC_BATCH, DEC_SEQ, D_MODEL)
    return (y_prompt, y_sample,
            jnp.concatenate(new_kv["ak"], axis=1), jnp.concatenate(new_kv["av"], axis=1),
            jnp.concatenate(new_kv["bk"], axis=1), jnp.concatenate(new_kv["bv"], axis=1))
```
